```python
import jax, jax.numpy as jnp
from jax import lax
import numpy as np

D_MODEL = 1024
BATCH = 2
SEQ = 8192
DEPTH = 2

N_A_LAYERS = DEPTH // 2
N_B_LAYERS = DEPTH - N_A_LAYERS
D_FF = 2816
RMS_EPS = 1e-6
S5_GROUP = 16
S5_GROUPS = D_MODEL // S5_GROUP
S5_STATE = 64
S5_DT_MIN = 1e-3
S5_DT_MAX = 1e-1
N_HEADS = 8
HEAD_DIM = D_MODEL // N_HEADS
N_KV_HEADS = 2
KV_GROUP = N_HEADS // N_KV_HEADS
ROPE_DIM = HEAD_DIM // 4
ROPE_THETA = 500000.0
MOBA_BLOCK = 256
MOBA_TOPK = 3
Q_BLOCK = 64
NEG_INF = -1e30

kernel_name = "yoco_s5_moba_macaron"


def rmsnorm(x, g):
    xf = x.astype(jnp.float32)
    y = xf * lax.rsqrt(jnp.mean(xf * xf, axis=-1, keepdims=True) + RMS_EPS)
    return (y * g.astype(jnp.float32)).astype(x.dtype)


def swiglu_ffn(h, w_in, w_out):
    gate, up = jnp.split(h @ w_in, 2, axis=-1)
    return (jax.nn.silu(gate) * up) @ w_out


def partial_rope(x):
    s = x.shape[1]
    pos = jnp.arange(s, dtype=jnp.float32)
    inv_freq = ROPE_THETA ** (-jnp.arange(0, ROPE_DIM, 2, dtype=jnp.float32) / ROPE_DIM)
    ang = pos[:, None] * inv_freq[None, :]
    cos = jnp.cos(ang)[None, :, None, :]
    sin = jnp.sin(ang)[None, :, None, :]
    xr = x[..., :ROPE_DIM].astype(jnp.float32)
    x1, x2 = xr[..., :ROPE_DIM // 2], xr[..., ROPE_DIM // 2:]
    rot = jnp.concatenate([x1 * cos - x2 * sin, x2 * cos + x1 * sin], axis=-1).astype(x.dtype)
    return jnp.concatenate([rot, x[..., ROPE_DIM:]], axis=-1)


def s5_mixer(u, a_re, a_im, log_step, b_re, b_im, c_re, c_im, d_skip, w_glu):
    f32 = jnp.float32
    bsz, s, _ = u.shape
    uf = u.astype(f32).reshape(bsz, s, S5_GROUPS, S5_GROUP)
    dt = jnp.exp(log_step.astype(f32))[:, None]
    lr, li = a_re.astype(f32), a_im.astype(f32)
    mag = jnp.exp(lr * dt)
    abar_re = mag * jnp.cos(li * dt)
    abar_im = mag * jnp.sin(li * dt)
    nr, ni = abar_re - 1.0, abar_im
    den = lr * lr + li * li
    coef_re = (nr * lr + ni * li) / den
    coef_im = (ni * lr - nr * li) / den
    bu_re = jnp.einsum('bsgc,gpc->bsgp', uf, b_re.astype(f32))
    bu_im = jnp.einsum('bsgc,gpc->bsgp', uf, b_im.astype(f32))
    x_re = coef_re * bu_re - coef_im * bu_im
    x_im = coef_re * bu_im + coef_im * bu_re
    a_re_t = jnp.broadcast_to(abar_re, (1, s) + abar_re.shape)
    a_im_t = jnp.broadcast_to(abar_im, (1, s) + abar_im.shape)

    def combine(left, right):
        ar1, ai1, br1, bi1 = left
        ar2, ai2, br2, bi2 = right
        return (ar2 * ar1 - ai2 * ai1,
                ar2 * ai1 + ai2 * ar1,
                ar2 * br1 - ai2 * bi1 + br2,
                ar2 * bi1 + ai2 * br1 + bi2)

    _, _, h_re, h_im = lax.associative_scan(combine, (a_re_t, a_im_t, x_re, x_im), axis=1)
    y = (jnp.einsum('bsgp,gcp->bsgc', h_re, c_re.astype(f32))
         - jnp.einsum('bsgp,gcp->bsgc', h_im, c_im.astype(f32)))
    y = y.reshape(bsz, s, D_MODEL) + d_skip.astype(f32) * u.astype(f32)
    y = jax.nn.gelu(y).astype(u.dtype)
    val, gate = jnp.split(y @ w_glu, 2, axis=-1)
    return val * jax.nn.sigmoid(gate)


def shared_kv(x, g, w_k, w_v):
    h = rmsnorm(x, g)
    bsz, s, _ = x.shape
    k = partial_rope((h @ w_k).reshape(bsz, s, N_KV_HEADS, HEAD_DIM))
    v = (h @ w_v).reshape(bsz, s, N_KV_HEADS, HEAD_DIM)
    n_blk = -(-s // MOBA_BLOCK)
    pad = n_blk * MOBA_BLOCK - s
    k = jnp.pad(k, ((0, 0), (0, pad), (0, 0), (0, 0)))
    v = jnp.pad(v, ((0, 0), (0, pad), (0, 0), (0, 0)))
    k_blocks = k.reshape(bsz, n_blk, MOBA_BLOCK, N_KV_HEADS, HEAD_DIM).transpose(0, 3, 1, 2, 4)
    v_blocks = v.reshape(bsz, n_blk, MOBA_BLOCK, N_KV_HEADS, HEAD_DIM).transpose(0, 3, 1, 2, 4)
    k_mean = jnp.mean(k_blocks.astype(jnp.float32), axis=3).astype(k.dtype)
    return k_blocks, v_blocks, k_mean


def moba_mixer(h, w_q, w_o, k_blocks, v_blocks, k_mean):
    bsz, s, _ = h.shape
    n_blk = k_blocks.shape[2]
    top_k = min(MOBA_TOPK, n_blk)
    n_qblk = s // Q_BLOCK
    q = partial_rope((h @ w_q).reshape(bsz, s, N_HEADS, HEAD_DIM)) * (HEAD_DIM ** -0.5)
    q = q.reshape(bsz, n_qblk, Q_BLOCK, N_KV_HEADS, KV_GROUP, HEAD_DIM).transpose(1, 0, 3, 4, 2, 5)
    b_idx = jnp.arange(bsz)[:, None, None, None, None]
    h_idx = jnp.arange(N_KV_HEADS)[None, :, None, None, None]
    key_off = jnp.arange(MOBA_BLOCK)
    blk_ids = jnp.arange(n_blk)
    sel_slot = jnp.arange(top_k)

    def attend_block(args):
        qb, i = args
        q_pos = i * Q_BLOCK + jnp.arange(Q_BLOCK)
        own = (i * Q_BLOCK) // MOBA_BLOCK
        k_own = lax.dynamic_index_in_dim(k_blocks, own, axis=2, keepdims=False)
        v_own = lax.dynamic_index_in_dim(v_blocks, own, axis=2, keepdims=False)
        s_own = jnp.einsum('bhgqd,bhkd->bhgqk', qb, k_own).astype(jnp.float32)
        causal = (own * MOBA_BLOCK + key_off)[None, :] <= q_pos[:, None]
        s_own = jnp.where(causal, s_own, NEG_INF)
        gate = jnp.einsum('bhgqd,bhnd->bhgqn', qb, k_mean).astype(jnp.float32)
        gate = jnp.where(blk_ids < own, gate, NEG_INF)
        _, idx = lax.top_k(gate, top_k)
        k_sel = k_blocks[b_idx, h_idx, idx]
        v_sel = v_blocks[b_idx, h_idx, idx]
        s_sel = jnp.einsum('bhgqd,bhgqjkd->bhgqjk', qb, k_sel).astype(jnp.float32)
        s_sel = jnp.where((sel_slot < own)[:, None], s_sel, NEG_INF)
        scores = jnp.concatenate(
            [s_own, s_sel.reshape(s_sel.shape[:4] + (top_k * MOBA_BLOCK,))], axis=-1)
        p = jax.nn.softmax(scores, axis=-1).astype(v_blocks.dtype)
        p_own = p[..., :MOBA_BLOCK]
        p_sel = p[..., MOBA_BLOCK:].reshape(s_sel.shape)
        return (jnp.einsum('bhgqk,bhkd->bhgqd', p_own, v_own)
                + jnp.einsum('bhgqjk,bhgqjkd->bhgqd', p_sel, v_sel))

    out = lax.map(attend_block, (q, jnp.arange(n_qblk)))
    out = out.transpose(1, 0, 4, 2, 3, 5).reshape(bsz, s, N_HEADS * HEAD_DIM)
    return out @ w_o


def setup_inputs(seed: int = 0) -> dict:
    key = jax.random.key(seed)
    ks = jax.random.split(key, 24)
    f32 = jnp.float32

    def nrm(k, shape, scale):
        return jax.random.normal(k, shape, f32) * scale

    x = jax.random.normal(ks[0], (BATCH, SEQ, D_MODEL), f32)
    norm_g = 1.0 + nrm(ks[1], (DEPTH, 3, D_MODEL), 0.02)
    ffn_w_in = nrm(ks[2], (DEPTH, 2, D_MODEL, 2 * D_FF), D_MODEL ** -0.5)
    ffn_w_out = nrm(ks[3], (DEPTH, 2, D_FF, D_MODEL), D_FF ** -0.5)
    n_idx = jnp.arange(S5_STATE, dtype=f32)
    s5_a_re = -0.5 + nrm(ks[4], (N_A_LAYERS, S5_GROUPS, S5_STATE), 0.01)
    s5_a_im = jnp.pi * n_idx + nrm(ks[5], (N_A_LAYERS, S5_GROUPS, S5_STATE), 0.01)
    s5_log_step = jax.random.uniform(ks[6], (N_A_LAYERS, S5_GROUPS), f32,
                                     float(np.log(S5_DT_MIN)), float(np.log(S5_DT_MAX)))
    b_scale = (2.0 * S5_GROUP) ** -0.5
    s5_b_re = nrm(ks[7], (N_A_LAYERS, S5_GROUPS, S5_STATE, S5_GROUP), b_scale)
    s5_b_im = nrm(ks[8], (N_A_LAYERS, S5_GROUPS, S5_STATE, S5_GROUP), b_scale)
    c_scale = S5_STATE ** -0.5
    s5_c_re = nrm(ks[9], (N_A_LAYERS, S5_GROUPS, S5_GROUP, S5_STATE), c_scale)
    s5_c_im = nrm(ks[10], (N_A_LAYERS, S5_GROUPS, S5_GROUP, S5_STATE), c_scale)
    s5_d = nrm(ks[11], (N_A_LAYERS, D_MODEL), 1.0)
    s5_w_glu = nrm(ks[12], (N_A_LAYERS, D_MODEL, 2 * D_MODEL), D_MODEL ** -0.5)
    kv_norm_g = 1.0 + nrm(ks[13], (D_MODEL,), 0.02)
    w_k = nrm(ks[14], (D_MODEL, N_KV_HEADS * HEAD_DIM), D_MODEL ** -0.5)
    w_v = nrm(ks[15], (D_MODEL, N_KV_HEADS * HEAD_DIM), D_MODEL ** -0.5)
    w_q = nrm(ks[16], (N_B_LAYERS, D_MODEL, N_HEADS * HEAD_DIM), D_MODEL ** -0.5)
    w_o = nrm(ks[17], (N_B_LAYERS, N_HEADS * HEAD_DIM, D_MODEL), (N_HEADS * HEAD_DIM) ** -0.5)
    final_g = 1.0 + nrm(ks[18], (D_MODEL,), 0.02)
    return {"x": x, "norm_g": norm_g, "ffn_w_in": ffn_w_in, "ffn_w_out": ffn_w_out,
            "s5_a_re": s5_a_re, "s5_a_im": s5_a_im, "s5_log_step": s5_log_step,
            "s5_b_re": s5_b_re, "s5_b_im": s5_b_im, "s5_c_re": s5_c_re, "s5_c_im": s5_c_im,
            "s5_d": s5_d, "s5_w_glu": s5_w_glu, "kv_norm_g": kv_norm_g, "w_k": w_k, "w_v": w_v,
            "w_q": w_q, "w_o": w_o, "final_g": final_g}


def reference(x, norm_g, ffn_w_in, ffn_w_out, s5_a_re, s5_a_im, s5_log_step,
              s5_b_re, s5_b_im, s5_c_re, s5_c_im, s5_d, s5_w_glu, kv_norm_g, w_k, w_v,
              w_q, w_o, final_g):
    kv = None
    for layer in range(DEPTH):
        if layer == N_A_LAYERS:
            kv = shared_kv(x, kv_norm_g, w_k, w_v)
        x = x + 0.5 * swiglu_ffn(rmsnorm(x, norm_g[layer, 0]), ffn_w_in[layer, 0], ffn_w_out[layer, 0])
        h = rmsnorm(x, norm_g[layer, 1])
        if layer < N_A_LAYERS:
            j = layer
            x = x + s5_mixer(h, s5_a_re[j], s5_a_im[j], s5_log_step[j], s5_b_re[j], s5_b_im[j],
                             s5_c_re[j], s5_c_im[j], s5_d[j], s5_w_glu[j])
        else:
            j = layer - N_A_LAYERS
            x = x + moba_mixer(h, w_q[j], w_o[j], kv[0], kv[1], kv[2])
        x = x + 0.5 * swiglu_ffn(rmsnorm(x, norm_g[layer, 2]), ffn_w_in[layer, 1], ffn_w_out[layer, 1])
    return rmsnorm(x, final_g)
```

```python
import functools

import jax
import jax.numpy as jnp
import numpy as np
from jax import lax
from jax.experimental import pallas as pl
from jax.experimental.pallas import tpu as pltpu

F32 = jnp.float32
BF16 = jnp.bfloat16

D_MODEL = 1024
D_FF = 2816
RMS_EPS = 1e-6
S5_GROUP = 16
S5_GROUPS = D_MODEL // S5_GROUP
S5_STATE = 64
N_HEADS = 8
HEAD_DIM = 128
N_KV_HEADS = 2
KV_GROUP = N_HEADS // N_KV_HEADS
ROPE_DIM = HEAD_DIM // 4
ROPE_THETA = 500000.0
MOBA_BLOCK = 256
MOBA_TOPK = 3
NEG_INF = -1e30

LANES = 128
SUBLANES = 8
VMEM_LIMIT = 56 * 1024 * 1024

FFN_ROWS = 512
FFN_COLS = 256

S5_CHUNK = 256
S5_SUBSEQ = SUBLANES
S5_STEPS = S5_CHUNK // S5_SUBSEQ
S5_LANE_BLOCKS = D_MODEL // LANES
S5_GROUPS_PER_BLOCK = LANES // S5_GROUP
S5_HALF = S5_GROUPS_PER_BLOCK * S5_STATE
S5_NSTATE = S5_GROUPS * S5_STATE


def _rms(x, g):
    ms = jnp.mean(x * x, axis=-1, keepdims=True)
    return (x * lax.rsqrt(ms + RMS_EPS)) * g


def _dot(a, b):
    return jnp.dot(a, b, preferred_element_type=F32)


def _resident(shape):
    nd = len(shape)
    return pl.BlockSpec(shape, lambda *_: (0,) * nd, pipeline_mode=pl.Buffered(1))


def _ffn_body(*refs, final):
    if final:
        x_ref, g_ref, win_ref, wout_ref, fg_ref, o_ref = refs
    else:
        x_ref, g_ref, win_ref, wout_ref, o_ref = refs
    x = x_ref[...]
    h = _rms(x, g_ref[...]).astype(BF16)
    acc = jnp.zeros(x.shape, F32)
    for c in range(D_FF // FFN_COLS):
        lo = c * FFN_COLS
        gate = _dot(h, win_ref[:, lo:lo + FFN_COLS])
        up = _dot(h, win_ref[:, D_FF + lo:D_FF + lo + FFN_COLS])
        act = (gate * jax.nn.sigmoid(gate)) * up
        acc = acc + _dot(act.astype(BF16), wout_ref[lo:lo + FFN_COLS, :])
    y = x + 0.5 * acc
    if final:
        y = _rms(y, fg_ref[...])
    o_ref[...] = y


def _ffn_call(x2, g, w_in, w_out, final_g=None):
    t = x2.shape[0]
    final = final_g is not None
    row_spec = pl.BlockSpec((FFN_ROWS, D_MODEL), lambda i: (i, 0))
    in_specs = [row_spec, _resident((1, D_MODEL)), _resident((D_MODEL, 2 * D_FF)),
                _resident((D_FF, D_MODEL))]
    args = [x2, g.reshape(1, D_MODEL), w_in.astype(BF16), w_out.astype(BF16)]
    if final:
        in_specs.append(_resident((1, D_MODEL)))
        args.append(final_g.reshape(1, D_MODEL))
    return pl.pallas_call(
        functools.partial(_ffn_body, final=final),
        grid=(t // FFN_ROWS,),
        in_specs=in_specs,
        out_specs=row_spec,
        out_shape=jax.ShapeDtypeStruct((t, D_MODEL), F32),
        compiler_params=pltpu.CompilerParams(
            dimension_semantics=("arbitrary",), vmem_limit_bytes=VMEM_LIMIT),
        name="ffn_final" if final else "ffn",
    )(*args)


def _s5_prep_body(lr_ref, li_ref, ls_ref, bre_ref, bim_ref, cim_ref,
                  ar_ref, ai_ref, amr_ref, ami_ref, pr_ref, pi_ref, btr_ref, bti_ref, cneg_ref):
    lr = lr_ref[...]
    li = li_ref[...]
    dt = jnp.exp(ls_ref[...])
    mag = jnp.exp(lr * dt)
    abar_re = mag * jnp.cos(li * dt)
    abar_im = mag * jnp.sin(li * dt)
    ar_ref[...] = abar_re
    ai_ref[...] = abar_im
    nr, ni = abar_re - 1.0, abar_im
    den = lr * lr + li * li
    coef_re = (nr * lr + ni * li) / den
    coef_im = (ni * lr - nr * li) / den
    k = (lax.broadcasted_iota(jnp.int32, (S5_STEPS, 1), 0) + 1).astype(F32)
    pmag = jnp.exp((lr * dt) * k)
    pang = (li * dt) * k
    pr = pmag * jnp.cos(pang)
    pi = pmag * jnp.sin(pang)
    pr_ref[...] = pr
    pi_ref[...] = pi
    amr_ref[...] = pr[S5_STEPS - 1:S5_STEPS, :]
    ami_ref[...] = pi[S5_STEPS - 1:S5_STEPS, :]
    bre = bre_ref[...]
    bim = bim_ref[...]
    btr_ref[...] = coef_re * bre - coef_im * bim
    bti_ref[...] = coef_re * bim + coef_im * bre
    cneg_ref[...] = -cim_ref[...]


def _s5_prep_call(a_re, a_im, log_step, b_re, b_im, c_im):
    n = S5_NSTATE
    row = lambda v: v.reshape(1, n)
    chan_major = lambda v: v.transpose(2, 0, 1).reshape(S5_GROUP, n)
    ls = jnp.repeat(log_step, S5_STATE)
    outs = pl.pallas_call(
        _s5_prep_body,
        out_shape=[jax.ShapeDtypeStruct((1, n), F32)] * 4
        + [jax.ShapeDtypeStruct((S5_STEPS, n), F32)] * 2
        + [jax.ShapeDtypeStruct((S5_GROUP, n), F32)] * 3,
        name="s5_prep",
    )(row(a_re), row(a_im), row(ls), chan_major(b_re), chan_major(b_im),
      c_im.transpose(1, 0, 2).reshape(S5_GROUP, n))
    return outs


def _s5_body(x_ref, g_ref, perm_ref, permt_ref, bblk_ref, cblk_ref, ar_ref, ai_ref,
             amr_ref, ami_ref, pr_ref, pi_ref, d_ref, wglu_ref, o_ref,
             xs_ref, st_ref, c_ref):
    @pl.when(pl.program_id(1) == 0)
    def _():
        st_ref[...] = jnp.zeros(st_ref.shape, F32)

    x = x_ref[...]
    u = _rms(x, g_ref[...])
    u_hi = u.astype(BF16)
    u_lo = (u - u_hi.astype(F32)).astype(BF16)
    perm = perm_ref[...]
    up_hi = _dot(perm, u_hi)
    up = up_hi + _dot(perm, u_lo)
    ub = up_hi.astype(BF16)

    last = SUBLANES * (S5_STEPS - 1)
    ys = []
    for j in range(S5_LANE_BLOCKS):
        xs_ref[j] = _dot(ub[:, j * LANES:(j + 1) * LANES], bblk_ref[j])

        a_r = jnp.broadcast_to(ar_ref[j], (SUBLANES, S5_HALF))
        a_i = jnp.broadcast_to(ai_ref[j], (SUBLANES, S5_HALF))

        def scan_step(k, carry, j=j, a_r=a_r, a_i=a_i):
            h_r, h_i = carry
            rows = pl.ds(pl.multiple_of(k * SUBLANES, SUBLANES), SUBLANES)
            n_r = a_r * h_r - a_i * h_i + xs_ref[j, rows, 0:S5_HALF]
            n_i = a_r * h_i + a_i * h_r + xs_ref[j, rows, S5_HALF:2 * S5_HALF]
            xs_ref[j, rows, 0:S5_HALF] = n_r
            xs_ref[j, rows, S5_HALF:2 * S5_HALF] = n_i
            return n_r, n_i

        zero = jnp.zeros((SUBLANES, S5_HALF), F32)
        lax.fori_loop(0, S5_STEPS, scan_step, (zero, zero), unroll=4)

        am_r = amr_ref[j]
        am_i = ami_ref[j]
        c_r = st_ref[j, :, 0:S5_HALF]
        c_i = st_ref[j, :, S5_HALF:2 * S5_HALF]
        for i in range(S5_SUBSEQ):
            c_ref[i:i + 1, 0:S5_HALF] = c_r
            c_ref[i:i + 1, S5_HALF:2 * S5_HALF] = c_i
            e_r = xs_ref[j, last + i:last + i + 1, 0:S5_HALF]
            e_i = xs_ref[j, last + i:last + i + 1, S5_HALF:2 * S5_HALF]
            c_r, c_i = am_r * c_r - am_i * c_i + e_r, am_r * c_i + am_i * c_r + e_i
        st_ref[j, :, 0:S5_HALF] = c_r
        st_ref[j, :, S5_HALF:2 * S5_HALF] = c_i

        cc_r = c_ref[:, 0:S5_HALF]
        cc_i = c_ref[:, S5_HALF:2 * S5_HALF]

        def fix_step(k, carry, j=j, cc_r=cc_r, cc_i=cc_i):
            rows = pl.ds(pl.multiple_of(k * SUBLANES, SUBLANES), SUBLANES)
            p_r = pr_ref[j, pl.ds(k, 1), :]
            p_i = pi_ref[j, pl.ds(k, 1), :]
            xs_ref[j, rows, 0:S5_HALF] = xs_ref[j, rows, 0:S5_HALF] + (p_r * cc_r - p_i * cc_i)
            xs_ref[j, rows, S5_HALF:2 * S5_HALF] = (
                xs_ref[j, rows, S5_HALF:2 * S5_HALF] + (p_r * cc_i + p_i * cc_r))
            return carry

        lax.fori_loop(0, S5_STEPS, fix_step, 0, unroll=4)

        ys.append(_dot(xs_ref[j].astype(BF16), cblk_ref[j]))

    y = jnp.concatenate(ys, axis=1) + d_ref[...] * up
    y = 0.5 * y * (1.0 + jnp.tanh(np.sqrt(2.0 / np.pi).astype(np.float32)
                                  * (y + 0.044715 * (y * y * y))))
    yn = _dot(permt_ref[...], y.astype(BF16)).astype(BF16)
    z = _dot(yn, wglu_ref[...])
    o_ref[...] = x + z[:, 0:D_MODEL] * jax.nn.sigmoid(z[:, D_MODEL:2 * D_MODEL])


def _s5_call(x2, bsz, seq, g, prep, b_unused, c_re, d_skip, w_glu):
    del b_unused
    ar, ai, amr, ami, pr, pi, btr, bti, cneg = prep
    nb, gb, ns = S5_LANE_BLOCKS, S5_GROUPS_PER_BLOCK, S5_STATE
    eye = jnp.eye(gb, dtype=F32)

    def per_block(v, rows):
        return v.reshape(rows, nb, S5_HALF).transpose(1, 0, 2)

    bt = jnp.stack([btr, bti], axis=1).reshape(S5_GROUP, 2, nb, gb, ns)
    bblk = jnp.einsum('crjgp,gh->jgcrhp', bt, eye).reshape(nb, LANES, 2 * S5_HALF).astype(BF16)
    cmat = jnp.stack([c_re.transpose(1, 0, 2).reshape(S5_GROUP, S5_NSTATE), cneg], axis=1)
    cmat = cmat.reshape(S5_GROUP, 2, nb, gb, ns)
    cblk = jnp.einsum('crjgp,gh->jrgphc', cmat, eye).reshape(nb, 2 * S5_HALF, LANES).astype(BF16)

    r = np.arange(S5_CHUNK)
    perm_np = np.zeros((S5_CHUNK, S5_CHUNK), np.float32)
    perm_np[r, (r % SUBLANES) * S5_STEPS + r // SUBLANES] = 1.0
    perm = jnp.asarray(perm_np, BF16)
    permt = jnp.asarray(perm_np.T, BF16)

    nchunk = seq // S5_CHUNK
    row_spec = pl.BlockSpec((S5_CHUNK, D_MODEL), lambda b, c: (b * nchunk + c, 0))
    in_specs = [
        row_spec, _resident((1, D_MODEL)),
        _resident((S5_CHUNK, S5_CHUNK)), _resident((S5_CHUNK, S5_CHUNK)),
        _resident((nb, LANES, 2 * S5_HALF)), _resident((nb, 2 * S5_HALF, LANES)),
        _resident((nb, 1, S5_HALF)), _resident((nb, 1, S5_HALF)),
        _resident((nb, 1, S5_HALF)), _resident((nb, 1, S5_HALF)),
        _resident((nb, S5_STEPS, S5_HALF)), _resident((nb, S5_STEPS, S5_HALF)),
        _resident((1, D_MODEL)), _resident((D_MODEL, 2 * D_MODEL)),
    ]
    return pl.pallas_call(
        _s5_body,
        grid=(bsz, nchunk),
        in_specs=in_specs,
        out_specs=row_spec,
        out_shape=jax.ShapeDtypeStruct(x2.shape, F32),
        scratch_shapes=[
            pltpu.VMEM((nb, S5_CHUNK, 2 * S5_HALF), F32),
            pltpu.VMEM((nb, 1, 2 * S5_HALF), F32),
            pltpu.VMEM((S5_SUBSEQ, 2 * S5_HALF), F32),
        ],
        compiler_params=pltpu.CompilerParams(
            dimension_semantics=("arbitrary", "arbitrary"), vmem_limit_bytes=VMEM_LIMIT),
        name="s5",
    )(x2, g.reshape(1, D_MODEL), perm, permt, bblk, cblk,
      per_block(ar, 1), per_block(ai, 1), per_block(amr, 1), per_block(ami, 1),
      per_block(pr, S5_STEPS), per_block(pi, S5_STEPS),
      d_skip.reshape(1, D_MODEL), w_glu.astype(BF16))


def _rope_tables(pos0, invf, sign):
    pos = (pos0 + lax.broadcasted_iota(jnp.int32, (MOBA_BLOCK, 1), 0)).astype(F32)
    ang = pos * invf
    return jnp.cos(ang), jnp.sin(ang) * sign


def _rope_head(xh, cos_t, sin_t, low_half):
    half = ROPE_DIM // 2
    swapped = jnp.where(low_half, pltpu.roll(xh, LANES - half, axis=1), pltpu.roll(xh, half, axis=1))
    return xh * cos_t + swapped * sin_t


def _rope_consts():
    half = ROPE_DIM // 2
    inv_freq = ROPE_THETA ** (-jnp.arange(0, ROPE_DIM, 2, dtype=F32) / ROPE_DIM)
    pad = jnp.zeros((HEAD_DIM - ROPE_DIM,), F32)
    invf = jnp.concatenate([inv_freq, inv_freq, pad]).reshape(1, HEAD_DIM)
    sign = jnp.concatenate([-jnp.ones((half,), F32), jnp.ones((half,), F32), pad]).reshape(1, HEAD_DIM)
    return invf, sign


def _kv_body(x_ref, g_ref, wkv_ref, invf_ref, sign_ref, kt_ref, v_ref, km_ref, *, nblk):
    blk = pl.program_id(0) % nblk
    h = _rms(x_ref[...], g_ref[...]).astype(BF16)
    kv = _dot(h, wkv_ref[...])
    kdim = N_KV_HEADS * HEAD_DIM
    v_ref[...] = kv[:, kdim:2 * kdim].astype(BF16)
    cos_t, sin_t = _rope_tables(blk * MOBA_BLOCK, invf_ref[...], sign_ref[...])
    lane = lax.broadcasted_iota(jnp.int32, (MOBA_BLOCK, HEAD_DIM), 1)
    low_half = lane < ROPE_DIM // 2
    onehot = jnp.where(lane == blk, 1.0, 0.0).astype(F32)
    means = []
    for hh in range(N_KV_HEADS):
        kh = _rope_head(kv[:, hh * HEAD_DIM:(hh + 1) * HEAD_DIM], cos_t, sin_t, low_half)
        means.append(jnp.mean(kh, axis=0, keepdims=True))
        kt_ref[0, hh] = jnp.concatenate([kh, onehot], axis=1).T.astype(BF16)
    km_ref[0] = jnp.concatenate(means, axis=1)


def _kv_call(x2, seq, g, w_k, w_v, invf, sign):
    t = x2.shape[0]
    ntile = t // MOBA_BLOCK
    kdim = N_KV_HEADS * HEAD_DIM
    wkv = jnp.concatenate([w_k, w_v], axis=1).astype(BF16)
    return pl.pallas_call(
        functools.partial(_kv_body, nblk=seq // MOBA_BLOCK),
        grid=(ntile,),
        in_specs=[pl.BlockSpec((MOBA_BLOCK, D_MODEL), lambda i: (i, 0)),
                  _resident((1, D_MODEL)), _resident((D_MODEL, 2 * kdim)),
                  _resident((1, HEAD_DIM)), _resident((1, HEAD_DIM))],
        out_specs=[pl.BlockSpec((1, N_KV_HEADS, 2 * HEAD_DIM, MOBA_BLOCK), lambda i: (i, 0, 0, 0)),
                   pl.BlockSpec((MOBA_BLOCK, kdim), lambda i: (i, 0)),
                   pl.BlockSpec((1, 1, kdim), lambda i: (i, 0, 0))],
        out_shape=[jax.ShapeDtypeStruct((ntile, N_KV_HEADS, 2 * HEAD_DIM, MOBA_BLOCK), BF16),
                   jax.ShapeDtypeStruct((t, kdim), BF16),
                   jax.ShapeDtypeStruct((ntile, 1, kdim), F32)],
        compiler_params=pltpu.CompilerParams(
            dimension_semantics=("arbitrary",), vmem_limit_bytes=VMEM_LIMIT),
        name="kv_proj",
    )(x2, g.reshape(1, D_MODEL), wkv, invf, sign)


def _split_bf16(v):
    hi = v.astype(BF16)
    return hi, (v - hi.astype(F32)).astype(BF16)


def _attn_body(x_ref, g_ref, wq_ref, wo_ref, kt_ref, v_ref, kmt_ref, invf_ref, sign_ref, o_ref,
               m_ref, l_ref, acc_ref, qa_ref):
    own = pl.program_id(1)
    rows = KV_GROUP * MOBA_BLOCK
    x = x_ref[0]
    h = _rms(x, g_ref[...]).astype(BF16)
    q = _dot(h, wq_ref[...])
    cos_t, sin_t = _rope_tables(own * MOBA_BLOCK, invf_ref[...], sign_ref[...])
    lane_blk = lax.broadcasted_iota(jnp.int32, (MOBA_BLOCK, HEAD_DIM), 1)
    low_half = lane_blk < ROPE_DIM // 2
    scale = HEAD_DIM ** -0.5
    qh = [_rope_head(q[:, i * HEAD_DIM:(i + 1) * HEAD_DIM], cos_t, sin_t, low_half) * scale
          for i in range(N_HEADS)]

    lane = lax.broadcasted_iota(jnp.int32, (rows, LANES), 1)
    lane_f = lane.astype(F32)
    past = lane < own
    r_loc = lax.broadcasted_iota(jnp.int32, (rows, MOBA_BLOCK), 0) % MOBA_BLOCK
    causal = lax.broadcasted_iota(jnp.int32, (rows, MOBA_BLOCK), 1) <= r_loc

    outs = []
    for kh in range(N_KV_HEADS):
        qg = jnp.concatenate(qh[kh * KV_GROUP:(kh + 1) * KV_GROUP], axis=0)
        q_hi, q_lo = _split_bf16(qg)
        k_hi, k_lo = _split_bf16(kmt_ref[0, kh])
        gate = _dot(q_hi, k_hi) + (_dot(q_lo, k_hi) + _dot(q_hi, k_lo))
        cur = jnp.where(past, gate, -jnp.inf)
        bias = jnp.where(lane == own, 0.0, NEG_INF)
        for _ in range(MOBA_TOPK):
            best = jnp.max(cur, axis=1, keepdims=True)
            cand = jnp.where((cur == best) & (best > -jnp.inf), lane_f, float(LANES))
            pick = lane_f == jnp.min(cand, axis=1, keepdims=True)
            bias = jnp.where(pick, 0.0, bias)
            cur = jnp.where(pick, -jnp.inf, cur)
        qa_ref[...] = jnp.concatenate([q_hi, bias.astype(BF16)], axis=1)

        def scores(n, kh=kh):
            s = _dot(qa_ref[...], kt_ref[0, n, kh])
            return s[:, 0:LANES], s[:, LANES:2 * LANES]

        def values(n, kh=kh):
            start = pl.multiple_of(n * MOBA_BLOCK, MOBA_BLOCK)
            return v_ref[0, pl.ds(start, MOBA_BLOCK), kh * HEAD_DIM:(kh + 1) * HEAD_DIM]

        s0, s1 = scores(own)
        s0 = jnp.where(causal[:, 0:LANES], s0, NEG_INF)
        s1 = jnp.where(causal[:, LANES:2 * LANES], s1, NEG_INF)
        m0 = jnp.max(jnp.maximum(s0, s1), axis=1, keepdims=True)
        m_ref[...] = jnp.broadcast_to(m0, (rows, LANES))
        p0 = jnp.exp(s0 - m0)
        p1 = jnp.exp(s1 - m0)
        l_ref[...] = jnp.broadcast_to(jnp.sum(p0 + p1, axis=1, keepdims=True), (rows, LANES))
        acc_ref[...] = _dot(jnp.concatenate([p0, p1], axis=1).astype(BF16), values(own))

        def past_block(n, carry):
            s0, s1 = scores(n)
            m_prev = m_ref[...]
            m_new = jnp.maximum(m_prev, jnp.max(jnp.maximum(s0, s1), axis=1, keepdims=True))
            alpha = jnp.exp(m_prev - m_new)
            p0 = jnp.exp(s0 - m_new)
            p1 = jnp.exp(s1 - m_new)
            l_ref[...] = alpha * l_ref[...] + jnp.sum(p0 + p1, axis=1, keepdims=True)
            acc_ref[...] = alpha * acc_ref[...] + _dot(
                jnp.concatenate([p0, p1], axis=1).astype(BF16), values(n))
            m_ref[...] = m_new
            return carry

        lax.fori_loop(0, own, past_block, 0)
        og = acc_ref[...] / l_ref[...]
        outs.extend(og[i * MOBA_BLOCK:(i + 1) * MOBA_BLOCK] for i in range(KV_GROUP))

    attn = jnp.concatenate(outs, axis=1).astype(BF16)
    o_ref[0] = x + _dot(attn, wo_ref[...])


def _attn_call(x3, g, w_q, w_o, kt, v, kmt, invf, sign):
    bsz, seq, _ = x3.shape
    nblk = seq // MOBA_BLOCK
    kdim = N_KV_HEADS * HEAD_DIM
    rows = KV_GROUP * MOBA_BLOCK
    x_spec = pl.BlockSpec((1, MOBA_BLOCK, D_MODEL), lambda b, i: (b, i, 0))
    return pl.pallas_call(
        _attn_body,
        grid=(bsz, nblk),
        in_specs=[x_spec, _resident((1, D_MODEL)),
                  _resident((D_MODEL, D_MODEL)), _resident((D_MODEL, D_MODEL)),
                  pl.BlockSpec((1, nblk, N_KV_HEADS, 2 * HEAD_DIM, MOBA_BLOCK),
                               lambda b, i: (b, 0, 0, 0, 0)),
                  pl.BlockSpec((1, seq, kdim), lambda b, i: (b, 0, 0)),
                  pl.BlockSpec((1, N_KV_HEADS, HEAD_DIM, LANES), lambda b, i: (b, 0, 0, 0)),
                  _resident((1, HEAD_DIM)), _resident((1, HEAD_DIM))],
        out_specs=x_spec,
        out_shape=jax.ShapeDtypeStruct(x3.shape, F32),
        scratch_shapes=[pltpu.VMEM((rows, LANES), F32), pltpu.VMEM((rows, LANES), F32),
                        pltpu.VMEM((rows, HEAD_DIM), F32), pltpu.VMEM((rows, 2 * HEAD_DIM), BF16)],
        compiler_params=pltpu.CompilerParams(
            dimension_semantics=("arbitrary", "arbitrary"), vmem_limit_bytes=VMEM_LIMIT),
        name="moba_attn",
    )(x3, g.reshape(1, D_MODEL), w_q.astype(BF16), w_o.astype(BF16), kt, v, kmt, invf, sign)


def kernel(x, norm_g, ffn_w_in, ffn_w_out, s5_a_re, s5_a_im, s5_log_step, s5_b_re, s5_b_im,
           s5_c_re, s5_c_im, s5_d, s5_w_glu, kv_norm_g, w_k, w_v, w_q, w_o, final_g):
    bsz, seq, _ = x.shape
    assert seq % S5_CHUNK == 0 and seq % MOBA_BLOCK == 0 and seq // MOBA_BLOCK <= LANES
    assert (bsz * seq) % FFN_ROWS == 0
    nblk = seq // MOBA_BLOCK
    x2 = x.reshape(bsz * seq, D_MODEL)
    invf, sign = _rope_consts()

    x2 = _ffn_call(x2, norm_g[0, 0], ffn_w_in[0, 0], ffn_w_out[0, 0])
    prep = _s5_prep_call(s5_a_re[0], s5_a_im[0], s5_log_step[0], s5_b_re[0], s5_b_im[0], s5_c_im[0])
    x2 = _s5_call(x2, bsz, seq, norm_g[0, 1], prep, None, s5_c_re[0], s5_d[0], s5_w_glu[0])
    x2 = _ffn_call(x2, norm_g[0, 2], ffn_w_in[0, 1], ffn_w_out[0, 1])

    kt, v, km = _kv_call(x2, seq, kv_norm_g, w_k, w_v, invf, sign)
    kt = kt.reshape(bsz, nblk, N_KV_HEADS, 2 * HEAD_DIM, MOBA_BLOCK)
    v = v.reshape(bsz, seq, N_KV_HEADS * HEAD_DIM)
    kmt = km.reshape(bsz, nblk, N_KV_HEADS, HEAD_DIM).transpose(0, 2, 3, 1)
    kmt = jnp.pad(kmt, ((0, 0), (0, 0), (0, 0), (0, LANES - nblk)))

    x2 = _ffn_call(x2, norm_g[1, 0], ffn_w_in[1, 0], ffn_w_out[1, 0])
    x3 = _attn_call(x2.reshape(bsz, seq, D_MODEL), norm_g[1, 1], w_q[0], w_o[0], kt, v, kmt, invf, sign)
    x2 = _ffn_call(x3.reshape(bsz * seq, D_MODEL), norm_g[1, 2], ffn_w_in[1, 1], ffn_w_out[1, 1],
                   final_g=final_g)
    return x2.reshape(bsz, seq, D_MODEL)
```

```python
import functools

import jax
import jax.numpy as jnp
import numpy as np
from jax import lax
from jax.experimental import pallas as pl
from jax.experimental.pallas import tpu as pltpu

F32 = jnp.float32
BF16 = jnp.bfloat16

D_MODEL = 1024
D_FF = 2816
RMS_EPS = 1e-6
S5_GROUP = 16
S5_GROUPS = D_MODEL // S5_GROUP
S5_STATE = 64
N_HEADS = 8
HEAD_DIM = 128
N_KV_HEADS = 2
KV_GROUP = N_HEADS // N_KV_HEADS
ROPE_DIM = HEAD_DIM // 4
ROPE_THETA = 500000.0
MOBA_BLOCK = 256
MOBA_TOPK = 3
NEG_INF = -1e30
LOG2E = 1.4426950408889634

LANES = 128
SUBLANES = 8
VMEM_LIMIT = 56 * 1024 * 1024

FFN_ROWS = 512
FFN_COLS = 256

ATT_GROUP = 4

S5_CHUNK = 256
S5_SUBSEQ = SUBLANES
S5_STEPS = S5_CHUNK // S5_SUBSEQ
S5_LANE_BLOCKS = D_MODEL // LANES
S5_GROUPS_PER_BLOCK = LANES // S5_GROUP
S5_HALF = S5_GROUPS_PER_BLOCK * S5_STATE
S5_NSTATE = S5_GROUPS * S5_STATE


def _rms(x, g):
    ms = jnp.mean(x * x, axis=-1, keepdims=True)
    return (x * lax.rsqrt(ms + RMS_EPS)) * g


def _dot(a, b):
    return jnp.dot(a, b, preferred_element_type=F32)


def _resident(shape):
    nd = len(shape)
    return pl.BlockSpec(shape, lambda *_: (0,) * nd, pipeline_mode=pl.Buffered(1))


def _ffn_body(*refs, final):
    if final:
        x_ref, g_ref, win_ref, wout_ref, fg_ref, o_ref = refs
    else:
        x_ref, g_ref, win_ref, wout_ref, o_ref = refs
    x = x_ref[...]
    h = _rms(x, g_ref[...]).astype(BF16)
    acc = jnp.zeros(x.shape, F32)
    for c in range(D_FF // FFN_COLS):
        lo = c * FFN_COLS
        gate = _dot(h, win_ref[:, lo:lo + FFN_COLS])
        up = _dot(h, win_ref[:, D_FF + lo:D_FF + lo + FFN_COLS])
        act = (gate * jax.nn.sigmoid(gate)) * up
        acc = acc + _dot(act.astype(BF16), wout_ref[lo:lo + FFN_COLS, :])
    y = x + 0.5 * acc
    if final:
        y = _rms(y, fg_ref[...])
    o_ref[...] = y


def _ffn_call(x2, g, w_in, w_out, final_g=None):
    t = x2.shape[0]
    final = final_g is not None
    row_spec = pl.BlockSpec((FFN_ROWS, D_MODEL), lambda i: (i, 0))
    in_specs = [row_spec, _resident((1, D_MODEL)), _resident((D_MODEL, 2 * D_FF)),
                _resident((D_FF, D_MODEL))]
    args = [x2, g.reshape(1, D_MODEL), w_in.astype(BF16), w_out.astype(BF16)]
    if final:
        in_specs.append(_resident((1, D_MODEL)))
        args.append(final_g.reshape(1, D_MODEL))
    return pl.pallas_call(
        functools.partial(_ffn_body, final=final),
        grid=(t // FFN_ROWS,),
        in_specs=in_specs,
        out_specs=row_spec,
        out_shape=jax.ShapeDtypeStruct((t, D_MODEL), F32),
        compiler_params=pltpu.CompilerParams(
            dimension_semantics=("arbitrary",), vmem_limit_bytes=VMEM_LIMIT),
        name="ffn_final" if final else "ffn",
    )(*args)


def _s5_prep_body(lr_ref, li_ref, ls_ref, bre_ref, bim_ref, cim_ref,
                  ar_ref, ai_ref, amr_ref, ami_ref, pr_ref, pi_ref, btr_ref, bti_ref, cneg_ref):
    lr = lr_ref[...]
    li = li_ref[...]
    dt = jnp.exp(ls_ref[...])
    mag = jnp.exp(lr * dt)
    abar_re = mag * jnp.cos(li * dt)
    abar_im = mag * jnp.sin(li * dt)
    ar_ref[...] = abar_re
    ai_ref[...] = abar_im
    nr, ni = abar_re - 1.0, abar_im
    den = lr * lr + li * li
    coef_re = (nr * lr + ni * li) / den
    coef_im = (ni * lr - nr * li) / den
    k = (lax.broadcasted_iota(jnp.int32, (S5_STEPS, 1), 0) + 1).astype(F32)
    pmag = jnp.exp((lr * dt) * k)
    pang = (li * dt) * k
    pr = pmag * jnp.cos(pang)
    pi = pmag * jnp.sin(pang)
    pr_ref[...] = pr
    pi_ref[...] = pi
    amr_ref[...] = pr[S5_STEPS - 1:S5_STEPS, :]
    ami_ref[...] = pi[S5_STEPS - 1:S5_STEPS, :]
    bre = bre_ref[...]
    bim = bim_ref[...]
    btr_ref[...] = coef_re * bre - coef_im * bim
    bti_ref[...] = coef_re * bim + coef_im * bre
    cneg_ref[...] = -cim_ref[...]


def _s5_prep_call(a_re, a_im, log_step, b_re, b_im, c_im):
    n = S5_NSTATE
    row = lambda v: v.reshape(1, n)
    chan_major = lambda v: v.transpose(2, 0, 1).reshape(S5_GROUP, n)
    ls = jnp.repeat(log_step, S5_STATE)
    outs = pl.pallas_call(
        _s5_prep_body,
        out_shape=[jax.ShapeDtypeStruct((1, n), F32)] * 4
        + [jax.ShapeDtypeStruct((S5_STEPS, n), F32)] * 2
        + [jax.ShapeDtypeStruct((S5_GROUP, n), F32)] * 3,
        name="s5_prep",
    )(row(a_re), row(a_im), row(ls), chan_major(b_re), chan_major(b_im),
      c_im.transpose(1, 0, 2).reshape(S5_GROUP, n))
    return outs


def _s5_body(x_ref, g_ref, perm_ref, permt_ref, bblk_ref, cblk_ref, ar_ref, ai_ref,
             amr_ref, ami_ref, pr_ref, pi_ref, d_ref, wglu_ref, o_ref,
             xs_ref, st_ref, c_ref):
    @pl.when(pl.program_id(1) == 0)
    def _():
        st_ref[...] = jnp.zeros(st_ref.shape, F32)

    x = x_ref[...]
    u = _rms(x, g_ref[...])
    u_hi = u.astype(BF16)
    u_lo = (u - u_hi.astype(F32)).astype(BF16)
    perm = perm_ref[...]
    up_hi = _dot(perm, u_hi)
    up = up_hi + _dot(perm, u_lo)
    ub = up_hi.astype(BF16)

    last = SUBLANES * (S5_STEPS - 1)
    ys = []
    for j in range(S5_LANE_BLOCKS):
        xs_ref[j] = _dot(ub[:, j * LANES:(j + 1) * LANES], bblk_ref[j])

        a_r = jnp.broadcast_to(ar_ref[j], (SUBLANES, S5_HALF))
        a_i = jnp.broadcast_to(ai_ref[j], (SUBLANES, S5_HALF))

        def scan_step(k, carry, j=j, a_r=a_r, a_i=a_i):
            h_r, h_i = carry
            rows = pl.ds(pl.multiple_of(k * SUBLANES, SUBLANES), SUBLANES)
            n_r = a_r * h_r - a_i * h_i + xs_ref[j, rows, 0:S5_HALF]
            n_i = a_r * h_i + a_i * h_r + xs_ref[j, rows, S5_HALF:2 * S5_HALF]
            xs_ref[j, rows, 0:S5_HALF] = n_r
            xs_ref[j, rows, S5_HALF:2 * S5_HALF] = n_i
            return n_r, n_i

        zero = jnp.zeros((SUBLANES, S5_HALF), F32)
        lax.fori_loop(0, S5_STEPS, scan_step, (zero, zero), unroll=4)

        am_r = amr_ref[j]
        am_i = ami_ref[j]
        c_r = st_ref[j, :, 0:S5_HALF]
        c_i = st_ref[j, :, S5_HALF:2 * S5_HALF]
        for i in range(S5_SUBSEQ):
            c_ref[i:i + 1, 0:S5_HALF] = c_r
            c_ref[i:i + 1, S5_HALF:2 * S5_HALF] = c_i
            e_r = xs_ref[j, last + i:last + i + 1, 0:S5_HALF]
            e_i = xs_ref[j, last + i:last + i + 1, S5_HALF:2 * S5_HALF]
            c_r, c_i = am_r * c_r - am_i * c_i + e_r, am_r * c_i + am_i * c_r + e_i
        st_ref[j, :, 0:S5_HALF] = c_r
        st_ref[j, :, S5_HALF:2 * S5_HALF] = c_i

        cc_r = c_ref[:, 0:S5_HALF]
        cc_i = c_ref[:, S5_HALF:2 * S5_HALF]

        def fix_step(k, carry, j=j, cc_r=cc_r, cc_i=cc_i):
            rows = pl.ds(pl.multiple_of(k * SUBLANES, SUBLANES), SUBLANES)
            p_r = pr_ref[j, pl.ds(k, 1), :]
            p_i = pi_ref[j, pl.ds(k, 1), :]
            xs_ref[j, rows, 0:S5_HALF] = xs_ref[j, rows, 0:S5_HALF] + (p_r * cc_r - p_i * cc_i)
            xs_ref[j, rows, S5_HALF:2 * S5_HALF] = (
                xs_ref[j, rows, S5_HALF:2 * S5_HALF] + (p_r * cc_i + p_i * cc_r))
            return carry

        lax.fori_loop(0, S5_STEPS, fix_step, 0, unroll=4)

        ys.append(_dot(xs_ref[j].astype(BF16), cblk_ref[j]))

    y = jnp.concatenate(ys, axis=1) + d_ref[...] * up
    y = 0.5 * y * (1.0 + jnp.tanh(np.sqrt(2.0 / np.pi).astype(np.float32)
                                  * (y + 0.044715 * (y * y * y))))
    yn = _dot(permt_ref[...], y.astype(BF16)).astype(BF16)
    z = _dot(yn, wglu_ref[...])
    o_ref[...] = x + z[:, 0:D_MODEL] * jax.nn.sigmoid(z[:, D_MODEL:2 * D_MODEL])


def _s5_call(x2, bsz, seq, g, prep, b_unused, c_re, d_skip, w_glu):
    del b_unused
    ar, ai, amr, ami, pr, pi, btr, bti, cneg = prep
    nb, gb, ns = S5_LANE_BLOCKS, S5_GROUPS_PER_BLOCK, S5_STATE
    eye = jnp.eye(gb, dtype=F32)

    def per_block(v, rows):
        return v.reshape(rows, nb, S5_HALF).transpose(1, 0, 2)

    bt = jnp.stack([btr, bti], axis=1).reshape(S5_GROUP, 2, nb, gb, ns)
    bblk = jnp.einsum('crjgp,gh->jgcrhp', bt, eye).reshape(nb, LANES, 2 * S5_HALF).astype(BF16)
    cmat = jnp.stack([c_re.transpose(1, 0, 2).reshape(S5_GROUP, S5_NSTATE), cneg], axis=1)
    cmat = cmat.reshape(S5_GROUP, 2, nb, gb, ns)
    cblk = jnp.einsum('crjgp,gh->jrgphc', cmat, eye).reshape(nb, 2 * S5_HALF, LANES).astype(BF16)

    r = np.arange(S5_CHUNK)
    perm_np = np.zeros((S5_CHUNK, S5_CHUNK), np.float32)
    perm_np[r, (r % SUBLANES) * S5_STEPS + r // SUBLANES] = 1.0
    perm = jnp.asarray(perm_np, BF16)
    permt = jnp.asarray(perm_np.T, BF16)

    nchunk = seq // S5_CHUNK
    row_spec = pl.BlockSpec((S5_CHUNK, D_MODEL), lambda b, c: (b * nchunk + c, 0))
    in_specs = [
        row_spec, _resident((1, D_MODEL)),
        _resident((S5_CHUNK, S5_CHUNK)), _resident((S5_CHUNK, S5_CHUNK)),
        _resident((nb, LANES, 2 * S5_HALF)), _resident((nb, 2 * S5_HALF, LANES)),
        _resident((nb, 1, S5_HALF)), _resident((nb, 1, S5_HALF)),
        _resident((nb, 1, S5_HALF)), _resident((nb, 1, S5_HALF)),
        _resident((nb, S5_STEPS, S5_HALF)), _resident((nb, S5_STEPS, S5_HALF)),
        _resident((1, D_MODEL)), _resident((D_MODEL, 2 * D_MODEL)),
    ]
    return pl.pallas_call(
        _s5_body,
        grid=(bsz, nchunk),
        in_specs=in_specs,
        out_specs=row_spec,
        out_shape=jax.ShapeDtypeStruct(x2.shape, F32),
        scratch_shapes=[
            pltpu.VMEM((nb, S5_CHUNK, 2 * S5_HALF), F32),
            pltpu.VMEM((nb, 1, 2 * S5_HALF), F32),
            pltpu.VMEM((S5_SUBSEQ, 2 * S5_HALF), F32),
        ],
        compiler_params=pltpu.CompilerParams(
            dimension_semantics=("arbitrary", "arbitrary"), vmem_limit_bytes=VMEM_LIMIT),
        name="s5",
    )(x2, g.reshape(1, D_MODEL), perm, permt, bblk, cblk,
      per_block(ar, 1), per_block(ai, 1), per_block(amr, 1), per_block(ami, 1),
      per_block(pr, S5_STEPS), per_block(pi, S5_STEPS),
      d_skip.reshape(1, D_MODEL), w_glu.astype(BF16))


def _rope_tables(pos0, invf, sign):
    pos = (pos0 + lax.broadcasted_iota(jnp.int32, (MOBA_BLOCK, 1), 0)).astype(F32)
    ang = pos * invf
    return jnp.cos(ang), jnp.sin(ang) * sign


def _rope_head(xh, cos_t, sin_t, low_half):
    half = ROPE_DIM // 2
    swapped = jnp.where(low_half, pltpu.roll(xh, LANES - half, axis=1), pltpu.roll(xh, half, axis=1))
    return xh * cos_t + swapped * sin_t


def _rope_consts():
    half = ROPE_DIM // 2
    inv_freq = ROPE_THETA ** (-jnp.arange(0, ROPE_DIM, 2, dtype=F32) / ROPE_DIM)
    pad = jnp.zeros((HEAD_DIM - ROPE_DIM,), F32)
    invf = jnp.concatenate([inv_freq, inv_freq, pad]).reshape(1, HEAD_DIM)
    sign = jnp.concatenate([-jnp.ones((half,), F32), jnp.ones((half,), F32), pad]).reshape(1, HEAD_DIM)
    return invf, sign


def _kv_body(x_ref, g_ref, wkv_ref, invf_ref, sign_ref, ka_ref, vt_ref, km_ref, *, nblk):
    blk = pl.program_id(0) % nblk
    h = _rms(x_ref[...], g_ref[...]).astype(BF16)
    kv = _dot(h, wkv_ref[...])
    kdim = N_KV_HEADS * HEAD_DIM
    cos_t, sin_t = _rope_tables(blk * MOBA_BLOCK, invf_ref[...], sign_ref[...])
    lane = lax.broadcasted_iota(jnp.int32, (MOBA_BLOCK, HEAD_DIM), 1)
    low_half = lane < ROPE_DIM // 2
    onehot = jnp.where(lane == blk, 1.0, 0.0).astype(BF16)
    means = []
    for hh in range(N_KV_HEADS):
        kh = _rope_head(kv[:, hh * HEAD_DIM:(hh + 1) * HEAD_DIM], cos_t, sin_t, low_half)
        means.append(jnp.mean(kh, axis=0, keepdims=True))
        ka_ref[hh] = jnp.concatenate([kh.astype(BF16), onehot], axis=1)
        vt_ref[0, hh] = kv[:, kdim + hh * HEAD_DIM:kdim + (hh + 1) * HEAD_DIM].T.astype(BF16)
    km_ref[0] = jnp.concatenate(means, axis=1)


def _kv_call(x2, seq, g, w_k, w_v, invf, sign):
    t = x2.shape[0]
    ntile = t // MOBA_BLOCK
    kdim = N_KV_HEADS * HEAD_DIM
    wkv = jnp.concatenate([w_k, w_v], axis=1).astype(BF16)
    return pl.pallas_call(
        functools.partial(_kv_body, nblk=seq // MOBA_BLOCK),
        grid=(ntile,),
        in_specs=[pl.BlockSpec((MOBA_BLOCK, D_MODEL), lambda i: (i, 0)),
                  _resident((1, D_MODEL)), _resident((D_MODEL, 2 * kdim)),
                  _resident((1, HEAD_DIM)), _resident((1, HEAD_DIM))],
        out_specs=[pl.BlockSpec((N_KV_HEADS, MOBA_BLOCK, 2 * HEAD_DIM), lambda i: (0, i, 0)),
                   pl.BlockSpec((1, N_KV_HEADS, HEAD_DIM, MOBA_BLOCK), lambda i: (i, 0, 0, 0)),
                   pl.BlockSpec((1, 1, kdim), lambda i: (i, 0, 0))],
        out_shape=[jax.ShapeDtypeStruct((N_KV_HEADS, t, 2 * HEAD_DIM), BF16),
                   jax.ShapeDtypeStruct((ntile, N_KV_HEADS, HEAD_DIM, MOBA_BLOCK), BF16),
                   jax.ShapeDtypeStruct((ntile, 1, kdim), F32)],
        compiler_params=pltpu.CompilerParams(
            dimension_semantics=("arbitrary",), vmem_limit_bytes=VMEM_LIMIT),
        name="kv_proj",
    )(x2, g.reshape(1, D_MODEL), wkv, invf, sign)


def _split_bf16(v):
    hi = v.astype(BF16)
    return hi, (v - hi.astype(F32)).astype(BF16)


def _attn_body(x_ref, g_ref, wq_ref, wo_ref, ka_ref, vt_ref, km_ref, invf_ref, o_ref,
               qa_ref, acc_ref, *, nblk):
    own = pl.program_id(1)
    items = KV_GROUP * MOBA_BLOCK
    half = ROPE_DIM // 2
    x = x_ref[0]
    h = _rms(x, g_ref[...]).astype(BF16)
    q = _dot(h, wq_ref[...])
    pos = (own * MOBA_BLOCK + lax.broadcasted_iota(jnp.int32, (1, MOBA_BLOCK), 1)).astype(F32)
    ang = invf_ref[...] * pos
    cos_t, sin_t = jnp.cos(ang), jnp.sin(ang)

    def head_t(i):
        qt = q[:, i * HEAD_DIM:(i + 1) * HEAD_DIM].T
        x1, x2 = qt[0:half], qt[half:2 * half]
        rot = jnp.concatenate([x1 * cos_t - x2 * sin_t, x2 * cos_t + x1 * sin_t, qt[2 * half:]], axis=0)
        return rot * (HEAD_DIM ** -0.5)

    blk = lax.broadcasted_iota(jnp.int32, (nblk, items), 0)
    blk_f = blk.astype(F32)
    past = blk < own
    causal = (lax.broadcasted_iota(jnp.int32, (MOBA_BLOCK, items), 0)
              <= lax.broadcasted_iota(jnp.int32, (MOBA_BLOCK, items), 1) % MOBA_BLOCK)
    feat_pad = jnp.zeros((HEAD_DIM - nblk, items), BF16)

    outs = []
    for kh in range(N_KV_HEADS):
        qt = jnp.concatenate([head_t(kh * KV_GROUP + i) for i in range(KV_GROUP)], axis=1)
        q_hi, q_lo = _split_bf16(qt)
        k_hi, k_lo = _split_bf16(km_ref[0, kh])
        gate = _dot(k_hi, q_hi) + (_dot(k_hi, q_lo) + _dot(k_lo, q_hi))
        cur = jnp.where(past, gate, -jnp.inf)
        bias = jnp.full((nblk, items), NEG_INF, F32)
        for _ in range(MOBA_TOPK):
            best = jnp.max(cur, axis=0, keepdims=True)
            cand = jnp.where((cur == best) & (best > -jnp.inf), blk_f, float(nblk))
            pick = blk_f == jnp.min(cand, axis=0, keepdims=True)
            bias = jnp.where(pick, 0.0, bias)
            cur = jnp.where(pick, -jnp.inf, cur)
        q_feat = (qt * LOG2E).astype(BF16)
        qa_ref[...] = jnp.concatenate([q_feat, bias.astype(BF16), feat_pad], axis=0)

        own_keys = pl.ds(pl.multiple_of(own * MOBA_BLOCK, MOBA_BLOCK), MOBA_BLOCK)
        s = _dot(ka_ref[kh, 0, own_keys, 0:HEAD_DIM], q_feat)
        s = jnp.where(causal, s, NEG_INF)
        m = jnp.max(s, axis=0, keepdims=True)
        p = jnp.exp2(s - m)
        l = jnp.sum(p, axis=0, keepdims=True)
        acc_ref[...] = _dot(vt_ref[0, own, kh], p.astype(BF16))

        def past_group(gi, carry, kh=kh):
            m_prev, l_prev = carry
            n0 = gi * ATT_GROUP
            keys = pl.ds(pl.multiple_of(n0 * MOBA_BLOCK, ATT_GROUP * MOBA_BLOCK), ATT_GROUP * MOBA_BLOCK)
            s = _dot(ka_ref[kh, 0, keys, :], qa_ref[...])
            m_new = jnp.maximum(m_prev, jnp.max(s, axis=0, keepdims=True))
            alpha = jnp.exp2(m_prev - m_new)
            p = jnp.exp2(s - m_new)
            pb = p.astype(BF16)
            pv = _dot(vt_ref[0, n0, kh], pb[0:MOBA_BLOCK])
            for j in range(1, ATT_GROUP):
                pv = pv + _dot(vt_ref[0, n0 + j, kh], pb[j * MOBA_BLOCK:(j + 1) * MOBA_BLOCK])
            acc_ref[...] = alpha * acc_ref[...] + pv
            return m_new, alpha * l_prev + jnp.sum(p, axis=0, keepdims=True)

        m, l = lax.fori_loop(0, (own + ATT_GROUP - 1) // ATT_GROUP, past_group, (m, l))
        ot = acc_ref[...] / l
        outs.extend(ot[:, i * MOBA_BLOCK:(i + 1) * MOBA_BLOCK].T for i in range(KV_GROUP))

    attn = jnp.concatenate(outs, axis=1).astype(BF16)
    o_ref[0] = x + _dot(attn, wo_ref[...])


def _attn_call(x3, g, w_q, w_o, ka, vt, km, invf_col):
    bsz, seq, _ = x3.shape
    nblk = seq // MOBA_BLOCK
    items = KV_GROUP * MOBA_BLOCK
    x_spec = pl.BlockSpec((1, MOBA_BLOCK, D_MODEL), lambda b, i: (b, i, 0))
    return pl.pallas_call(
        functools.partial(_attn_body, nblk=nblk),
        grid=(bsz, nblk),
        in_specs=[x_spec, _resident((1, D_MODEL)),
                  _resident((D_MODEL, D_MODEL)), _resident((D_MODEL, D_MODEL)),
                  pl.BlockSpec((N_KV_HEADS, 1, seq, 2 * HEAD_DIM), lambda b, i: (0, b, 0, 0)),
                  pl.BlockSpec((1, nblk, N_KV_HEADS, HEAD_DIM, MOBA_BLOCK),
                               lambda b, i: (b, 0, 0, 0, 0)),
                  pl.BlockSpec((1, N_KV_HEADS, nblk, HEAD_DIM), lambda b, i: (b, 0, 0, 0)),
                  _resident((ROPE_DIM // 2, 1))],
        out_specs=x_spec,
        out_shape=jax.ShapeDtypeStruct(x3.shape, F32),
        scratch_shapes=[pltpu.VMEM((2 * HEAD_DIM, items), BF16),
                        pltpu.VMEM((HEAD_DIM, items), F32)],
        compiler_params=pltpu.CompilerParams(
            dimension_semantics=("arbitrary", "arbitrary"), vmem_limit_bytes=VMEM_LIMIT),
        name="moba_attn",
    )(x3, g.reshape(1, D_MODEL), w_q.astype(BF16), w_o.astype(BF16), ka, vt, km, invf_col)


def kernel(x, norm_g, ffn_w_in, ffn_w_out, s5_a_re, s5_a_im, s5_log_step, s5_b_re, s5_b_im,
           s5_c_re, s5_c_im, s5_d, s5_w_glu, kv_norm_g, w_k, w_v, w_q, w_o, final_g):
    bsz, seq, _ = x.shape
    assert seq % S5_CHUNK == 0 and seq % MOBA_BLOCK == 0
    assert (seq // MOBA_BLOCK) % (2 * SUBLANES) == 0 and seq // MOBA_BLOCK <= HEAD_DIM
    assert (seq // MOBA_BLOCK) % ATT_GROUP == 0
    assert (bsz * seq) % FFN_ROWS == 0
    nblk = seq // MOBA_BLOCK
    x2 = x.reshape(bsz * seq, D_MODEL)
    invf, sign = _rope_consts()

    x2 = _ffn_call(x2, norm_g[0, 0], ffn_w_in[0, 0], ffn_w_out[0, 0])
    prep = _s5_prep_call(s5_a_re[0], s5_a_im[0], s5_log_step[0], s5_b_re[0], s5_b_im[0], s5_c_im[0])
    x2 = _s5_call(x2, bsz, seq, norm_g[0, 1], prep, None, s5_c_re[0], s5_d[0], s5_w_glu[0])
    x2 = _ffn_call(x2, norm_g[0, 2], ffn_w_in[0, 1], ffn_w_out[0, 1])

    ka, vt, km = _kv_call(x2, seq, kv_norm_g, w_k, w_v, invf, sign)
    ka = ka.reshape(N_KV_HEADS, bsz, seq, 2 * HEAD_DIM)
    vt = vt.reshape(bsz, nblk, N_KV_HEADS, HEAD_DIM, MOBA_BLOCK)
    km = km.reshape(bsz, nblk, N_KV_HEADS, HEAD_DIM).transpose(0, 2, 1, 3)

    x2 = _ffn_call(x2, norm_g[1, 0], ffn_w_in[1, 0], ffn_w_out[1, 0])
    x3 = _attn_call(x2.reshape(bsz, seq, D_MODEL), norm_g[1, 1], w_q[0], w_o[0], ka, vt, km,
                    invf[0, 0:ROPE_DIM // 2].reshape(ROPE_DIM // 2, 1))
    x2 = _ffn_call(x3.reshape(bsz * seq, D_MODEL), norm_g[1, 2], ffn_w_in[1, 1], ffn_w_out[1, 1],
                   final_g=final_g)
    return x2.reshape(bsz, seq, D_MODEL)
```

```python
import functools

import jax
import jax.numpy as jnp
import numpy as np
from jax import lax
from jax.experimental import pallas as pl
from jax.experimental.pallas import tpu as pltpu

F32 = jnp.float32
BF16 = jnp.bfloat16

D_MODEL = 1024
D_FF = 2816
RMS_EPS = 1e-6
S5_GROUP = 16
S5_GROUPS = D_MODEL // S5_GROUP
S5_STATE = 64
N_HEADS = 8
HEAD_DIM = 128
N_KV_HEADS = 2
KV_GROUP = N_HEADS // N_KV_HEADS
ROPE_DIM = HEAD_DIM // 4
ROPE_THETA = 500000.0
MOBA_BLOCK = 256
MOBA_TOPK = 3
NEG_INF = -1e30
LOG2E = 1.4426950408889634

LANES = 128
SUBLANES = 8
VMEM_LIMIT = 56 * 1024 * 1024

FFN_ROWS = 512
FFN_COLS = 256

ATT_GROUP = 4
V_ROWS = HEAD_DIM + 2 * SUBLANES

S5_CHUNK = 256
S5_SUBSEQ = SUBLANES
S5_STEPS = S5_CHUNK // S5_SUBSEQ
S5_LANE_BLOCKS = D_MODEL // LANES
S5_GROUPS_PER_BLOCK = LANES // S5_GROUP
S5_HALF = S5_GROUPS_PER_BLOCK * S5_STATE
S5_NSTATE = S5_GROUPS * S5_STATE


def _rms(x, g):
    ms = jnp.mean(x * x, axis=-1, keepdims=True)
    return (x * lax.rsqrt(ms + RMS_EPS)) * g


def _dot(a, b):
    return jnp.dot(a, b, preferred_element_type=F32)


def _resident(shape):
    nd = len(shape)
    return pl.BlockSpec(shape, lambda *_: (0,) * nd, pipeline_mode=pl.Buffered(1))


def _ffn_body(*refs, final):
    if final:
        x_ref, g_ref, win_ref, wout_ref, fg_ref, o_ref = refs
    else:
        x_ref, g_ref, win_ref, wout_ref, o_ref = refs
    x = x_ref[...]
    h = _rms(x, g_ref[...]).astype(BF16)
    acc = jnp.zeros(x.shape, F32)
    for c in range(D_FF // FFN_COLS):
        lo = c * FFN_COLS
        gate = _dot(h, win_ref[:, lo:lo + FFN_COLS])
        up = _dot(h, win_ref[:, D_FF + lo:D_FF + lo + FFN_COLS])
        act = (gate * jax.nn.sigmoid(gate)) * up
        acc = acc + _dot(act.astype(BF16), wout_ref[lo:lo + FFN_COLS, :])
    y = x + 0.5 * acc
    if final:
        y = _rms(y, fg_ref[...])
    o_ref[...] = y


def _ffn_call(x2, g, w_in, w_out, final_g=None):
    t = x2.shape[0]
    final = final_g is not None
    row_spec = pl.BlockSpec((FFN_ROWS, D_MODEL), lambda i: (i, 0))
    in_specs = [row_spec, _resident((1, D_MODEL)), _resident((D_MODEL, 2 * D_FF)),
                _resident((D_FF, D_MODEL))]
    args = [x2, g.reshape(1, D_MODEL), w_in.astype(BF16), w_out.astype(BF16)]
    if final:
        in_specs.append(_resident((1, D_MODEL)))
        args.append(final_g.reshape(1, D_MODEL))
    return pl.pallas_call(
        functools.partial(_ffn_body, final=final),
        grid=(t // FFN_ROWS,),
        in_specs=in_specs,
        out_specs=row_spec,
        out_shape=jax.ShapeDtypeStruct((t, D_MODEL), F32),
        compiler_params=pltpu.CompilerParams(
            dimension_semantics=("arbitrary",), vmem_limit_bytes=VMEM_LIMIT),
        name="ffn_final" if final else "ffn",
    )(*args)


def _s5_prep_body(lr_ref, li_ref, ls_ref, bre_ref, bim_ref, cim_ref,
                  ar_ref, ai_ref, amr_ref, ami_ref, pr_ref, pi_ref, btr_ref, bti_ref, cneg_ref):
    lr = lr_ref[...]
    li = li_ref[...]
    dt = jnp.exp(ls_ref[...])
    mag = jnp.exp(lr * dt)
    abar_re = mag * jnp.cos(li * dt)
    abar_im = mag * jnp.sin(li * dt)
    ar_ref[...] = abar_re
    ai_ref[...] = abar_im
    nr, ni = abar_re - 1.0, abar_im
    den = lr * lr + li * li
    coef_re = (nr * lr + ni * li) / den
    coef_im = (ni * lr - nr * li) / den
    k = (lax.broadcasted_iota(jnp.int32, (S5_STEPS, 1), 0) + 1).astype(F32)
    pmag = jnp.exp((lr * dt) * k)
    pang = (li * dt) * k
    pr = pmag * jnp.cos(pang)
    pi = pmag * jnp.sin(pang)
    pr_ref[...] = pr
    pi_ref[...] = pi
    amr_ref[...] = pr[S5_STEPS - 1:S5_STEPS, :]
    ami_ref[...] = pi[S5_STEPS - 1:S5_STEPS, :]
    bre = bre_ref[...]
    bim = bim_ref[...]
    btr_ref[...] = coef_re * bre - coef_im * bim
    bti_ref[...] = coef_re * bim + coef_im * bre
    cneg_ref[...] = -cim_ref[...]


def _s5_prep_call(a_re, a_im, log_step, b_re, b_im, c_im):
    n = S5_NSTATE
    row = lambda v: v.reshape(1, n)
    chan_major = lambda v: v.transpose(2, 0, 1).reshape(S5_GROUP, n)
    ls = jnp.repeat(log_step, S5_STATE)
    outs = pl.pallas_call(
        _s5_prep_body,
        out_shape=[jax.ShapeDtypeStruct((1, n), F32)] * 4
        + [jax.ShapeDtypeStruct((S5_STEPS, n), F32)] * 2
        + [jax.ShapeDtypeStruct((S5_GROUP, n), F32)] * 3,
        name="s5_prep",
    )(row(a_re), row(a_im), row(ls), chan_major(b_re), chan_major(b_im),
      c_im.transpose(1, 0, 2).reshape(S5_GROUP, n))
    return outs


def _s5_body(x_ref, g_ref, perm_ref, permt_ref, bblk_ref, cblk_ref, ar_ref, ai_ref,
             amr_ref, ami_ref, pr_ref, pi_ref, d_ref, wglu_ref, o_ref,
             xs_ref, st_ref, c_ref, hb_ref):
    @pl.when(pl.program_id(1) == 0)
    def _():
        st_ref[...] = jnp.zeros(st_ref.shape, F32)

    x = x_ref[...]
    u = _rms(x, g_ref[...])
    u_hi = u.astype(BF16)
    u_lo = (u - u_hi.astype(F32)).astype(BF16)
    perm = perm_ref[...]
    up_hi = _dot(perm, u_hi)
    up = up_hi + _dot(perm, u_lo)
    ub = up_hi.astype(BF16)

    last = SUBLANES * (S5_STEPS - 1)
    ys = []
    for j in range(S5_LANE_BLOCKS):
        xs_ref[j] = _dot(ub[:, j * LANES:(j + 1) * LANES], bblk_ref[j])

        a_r = jnp.broadcast_to(ar_ref[j], (SUBLANES, S5_HALF))
        a_i = jnp.broadcast_to(ai_ref[j], (SUBLANES, S5_HALF))

        h_r = xs_ref[j, 0:SUBLANES, 0:S5_HALF]
        h_i = xs_ref[j, 0:SUBLANES, S5_HALF:2 * S5_HALF]
        for k in range(1, S5_STEPS):
            rows = slice(k * SUBLANES, (k + 1) * SUBLANES)
            h_r, h_i = (a_r * h_r - a_i * h_i + xs_ref[j, rows, 0:S5_HALF],
                        a_r * h_i + a_i * h_r + xs_ref[j, rows, S5_HALF:2 * S5_HALF])
            xs_ref[j, rows, 0:S5_HALF] = h_r
            xs_ref[j, rows, S5_HALF:2 * S5_HALF] = h_i

        am_r = amr_ref[j]
        am_i = ami_ref[j]
        c_r = st_ref[j, :, 0:S5_HALF]
        c_i = st_ref[j, :, S5_HALF:2 * S5_HALF]
        for i in range(S5_SUBSEQ):
            c_ref[j, i:i + 1, 0:S5_HALF] = c_r
            c_ref[j, i:i + 1, S5_HALF:2 * S5_HALF] = c_i
            e_r = xs_ref[j, last + i:last + i + 1, 0:S5_HALF]
            e_i = xs_ref[j, last + i:last + i + 1, S5_HALF:2 * S5_HALF]
            c_r, c_i = am_r * c_r - am_i * c_i + e_r, am_r * c_i + am_i * c_r + e_i
        st_ref[j, :, 0:S5_HALF] = c_r
        st_ref[j, :, S5_HALF:2 * S5_HALF] = c_i

        cc_r = jnp.concatenate([c_ref[j, :, 0:S5_HALF]] * 2, axis=0)
        cc_i = jnp.concatenate([c_ref[j, :, S5_HALF:2 * S5_HALF]] * 2, axis=0)
        for k in range(S5_STEPS // 2):
            rows = slice(2 * k * SUBLANES, 2 * (k + 1) * SUBLANES)
            p_r = pr_ref[j, rows, :]
            p_i = pi_ref[j, rows, :]
            t_r = xs_ref[j, rows, 0:S5_HALF] + (p_r * cc_r - p_i * cc_i)
            t_i = xs_ref[j, rows, S5_HALF:2 * S5_HALF] + (p_r * cc_i + p_i * cc_r)
            hb_ref[j, rows, :] = jnp.concatenate([t_r, t_i], axis=1).astype(BF16)

        ys.append(_dot(hb_ref[j], cblk_ref[j]))

    y = jnp.concatenate(ys, axis=1) + d_ref[...] * up
    y = 0.5 * y * (1.0 + jnp.tanh(np.sqrt(2.0 / np.pi).astype(np.float32)
                                  * (y + 0.044715 * (y * y * y))))
    yn = _dot(permt_ref[...], y.astype(BF16)).astype(BF16)
    z = _dot(yn, wglu_ref[...])
    o_ref[...] = x + z[:, 0:D_MODEL] * jax.nn.sigmoid(z[:, D_MODEL:2 * D_MODEL])


def _s5_call(x2, bsz, seq, g, prep, b_unused, c_re, d_skip, w_glu):
    del b_unused
    ar, ai, amr, ami, pr, pi, btr, bti, cneg = prep
    nb, gb, ns = S5_LANE_BLOCKS, S5_GROUPS_PER_BLOCK, S5_STATE
    eye = jnp.eye(gb, dtype=F32)

    def per_block(v, rows):
        return v.reshape(rows, nb, S5_HALF).transpose(1, 0, 2)

    bt = jnp.stack([btr, bti], axis=1).reshape(S5_GROUP, 2, nb, gb, ns)
    bblk = jnp.einsum('crjgp,gh->jgcrhp', bt, eye).reshape(nb, LANES, 2 * S5_HALF).astype(BF16)
    cmat = jnp.stack([c_re.transpose(1, 0, 2).reshape(S5_GROUP, S5_NSTATE), cneg], axis=1)
    cmat = cmat.reshape(S5_GROUP, 2, nb, gb, ns)
    cblk = jnp.einsum('crjgp,gh->jrgphc', cmat, eye).reshape(nb, 2 * S5_HALF, LANES).astype(BF16)

    r = np.arange(S5_CHUNK)
    perm_np = np.zeros((S5_CHUNK, S5_CHUNK), np.float32)
    perm_np[r, (r % SUBLANES) * S5_STEPS + r // SUBLANES] = 1.0
    perm = jnp.asarray(perm_np, BF16)
    permt = jnp.asarray(perm_np.T, BF16)

    nchunk = seq // S5_CHUNK
    row_spec = pl.BlockSpec((S5_CHUNK, D_MODEL), lambda b, c: (b * nchunk + c, 0))
    in_specs = [
        row_spec, _resident((1, D_MODEL)),
        _resident((S5_CHUNK, S5_CHUNK)), _resident((S5_CHUNK, S5_CHUNK)),
        _resident((nb, LANES, 2 * S5_HALF)), _resident((nb, 2 * S5_HALF, LANES)),
        _resident((nb, 1, S5_HALF)), _resident((nb, 1, S5_HALF)),
        _resident((nb, 1, S5_HALF)), _resident((nb, 1, S5_HALF)),
        _resident((nb, S5_CHUNK, S5_HALF)), _resident((nb, S5_CHUNK, S5_HALF)),
        _resident((1, D_MODEL)), _resident((D_MODEL, 2 * D_MODEL)),
    ]
    return pl.pallas_call(
        _s5_body,
        grid=(bsz, nchunk),
        in_specs=in_specs,
        out_specs=row_spec,
        out_shape=jax.ShapeDtypeStruct(x2.shape, F32),
        scratch_shapes=[
            pltpu.VMEM((nb, S5_CHUNK, 2 * S5_HALF), F32),
            pltpu.VMEM((nb, 1, 2 * S5_HALF), F32),
            pltpu.VMEM((nb, S5_SUBSEQ, 2 * S5_HALF), F32),
            pltpu.VMEM((nb, S5_CHUNK, 2 * S5_HALF), BF16),
        ],
        compiler_params=pltpu.CompilerParams(
            dimension_semantics=("arbitrary", "arbitrary"), vmem_limit_bytes=VMEM_LIMIT),
        name="s5",
    )(x2, g.reshape(1, D_MODEL), perm, permt, bblk, cblk,
      per_block(ar, 1), per_block(ai, 1), per_block(amr, 1), per_block(ami, 1),
      per_block(jnp.repeat(pr, SUBLANES, axis=0), S5_CHUNK),
      per_block(jnp.repeat(pi, SUBLANES, axis=0), S5_CHUNK),
      d_skip.reshape(1, D_MODEL), w_glu.astype(BF16))


def _rope_tables(pos0, invf, sign):
    pos = (pos0 + lax.broadcasted_iota(jnp.int32, (MOBA_BLOCK, 1), 0)).astype(F32)
    ang = pos * invf
    return jnp.cos(ang), jnp.sin(ang) * sign


def _rope_head(xh, cos_t, sin_t, low_half):
    half = ROPE_DIM // 2
    swapped = jnp.where(low_half, pltpu.roll(xh, LANES - half, axis=1), pltpu.roll(xh, half, axis=1))
    return xh * cos_t + swapped * sin_t


def _rope_consts():
    half = ROPE_DIM // 2
    inv_freq = ROPE_THETA ** (-jnp.arange(0, ROPE_DIM, 2, dtype=F32) / ROPE_DIM)
    pad = jnp.zeros((HEAD_DIM - ROPE_DIM,), F32)
    invf = jnp.concatenate([inv_freq, inv_freq, pad]).reshape(1, HEAD_DIM)
    sign = jnp.concatenate([-jnp.ones((half,), F32), jnp.ones((half,), F32), pad]).reshape(1, HEAD_DIM)
    return invf, sign


def _kv_body(x_ref, g_ref, wkv_ref, invf_ref, sign_ref, ka_ref, vt_ref, km_ref, *, nblk):
    blk = pl.program_id(0) % nblk
    h = _rms(x_ref[...], g_ref[...]).astype(BF16)
    kv = _dot(h, wkv_ref[...])
    kdim = N_KV_HEADS * HEAD_DIM
    cos_t, sin_t = _rope_tables(blk * MOBA_BLOCK, invf_ref[...], sign_ref[...])
    lane = lax.broadcasted_iota(jnp.int32, (MOBA_BLOCK, HEAD_DIM), 1)
    low_half = lane < ROPE_DIM // 2
    onehot = jnp.where(lane == blk, 1.0, 0.0).astype(BF16)
    pad_row = lax.broadcasted_iota(jnp.int32, (V_ROWS - HEAD_DIM, MOBA_BLOCK), 0)
    ones_rows = jnp.where(pad_row == 0, 1.0, 0.0).astype(BF16)
    means = []
    for hh in range(N_KV_HEADS):
        kh = _rope_head(kv[:, hh * HEAD_DIM:(hh + 1) * HEAD_DIM], cos_t, sin_t, low_half)
        means.append(jnp.mean(kh, axis=0, keepdims=True))
        ka_ref[hh] = jnp.concatenate([kh.astype(BF16), onehot], axis=1)
        vt = kv[:, kdim + hh * HEAD_DIM:kdim + (hh + 1) * HEAD_DIM].T.astype(BF16)
        vt_ref[0, hh] = jnp.concatenate([vt, ones_rows], axis=0)
    km_ref[0] = jnp.concatenate(means, axis=1)


def _kv_call(x2, seq, g, w_k, w_v, invf, sign):
    t = x2.shape[0]
    ntile = t // MOBA_BLOCK
    kdim = N_KV_HEADS * HEAD_DIM
    wkv = jnp.concatenate([w_k, w_v], axis=1).astype(BF16)
    return pl.pallas_call(
        functools.partial(_kv_body, nblk=seq // MOBA_BLOCK),
        grid=(ntile,),
        in_specs=[pl.BlockSpec((MOBA_BLOCK, D_MODEL), lambda i: (i, 0)),
                  _resident((1, D_MODEL)), _resident((D_MODEL, 2 * kdim)),
                  _resident((1, HEAD_DIM)), _resident((1, HEAD_DIM))],
        out_specs=[pl.BlockSpec((N_KV_HEADS, MOBA_BLOCK, 2 * HEAD_DIM), lambda i: (0, i, 0)),
                   pl.BlockSpec((1, N_KV_HEADS, V_ROWS, MOBA_BLOCK), lambda i: (i, 0, 0, 0)),
                   pl.BlockSpec((1, 1, kdim), lambda i: (i, 0, 0))],
        out_shape=[jax.ShapeDtypeStruct((N_KV_HEADS, t, 2 * HEAD_DIM), BF16),
                   jax.ShapeDtypeStruct((ntile, N_KV_HEADS, V_ROWS, MOBA_BLOCK), BF16),
                   jax.ShapeDtypeStruct((ntile, 1, kdim), F32)],
        compiler_params=pltpu.CompilerParams(
            dimension_semantics=("arbitrary",), vmem_limit_bytes=VMEM_LIMIT),
        name="kv_proj",
    )(x2, g.reshape(1, D_MODEL), wkv, invf, sign)


def _split_bf16(v):
    hi = v.astype(BF16)
    return hi, (v - hi.astype(F32)).astype(BF16)


def _attn_body(x_ref, g_ref, wq_ref, wo_ref, ka_ref, vt_ref, km_ref, invf_ref, o_ref,
               qa_ref, acc_ref, s0_ref, s1_ref, *, nblk):
    own = pl.program_id(1)
    items = KV_GROUP * MOBA_BLOCK
    group_keys = ATT_GROUP * MOBA_BLOCK
    half = ROPE_DIM // 2
    x = x_ref[0]
    h = _rms(x, g_ref[...]).astype(BF16)
    q = _dot(h, wq_ref[...])
    pos = (own * MOBA_BLOCK + lax.broadcasted_iota(jnp.int32, (1, MOBA_BLOCK), 1)).astype(F32)
    ang = invf_ref[...] * pos
    cos_t, sin_t = jnp.cos(ang), jnp.sin(ang)

    def head_t(i):
        qt = q[:, i * HEAD_DIM:(i + 1) * HEAD_DIM].T
        x1, x2 = qt[0:half], qt[half:2 * half]
        rot = jnp.concatenate([x1 * cos_t - x2 * sin_t, x2 * cos_t + x1 * sin_t, qt[2 * half:]], axis=0)
        return rot * (HEAD_DIM ** -0.5)

    blk = lax.broadcasted_iota(jnp.int32, (nblk, items), 0)
    blk_f = blk.astype(F32)
    past = blk < own
    causal = (lax.broadcasted_iota(jnp.int32, (MOBA_BLOCK, items), 0)
              <= lax.broadcasted_iota(jnp.int32, (MOBA_BLOCK, items), 1) % MOBA_BLOCK)
    feat_pad = jnp.zeros((HEAD_DIM - nblk, items), BF16)

    outs = []
    for kh in range(N_KV_HEADS):
        qt = jnp.concatenate([head_t(kh * KV_GROUP + i) for i in range(KV_GROUP)], axis=1)
        q_hi, q_lo = _split_bf16(qt)
        k_hi, k_lo = _split_bf16(km_ref[0, kh])
        gate = _dot(k_hi, q_hi) + (_dot(k_hi, q_lo) + _dot(k_lo, q_hi))
        cur = jnp.where(past, gate, -jnp.inf)
        bias = jnp.full((nblk, items), NEG_INF, F32)
        for _ in range(MOBA_TOPK):
            best = jnp.max(cur, axis=0, keepdims=True)
            cand = jnp.where((cur == best) & (best > -jnp.inf), blk_f, float(nblk))
            pick = blk_f == jnp.min(cand, axis=0, keepdims=True)
            bias = jnp.where(pick, 0.0, bias)
            cur = jnp.where(pick, -jnp.inf, cur)
        q_feat = (qt * LOG2E).astype(BF16)
        qa_ref[...] = jnp.concatenate([q_feat, bias.astype(BF16), feat_pad], axis=0)

        own_keys = pl.ds(pl.multiple_of(own * MOBA_BLOCK, MOBA_BLOCK), MOBA_BLOCK)
        s = _dot(ka_ref[kh, 0, own_keys, 0:HEAD_DIM], q_feat)
        s = jnp.where(causal, s, NEG_INF)
        m = jnp.max(s, axis=0, keepdims=True)
        acc_ref[...] = _dot(vt_ref[0, own, kh], jnp.exp2((s - m).astype(BF16)))

        def score_group(gi, kh=kh):
            keys = pl.ds(pl.multiple_of(gi * group_keys, group_keys), group_keys)
            return _dot(ka_ref[kh, 0, keys, :], qa_ref[...])

        def consume(s_cur_ref, gi, m_prev, kh=kh):
            s = s_cur_ref[...]
            m_new = jnp.maximum(m_prev, jnp.max(s, axis=0, keepdims=True))
            alpha = jnp.exp2(m_prev - m_new)
            pb = jnp.exp2((s - m_new).astype(BF16))
            n0 = gi * ATT_GROUP
            pv = _dot(vt_ref[0, n0, kh], pb[0:MOBA_BLOCK])
            for j in range(1, ATT_GROUP):
                pv = pv + _dot(vt_ref[0, n0 + j, kh], pb[j * MOBA_BLOCK:(j + 1) * MOBA_BLOCK])
            acc_ref[...] = alpha * acc_ref[...] + pv
            return m_new

        n_groups = (own + ATT_GROUP - 1) // ATT_GROUP
        last_group = nblk // ATT_GROUP - 1
        s0_ref[...] = score_group(0)

        def group_pair(pi, m_prev):
            g = 2 * pi
            s1_ref[...] = score_group(g + 1)
            m_mid = consume(s0_ref, g, m_prev)
            s0_ref[...] = score_group(jnp.minimum(g + 2, last_group))
            return consume(s1_ref, g + 1, m_mid)

        m = lax.fori_loop(0, n_groups // 2, group_pair, m)

        @pl.when(n_groups % 2 == 1)
        def _(m=m):
            consume(s0_ref, n_groups - 1, m)

        ot = acc_ref[0:HEAD_DIM, :] / acc_ref[HEAD_DIM:HEAD_DIM + 1, :]
        outs.extend(ot[:, i * MOBA_BLOCK:(i + 1) * MOBA_BLOCK].T for i in range(KV_GROUP))

    attn = jnp.concatenate(outs, axis=1).astype(BF16)
    o_ref[0] = x + _dot(attn, wo_ref[...])


def _attn_call(x3, g, w_q, w_o, ka, vt, km, invf_col):
    bsz, seq, _ = x3.shape
    nblk = seq // MOBA_BLOCK
    items = KV_GROUP * MOBA_BLOCK
    x_spec = pl.BlockSpec((1, MOBA_BLOCK, D_MODEL), lambda b, i: (b, i, 0))
    return pl.pallas_call(
        functools.partial(_attn_body, nblk=nblk),
        grid=(bsz, nblk),
        in_specs=[x_spec, _resident((1, D_MODEL)),
                  _resident((D_MODEL, D_MODEL)), _resident((D_MODEL, D_MODEL)),
                  pl.BlockSpec((N_KV_HEADS, 1, seq, 2 * HEAD_DIM), lambda b, i: (0, b, 0, 0),
                               pipeline_mode=pl.Buffered(1)),
                  pl.BlockSpec((1, nblk, N_KV_HEADS, V_ROWS, MOBA_BLOCK),
                               lambda b, i: (b, 0, 0, 0, 0), pipeline_mode=pl.Buffered(1)),
                  pl.BlockSpec((1, N_KV_HEADS, nblk, HEAD_DIM), lambda b, i: (b, 0, 0, 0)),
                  _resident((ROPE_DIM // 2, 1))],
        out_specs=x_spec,
        out_shape=jax.ShapeDtypeStruct(x3.shape, F32),
        scratch_shapes=[pltpu.VMEM((2 * HEAD_DIM, items), BF16),
                        pltpu.VMEM((V_ROWS, items), F32),
                        pltpu.VMEM((ATT_GROUP * MOBA_BLOCK, items), F32),
                        pltpu.VMEM((ATT_GROUP * MOBA_BLOCK, items), F32)],
        compiler_params=pltpu.CompilerParams(
            dimension_semantics=("arbitrary", "arbitrary"), vmem_limit_bytes=VMEM_LIMIT),
        name="moba_attn",
    )(x3, g.reshape(1, D_MODEL), w_q.astype(BF16), w_o.astype(BF16), ka, vt, km, invf_col)


def kernel(x, norm_g, ffn_w_in, ffn_w_out, s5_a_re, s5_a_im, s5_log_step, s5_b_re, s5_b_im,
           s5_c_re, s5_c_im, s5_d, s5_w_glu, kv_norm_g, w_k, w_v, w_q, w_o, final_g):
    bsz, seq, _ = x.shape
    assert seq % S5_CHUNK == 0 and seq % MOBA_BLOCK == 0
    assert (seq // MOBA_BLOCK) % (2 * SUBLANES) == 0 and seq // MOBA_BLOCK <= HEAD_DIM
    assert (seq // MOBA_BLOCK) % ATT_GROUP == 0
    assert (bsz * seq) % FFN_ROWS == 0
    nblk = seq // MOBA_BLOCK
    x2 = x.reshape(bsz * seq, D_MODEL)
    invf, sign = _rope_consts()

    x2 = _ffn_call(x2, norm_g[0, 0], ffn_w_in[0, 0], ffn_w_out[0, 0])
    prep = _s5_prep_call(s5_a_re[0], s5_a_im[0], s5_log_step[0], s5_b_re[0], s5_b_im[0], s5_c_im[0])
    x2 = _s5_call(x2, bsz, seq, norm_g[0, 1], prep, None, s5_c_re[0], s5_d[0], s5_w_glu[0])
    x2 = _ffn_call(x2, norm_g[0, 2], ffn_w_in[0, 1], ffn_w_out[0, 1])

    ka, vt, km = _kv_call(x2, seq, kv_norm_g, w_k, w_v, invf, sign)
    ka = ka.reshape(N_KV_HEADS, bsz, seq, 2 * HEAD_DIM)
    vt = vt.reshape(bsz, nblk, N_KV_HEADS, V_ROWS, MOBA_BLOCK)
    km = km.reshape(bsz, nblk, N_KV_HEADS, HEAD_DIM).transpose(0, 2, 1, 3)

    x2 = _ffn_call(x2, norm_g[1, 0], ffn_w_in[1, 0], ffn_w_out[1, 0])
    x3 = _attn_call(x2.reshape(bsz, seq, D_MODEL), norm_g[1, 1], w_q[0], w_o[0], ka, vt, km,
                    invf[0, 0:ROPE_DIM // 2].reshape(ROPE_DIM // 2, 1))
    x2 = _ffn_call(x3.reshape(bsz * seq, D_MODEL), norm_g[1, 2], ffn_w_in[1, 1], ffn_w_out[1, 1],
                   final_g=final_g)
    return x2.reshape(bsz, seq, D_MODEL)
```

```python
import functools

import jax
import jax.numpy as jnp
import numpy as np
from jax import lax
from jax.experimental import pallas as pl
from jax.experimental.pallas import tpu as pltpu

F32 = jnp.float32
BF16 = jnp.bfloat16

D_MODEL = 1024
D_FF = 2816
RMS_EPS = 1e-6
S5_GROUP = 16
S5_GROUPS = D_MODEL // S5_GROUP
S5_STATE = 64
N_HEADS = 8
HEAD_DIM = 128
N_KV_HEADS = 2
KV_GROUP = N_HEADS // N_KV_HEADS
ROPE_DIM = HEAD_DIM // 4
ROPE_THETA = 500000.0
MOBA_BLOCK = 256
MOBA_TOPK = 3
NEG_INF = -1e30
LOG2E = 1.4426950408889634

LANES = 128
SUBLANES = 8
VMEM_LIMIT = 56 * 1024 * 1024

FFN_ROWS = 512
FFN_COLS = 256

ATT_GROUP = 4
V_ROWS = HEAD_DIM + 2 * SUBLANES

S5_CHUNK = 256
S5_SUBSEQ = SUBLANES
S5_STEPS = S5_CHUNK // S5_SUBSEQ
S5_LANE_BLOCKS = D_MODEL // LANES
S5_GROUPS_PER_BLOCK = LANES // S5_GROUP
S5_HALF = S5_GROUPS_PER_BLOCK * S5_STATE
S5_NSTATE = S5_GROUPS * S5_STATE


def _rms(x, g):
    ms = jnp.mean(x * x, axis=-1, keepdims=True)
    return (x * lax.rsqrt(ms + RMS_EPS)) * g


def _dot(a, b):
    return jnp.dot(a, b, preferred_element_type=F32)


def _resident(shape):
    nd = len(shape)
    return pl.BlockSpec(shape, lambda *_: (0,) * nd, pipeline_mode=pl.Buffered(1))


def _ffn_body(*refs, mode, nblk):
    x_ref, g_ref, win_ref, wout_ref = refs[:4]
    x = x_ref[...]
    h = _rms(x, g_ref[...]).astype(BF16)
    acc = jnp.zeros(x.shape, F32)
    for c in range(D_FF // FFN_COLS):
        lo = c * FFN_COLS
        gate = _dot(h, win_ref[:, lo:lo + FFN_COLS])
        up = _dot(h, win_ref[:, D_FF + lo:D_FF + lo + FFN_COLS])
        act = (gate * jax.nn.sigmoid(gate)) * up
        acc = acc + _dot(act.astype(BF16), wout_ref[lo:lo + FFN_COLS, :])
    y = x + 0.5 * acc
    if mode == 'final':
        fg_ref, o_ref = refs[4:]
        y = _rms(y, fg_ref[...])
    elif mode == 'kv':
        kvg_ref, wkv_ref, invf_ref, tab_ref, o_ref, ka_ref, vt_ref, km_ref = refs[4:]
        for sub in range(FFN_ROWS // MOBA_BLOCK):
            rows = slice(sub * MOBA_BLOCK, (sub + 1) * MOBA_BLOCK)
            blk = (pl.program_id(0) * (FFN_ROWS // MOBA_BLOCK) + sub) % nblk
            _kv_tile(y[rows], blk, kvg_ref[...], wkv_ref, invf_ref[...], tab_ref,
                     ka_ref, vt_ref, km_ref, sub)
    else:
        o_ref, = refs[4:]
    o_ref[...] = y


def _ffn_call(x2, g, w_in_all, w_out_all, layer, idx, final_g=None, kv=None, nblk=None):
    t = x2.shape[0]
    mode = 'final' if final_g is not None else ('kv' if kv is not None else 'plain')
    row_spec = pl.BlockSpec((FFN_ROWS, D_MODEL), lambda i: (i, 0))
    pick = lambda i: (layer, idx, 0, 0)
    in_specs = [row_spec, _resident((1, D_MODEL)),
                pl.BlockSpec((None, None, D_MODEL, 2 * D_FF), pick, pipeline_mode=pl.Buffered(1)),
                pl.BlockSpec((None, None, D_FF, D_MODEL), pick, pipeline_mode=pl.Buffered(1))]
    args = [x2, g.reshape(1, D_MODEL), w_in_all, w_out_all]
    out_specs = row_spec
    out_shape = jax.ShapeDtypeStruct((t, D_MODEL), F32)
    if mode == 'final':
        in_specs.append(_resident((1, D_MODEL)))
        args.append(final_g.reshape(1, D_MODEL))
    elif mode == 'kv':
        kvg, wkv, invf, rope_tab = kv
        kdim = N_KV_HEADS * HEAD_DIM
        sub = FFN_ROWS // MOBA_BLOCK
        ntile = t // MOBA_BLOCK
        in_specs += [_resident((1, D_MODEL)), _resident((D_MODEL, 2 * kdim)),
                     _resident((1, HEAD_DIM)), _resident(rope_tab.shape)]
        args += [kvg.reshape(1, D_MODEL), wkv, invf, rope_tab]
        out_specs = [row_spec,
                     pl.BlockSpec((N_KV_HEADS, FFN_ROWS, 2 * HEAD_DIM), lambda i: (0, i, 0)),
                     pl.BlockSpec((sub, N_KV_HEADS, V_ROWS, MOBA_BLOCK), lambda i: (i, 0, 0, 0)),
                     pl.BlockSpec((sub, 1, kdim), lambda i: (i, 0, 0))]
        out_shape = [out_shape,
                     jax.ShapeDtypeStruct((N_KV_HEADS, t, 2 * HEAD_DIM), BF16),
                     jax.ShapeDtypeStruct((ntile, N_KV_HEADS, V_ROWS, MOBA_BLOCK), BF16),
                     jax.ShapeDtypeStruct((ntile, 1, kdim), F32)]
    return pl.pallas_call(
        functools.partial(_ffn_body, mode=mode, nblk=nblk),
        grid=(t // FFN_ROWS,),
        in_specs=in_specs,
        out_specs=out_specs,
        out_shape=out_shape,
        compiler_params=pltpu.CompilerParams(
            dimension_semantics=("arbitrary",), vmem_limit_bytes=VMEM_LIMIT),
        name="ffn_" + mode,
    )(*args)


def _s5_prep_body(lr_ref, li_ref, ls_ref, bre_ref, bim_ref, cim_ref,
                  ar_ref, ai_ref, amr_ref, ami_ref, pr_ref, pi_ref, btr_ref, bti_ref, cneg_ref):
    lr = lr_ref[...]
    li = li_ref[...]
    dt = jnp.exp(ls_ref[...])
    mag = jnp.exp(lr * dt)
    abar_re = mag * jnp.cos(li * dt)
    abar_im = mag * jnp.sin(li * dt)
    ar_ref[...] = abar_re
    ai_ref[...] = abar_im
    nr, ni = abar_re - 1.0, abar_im
    den = lr * lr + li * li
    coef_re = (nr * lr + ni * li) / den
    coef_im = (ni * lr - nr * li) / den
    k = (lax.broadcasted_iota(jnp.int32, (S5_STEPS, 1), 0) + 1).astype(F32)
    pmag = jnp.exp((lr * dt) * k)
    pang = (li * dt) * k
    pr = pmag * jnp.cos(pang)
    pi = pmag * jnp.sin(pang)
    pr_ref[...] = pr
    pi_ref[...] = pi
    amr_ref[...] = pr[S5_STEPS - 1:S5_STEPS, :]
    ami_ref[...] = pi[S5_STEPS - 1:S5_STEPS, :]
    bre = bre_ref[...]
    bim = bim_ref[...]
    btr_ref[...] = coef_re * bre - coef_im * bim
    bti_ref[...] = coef_re * bim + coef_im * bre
    cneg_ref[...] = -cim_ref[...]


def _s5_prep_call(a_re, a_im, log_step, b_re, b_im, c_im):
    n = S5_NSTATE
    row = lambda v: v.reshape(1, n)
    chan_major = lambda v: v.transpose(2, 0, 1).reshape(S5_GROUP, n)
    ls = jnp.repeat(log_step, S5_STATE)
    outs = pl.pallas_call(
        _s5_prep_body,
        out_shape=[jax.ShapeDtypeStruct((1, n), F32)] * 4
        + [jax.ShapeDtypeStruct((S5_STEPS, n), F32)] * 2
        + [jax.ShapeDtypeStruct((S5_GROUP, n), F32)] * 3,
        name="s5_prep",
    )(row(a_re), row(a_im), row(ls), chan_major(b_re), chan_major(b_im),
      c_im.transpose(1, 0, 2).reshape(S5_GROUP, n))
    return outs


def _s5_body(x_ref, g_ref, perm_ref, permt_ref, bblk_ref, cblk_ref, ar_ref, ai_ref,
             amr_ref, ami_ref, pr_ref, pi_ref, d_ref, wglu_ref, o_ref,
             xs_ref, st_ref, c_ref, hb_ref):
    @pl.when(pl.program_id(1) == 0)
    def _():
        st_ref[...] = jnp.zeros(st_ref.shape, F32)

    x = x_ref[...]
    u = _rms(x, g_ref[...])
    u_hi = u.astype(BF16)
    u_lo = (u - u_hi.astype(F32)).astype(BF16)
    perm = perm_ref[...]
    up_hi = _dot(perm, u_hi)
    up = up_hi + _dot(perm, u_lo)
    ub = up_hi.astype(BF16)

    last = SUBLANES * (S5_STEPS - 1)
    ys = []
    for j in range(S5_LANE_BLOCKS):
        xs_ref[j] = _dot(ub[:, j * LANES:(j + 1) * LANES], bblk_ref[j])

        a_r = jnp.broadcast_to(ar_ref[j], (SUBLANES, S5_HALF))
        a_i = jnp.broadcast_to(ai_ref[j], (SUBLANES, S5_HALF))

        h_r = xs_ref[j, 0:SUBLANES, 0:S5_HALF]
        h_i = xs_ref[j, 0:SUBLANES, S5_HALF:2 * S5_HALF]
        for k in range(1, S5_STEPS):
            rows = slice(k * SUBLANES, (k + 1) * SUBLANES)
            h_r, h_i = (a_r * h_r - a_i * h_i + xs_ref[j, rows, 0:S5_HALF],
                        a_r * h_i + a_i * h_r + xs_ref[j, rows, S5_HALF:2 * S5_HALF])
            xs_ref[j, rows, 0:S5_HALF] = h_r
            xs_ref[j, rows, S5_HALF:2 * S5_HALF] = h_i

        am_r = amr_ref[j]
        am_i = ami_ref[j]
        c_r = st_ref[j, :, 0:S5_HALF]
        c_i = st_ref[j, :, S5_HALF:2 * S5_HALF]
        for i in range(S5_SUBSEQ):
            c_ref[j, i:i + 1, 0:S5_HALF] = c_r
            c_ref[j, i:i + 1, S5_HALF:2 * S5_HALF] = c_i
            e_r = xs_ref[j, last + i:last + i + 1, 0:S5_HALF]
            e_i = xs_ref[j, last + i:last + i + 1, S5_HALF:2 * S5_HALF]
            c_r, c_i = am_r * c_r - am_i * c_i + e_r, am_r * c_i + am_i * c_r + e_i
        st_ref[j, :, 0:S5_HALF] = c_r
        st_ref[j, :, S5_HALF:2 * S5_HALF] = c_i

        cc_r = jnp.concatenate([c_ref[j, :, 0:S5_HALF]] * 2, axis=0)
        cc_i = jnp.concatenate([c_ref[j, :, S5_HALF:2 * S5_HALF]] * 2, axis=0)
        for k in range(S5_STEPS // 2):
            rows = slice(2 * k * SUBLANES, 2 * (k + 1) * SUBLANES)
            p_r = pr_ref[j, rows, :]
            p_i = pi_ref[j, rows, :]
            t_r = xs_ref[j, rows, 0:S5_HALF] + (p_r * cc_r - p_i * cc_i)
            t_i = xs_ref[j, rows, S5_HALF:2 * S5_HALF] + (p_r * cc_i + p_i * cc_r)
            hb_ref[j, rows, :] = jnp.concatenate([t_r, t_i], axis=1).astype(BF16)

        ys.append(_dot(hb_ref[j], cblk_ref[j]))

    y = jnp.concatenate(ys, axis=1) + d_ref[...] * up
    y = 0.5 * y * (1.0 + jnp.tanh(np.sqrt(2.0 / np.pi).astype(np.float32)
                                  * (y + 0.044715 * (y * y * y))))
    yn = _dot(permt_ref[...], y.astype(BF16)).astype(BF16)
    z = _dot(yn, wglu_ref[...])
    o_ref[...] = x + z[:, 0:D_MODEL] * jax.nn.sigmoid(z[:, D_MODEL:2 * D_MODEL])


def _s5_call(x2, bsz, seq, g, prep, b_unused, c_re, d_skip, w_glu):
    del b_unused
    ar, ai, amr, ami, pr, pi, btr, bti, cneg = prep
    nb, gb, ns = S5_LANE_BLOCKS, S5_GROUPS_PER_BLOCK, S5_STATE
    eye = jnp.eye(gb, dtype=F32)

    def per_block(v, rows):
        return v.reshape(rows, nb, S5_HALF).transpose(1, 0, 2)

    bt = jnp.stack([btr, bti], axis=1).reshape(S5_GROUP, 2, nb, gb, ns)
    bblk = jnp.einsum('crjgp,gh->jgcrhp', bt, eye).reshape(nb, LANES, 2 * S5_HALF).astype(BF16)
    cmat = jnp.stack([c_re.transpose(1, 0, 2).reshape(S5_GROUP, S5_NSTATE), cneg], axis=1)
    cmat = cmat.reshape(S5_GROUP, 2, nb, gb, ns)
    cblk = jnp.einsum('crjgp,gh->jrgphc', cmat, eye).reshape(nb, 2 * S5_HALF, LANES).astype(BF16)

    r = np.arange(S5_CHUNK)
    perm_np = np.zeros((S5_CHUNK, S5_CHUNK), np.float32)
    perm_np[r, (r % SUBLANES) * S5_STEPS + r // SUBLANES] = 1.0
    perm = jnp.asarray(perm_np, BF16)
    permt = jnp.asarray(perm_np.T, BF16)

    nchunk = seq // S5_CHUNK
    row_spec = pl.BlockSpec((S5_CHUNK, D_MODEL), lambda b, c: (b * nchunk + c, 0))
    in_specs = [
        row_spec, _resident((1, D_MODEL)),
        _resident((S5_CHUNK, S5_CHUNK)), _resident((S5_CHUNK, S5_CHUNK)),
        _resident((nb, LANES, 2 * S5_HALF)), _resident((nb, 2 * S5_HALF, LANES)),
        _resident((nb, 1, S5_HALF)), _resident((nb, 1, S5_HALF)),
        _resident((nb, 1, S5_HALF)), _resident((nb, 1, S5_HALF)),
        _resident((nb, S5_CHUNK, S5_HALF)), _resident((nb, S5_CHUNK, S5_HALF)),
        _resident((1, D_MODEL)), _resident((D_MODEL, 2 * D_MODEL)),
    ]
    return pl.pallas_call(
        _s5_body,
        grid=(bsz, nchunk),
        in_specs=in_specs,
        out_specs=row_spec,
        out_shape=jax.ShapeDtypeStruct(x2.shape, F32),
        scratch_shapes=[
            pltpu.VMEM((nb, S5_CHUNK, 2 * S5_HALF), F32),
            pltpu.VMEM((nb, 1, 2 * S5_HALF), F32),
            pltpu.VMEM((nb, S5_SUBSEQ, 2 * S5_HALF), F32),
            pltpu.VMEM((nb, S5_CHUNK, 2 * S5_HALF), BF16),
        ],
        compiler_params=pltpu.CompilerParams(
            dimension_semantics=("arbitrary", "arbitrary"), vmem_limit_bytes=VMEM_LIMIT),
        name="s5",
    )(x2, g.reshape(1, D_MODEL), perm, permt, bblk, cblk,
      per_block(ar, 1), per_block(ai, 1), per_block(amr, 1), per_block(ami, 1),
      per_block(jnp.repeat(pr, SUBLANES, axis=0), S5_CHUNK),
      per_block(jnp.repeat(pi, SUBLANES, axis=0), S5_CHUNK),
      d_skip.reshape(1, D_MODEL), w_glu.astype(BF16))


def _rope_tables(pos0, invf, tab_ref):
    ang0 = pos0.astype(F32) * invf
    c0, s0 = jnp.cos(ang0), jnp.sin(ang0)
    cos_t = c0 * tab_ref[0] - s0 * tab_ref[1]
    sin_t = s0 * tab_ref[2] + c0 * tab_ref[3]
    return cos_t, sin_t


def _rope_head(xh, cos_t, sin_t, low_half):
    half = ROPE_DIM // 2
    swapped = jnp.where(low_half, pltpu.roll(xh, LANES - half, axis=1), pltpu.roll(xh, half, axis=1))
    return xh * cos_t + swapped * sin_t


def _rope_consts():
    half = ROPE_DIM // 2
    inv_freq = ROPE_THETA ** (-jnp.arange(0, ROPE_DIM, 2, dtype=F32) / ROPE_DIM)
    pad = jnp.zeros((HEAD_DIM - ROPE_DIM,), F32)
    invf = jnp.concatenate([inv_freq, inv_freq, pad]).reshape(1, HEAD_DIM)
    sign = jnp.concatenate([-jnp.ones((half,), F32), jnp.ones((half,), F32), pad]).reshape(1, HEAD_DIM)
    ang_r = jnp.arange(MOBA_BLOCK, dtype=F32)[:, None] * invf
    cos_r, sin_r = jnp.cos(ang_r), jnp.sin(ang_r)
    tables = jnp.stack([cos_r, sin_r, cos_r * sign, sin_r * sign])
    return invf, tables


def _kv_tile(x, blk, g, wkv_ref, invf, tab_ref, ka_ref, vt_ref, km_ref, sub):
    rows = slice(sub * MOBA_BLOCK, (sub + 1) * MOBA_BLOCK)
    h = _rms(x, g).astype(BF16)
    kv = _dot(h, wkv_ref[...])
    kdim = N_KV_HEADS * HEAD_DIM
    cos_t, sin_t = _rope_tables(blk * MOBA_BLOCK, invf, tab_ref)
    lane = lax.broadcasted_iota(jnp.int32, (MOBA_BLOCK, HEAD_DIM), 1)
    low_half = lane < ROPE_DIM // 2
    onehot = jnp.where(lane == blk, 1.0, 0.0).astype(BF16)
    pad_row = lax.broadcasted_iota(jnp.int32, (V_ROWS - HEAD_DIM, MOBA_BLOCK), 0)
    ones_rows = jnp.where(pad_row == 0, 1.0, 0.0).astype(BF16)
    means = []
    for hh in range(N_KV_HEADS):
        kh = _rope_head(kv[:, hh * HEAD_DIM:(hh + 1) * HEAD_DIM], cos_t, sin_t, low_half)
        means.append(jnp.mean(kh, axis=0, keepdims=True))
        ka_ref[hh, rows, :] = jnp.concatenate([kh.astype(BF16), onehot], axis=1)
        vt = kv[:, kdim + hh * HEAD_DIM:kdim + (hh + 1) * HEAD_DIM].T.astype(BF16)
        vt_ref[sub, hh] = jnp.concatenate([vt, ones_rows], axis=0)
    km_ref[sub] = jnp.concatenate(means, axis=1)


def _split_bf16(v):
    hi = v.astype(BF16)
    return hi, (v - hi.astype(F32)).astype(BF16)


def _attn_body(x_ref, g_ref, wq_ref, wo_ref, ka_ref, vt_ref, km_ref, invf_ref, o_ref,
               qa_ref, acc_ref, s0_ref, s1_ref, *, nblk):
    own = pl.program_id(1)
    items = KV_GROUP * MOBA_BLOCK
    group_keys = ATT_GROUP * MOBA_BLOCK
    half = ROPE_DIM // 2
    x = x_ref[0]
    h = _rms(x, g_ref[...]).astype(BF16)
    q = _dot(h, wq_ref[...])
    pos = (own * MOBA_BLOCK + lax.broadcasted_iota(jnp.int32, (1, MOBA_BLOCK), 1)).astype(F32)
    ang = invf_ref[...] * pos
    cos_t, sin_t = jnp.cos(ang), jnp.sin(ang)

    def head_t(i):
        qt = q[:, i * HEAD_DIM:(i + 1) * HEAD_DIM].T
        x1, x2 = qt[0:half], qt[half:2 * half]
        rot = jnp.concatenate([x1 * cos_t - x2 * sin_t, x2 * cos_t + x1 * sin_t, qt[2 * half:]], axis=0)
        return rot * (HEAD_DIM ** -0.5)

    blk = lax.broadcasted_iota(jnp.int32, (nblk, items), 0)
    blk_f = blk.astype(F32)
    past = blk < own
    causal = (lax.broadcasted_iota(jnp.int32, (MOBA_BLOCK, items), 0)
              <= lax.broadcasted_iota(jnp.int32, (MOBA_BLOCK, items), 1) % MOBA_BLOCK)
    feat_pad = jnp.zeros((HEAD_DIM - nblk, items), BF16)

    outs = []
    for kh in range(N_KV_HEADS):
        qt = jnp.concatenate([head_t(kh * KV_GROUP + i) for i in range(KV_GROUP)], axis=1)
        q_hi, q_lo = _split_bf16(qt)
        k_hi, k_lo = _split_bf16(km_ref[0, kh])
        gate = _dot(k_hi, q_hi) + (_dot(k_hi, q_lo) + _dot(k_lo, q_hi))
        cur = jnp.where(past, gate, -jnp.inf)
        bias = jnp.full((nblk, items), NEG_INF, F32)
        for _ in range(MOBA_TOPK):
            best = jnp.max(cur, axis=0, keepdims=True)
            cand = jnp.where((cur == best) & (best > -jnp.inf), blk_f, float(nblk))
            pick = blk_f == jnp.min(cand, axis=0, keepdims=True)
            bias = jnp.where(pick, 0.0, bias)
            cur = jnp.where(pick, -jnp.inf, cur)
        q_feat = (qt * LOG2E).astype(BF16)
        qa_ref[...] = jnp.concatenate([q_feat, bias.astype(BF16), feat_pad], axis=0)

        own_keys = pl.ds(pl.multiple_of(own * MOBA_BLOCK, MOBA_BLOCK), MOBA_BLOCK)
        s = _dot(ka_ref[kh, 0, own_keys, 0:HEAD_DIM], q_feat)
        s = jnp.where(causal, s, NEG_INF)
        m = jnp.max(s, axis=0, keepdims=True)
        acc_ref[...] = _dot(vt_ref[0, own, kh], jnp.exp2((s - m).astype(BF16)))

        def score_group(gi, kh=kh):
            keys = pl.ds(pl.multiple_of(gi * group_keys, group_keys), group_keys)
            return _dot(ka_ref[kh, 0, keys, :], qa_ref[...])

        def consume(s_cur_ref, gi, m_prev, kh=kh):
            s = s_cur_ref[...]
            m_new = jnp.maximum(m_prev, jnp.max(s, axis=0, keepdims=True))
            alpha = jnp.exp2(m_prev - m_new)
            pb = jnp.exp2((s - m_new).astype(BF16))
            n0 = gi * ATT_GROUP
            pv = _dot(vt_ref[0, n0, kh], pb[0:MOBA_BLOCK])
            for j in range(1, ATT_GROUP):
                pv = pv + _dot(vt_ref[0, n0 + j, kh], pb[j * MOBA_BLOCK:(j + 1) * MOBA_BLOCK])
            acc_ref[...] = alpha * acc_ref[...] + pv
            return m_new

        n_groups = (own + ATT_GROUP - 1) // ATT_GROUP
        last_group = nblk // ATT_GROUP - 1
        s0_ref[...] = score_group(0)

        def group_pair(pi, m_prev):
            g = 2 * pi
            s1_ref[...] = score_group(g + 1)
            m_mid = consume(s0_ref, g, m_prev)
            s0_ref[...] = score_group(jnp.minimum(g + 2, last_group))
            return consume(s1_ref, g + 1, m_mid)

        m = lax.fori_loop(0, n_groups // 2, group_pair, m)

        @pl.when(n_groups % 2 == 1)
        def _(m=m):
            consume(s0_ref, n_groups - 1, m)

        ot = acc_ref[0:HEAD_DIM, :] / acc_ref[HEAD_DIM:HEAD_DIM + 1, :]
        outs.extend(ot[:, i * MOBA_BLOCK:(i + 1) * MOBA_BLOCK].T for i in range(KV_GROUP))

    attn = jnp.concatenate(outs, axis=1).astype(BF16)
    o_ref[0] = x + _dot(attn, wo_ref[...])


def _attn_call(x3, g, w_q, w_o, ka, vt, km, invf_col):
    bsz, seq, _ = x3.shape
    nblk = seq // MOBA_BLOCK
    items = KV_GROUP * MOBA_BLOCK
    x_spec = pl.BlockSpec((1, MOBA_BLOCK, D_MODEL), lambda b, i: (b, i, 0))
    return pl.pallas_call(
        functools.partial(_attn_body, nblk=nblk),
        grid=(bsz, nblk),
        in_specs=[x_spec, _resident((1, D_MODEL)),
                  _resident((D_MODEL, D_MODEL)), _resident((D_MODEL, D_MODEL)),
                  pl.BlockSpec((N_KV_HEADS, 1, seq, 2 * HEAD_DIM), lambda b, i: (0, b, 0, 0),
                               pipeline_mode=pl.Buffered(1)),
                  pl.BlockSpec((1, nblk, N_KV_HEADS, V_ROWS, MOBA_BLOCK),
                               lambda b, i: (b, 0, 0, 0, 0), pipeline_mode=pl.Buffered(1)),
                  pl.BlockSpec((1, N_KV_HEADS, nblk, HEAD_DIM), lambda b, i: (b, 0, 0, 0)),
                  _resident((ROPE_DIM // 2, 1))],
        out_specs=x_spec,
        out_shape=jax.ShapeDtypeStruct(x3.shape, F32),
        scratch_shapes=[pltpu.VMEM((2 * HEAD_DIM, items), BF16),
                        pltpu.VMEM((V_ROWS, items), F32),
                        pltpu.VMEM((ATT_GROUP * MOBA_BLOCK, items), F32),
                        pltpu.VMEM((ATT_GROUP * MOBA_BLOCK, items), F32)],
        compiler_params=pltpu.CompilerParams(
            dimension_semantics=("arbitrary", "arbitrary"), vmem_limit_bytes=VMEM_LIMIT),
        name="moba_attn",
    )(x3, g.reshape(1, D_MODEL), w_q.astype(BF16), w_o.astype(BF16), ka, vt, km, invf_col)


def kernel(x, norm_g, ffn_w_in, ffn_w_out, s5_a_re, s5_a_im, s5_log_step, s5_b_re, s5_b_im,
           s5_c_re, s5_c_im, s5_d, s5_w_glu, kv_norm_g, w_k, w_v, w_q, w_o, final_g):
    bsz, seq, _ = x.shape
    assert seq % S5_CHUNK == 0 and seq % MOBA_BLOCK == 0
    assert (seq // MOBA_BLOCK) % (2 * SUBLANES) == 0 and seq // MOBA_BLOCK <= HEAD_DIM
    assert (seq // MOBA_BLOCK) % ATT_GROUP == 0
    assert (bsz * seq) % FFN_ROWS == 0 and FFN_ROWS % MOBA_BLOCK == 0
    nblk = seq // MOBA_BLOCK
    x2 = x.reshape(bsz * seq, D_MODEL)
    invf, rope_tab = _rope_consts()

    w_in_b = ffn_w_in.astype(BF16)
    w_out_b = ffn_w_out.astype(BF16)
    wkv = jnp.concatenate([w_k, w_v], axis=1).astype(BF16)

    x2 = _ffn_call(x2, norm_g[0, 0], w_in_b, w_out_b, 0, 0)
    prep = _s5_prep_call(s5_a_re[0], s5_a_im[0], s5_log_step[0], s5_b_re[0], s5_b_im[0], s5_c_im[0])
    x2 = _s5_call(x2, bsz, seq, norm_g[0, 1], prep, None, s5_c_re[0], s5_d[0], s5_w_glu[0])
    x2, ka, vt, km = _ffn_call(x2, norm_g[0, 2], w_in_b, w_out_b, 0, 1,
                               kv=(kv_norm_g, wkv, invf, rope_tab), nblk=nblk)
    ka = ka.reshape(N_KV_HEADS, bsz, seq, 2 * HEAD_DIM)
    vt = vt.reshape(bsz, nblk, N_KV_HEADS, V_ROWS, MOBA_BLOCK)
    km = km.reshape(bsz, nblk, N_KV_HEADS, HEAD_DIM).transpose(0, 2, 1, 3)

    x2 = _ffn_call(x2, norm_g[1, 0], w_in_b, w_out_b, 1, 0)
    x3 = _attn_call(x2.reshape(bsz, seq, D_MODEL), norm_g[1, 1], w_q[0], w_o[0], ka, vt, km,
                    invf[0, 0:ROPE_DIM // 2].reshape(ROPE_DIM // 2, 1))
    x2 = _ffn_call(x3.reshape(bsz * seq, D_MODEL), norm_g[1, 2], w_in_b, w_out_b, 1, 1,
                   final_g=final_g)
    return x2.reshape(bsz, seq, D_MODEL)
```

```python
import functools

import jax
import jax.numpy as jnp
import numpy as np
from jax import lax
from jax.experimental import pallas as pl
from jax.experimental.pallas import tpu as pltpu

F32 = jnp.float32
BF16 = jnp.bfloat16

D_MODEL = 1024
D_FF = 2816
RMS_EPS = 1e-6
S5_GROUP = 16
S5_GROUPS = D_MODEL // S5_GROUP
S5_STATE = 64
N_HEADS = 8
HEAD_DIM = 128
N_KV_HEADS = 2
KV_GROUP = N_HEADS // N_KV_HEADS
ROPE_DIM = HEAD_DIM // 4
ROPE_THETA = 500000.0
MOBA_BLOCK = 256
MOBA_TOPK = 3
NEG_INF = -1e30
LOG2E = 1.4426950408889634

LANES = 128
SUBLANES = 8
VMEM_LIMIT = 56 * 1024 * 1024

FFN_ROWS = 512
FFN_COLS = 256

ATT_GROUP = 4
V_ROWS = HEAD_DIM + 2 * SUBLANES

S5_CHUNK = 256
S5_SUBSEQ = SUBLANES
S5_STEPS = S5_CHUNK // S5_SUBSEQ
S5_LANE_BLOCKS = D_MODEL // LANES
S5_GROUPS_PER_BLOCK = LANES // S5_GROUP
S5_HALF = S5_GROUPS_PER_BLOCK * S5_STATE
S5_NSTATE = S5_GROUPS * S5_STATE
S5_GLU_COLS = 2 * D_MODEL // S5_LANE_BLOCKS


def _rms(x, g):
    ms = jnp.mean(x * x, axis=-1, keepdims=True)
    return (x * lax.rsqrt(ms + RMS_EPS)) * g


def _dot(a, b):
    return jnp.dot(a, b, preferred_element_type=F32)


def _resident(shape):
    nd = len(shape)
    return pl.BlockSpec(shape, lambda *_: (0,) * nd, pipeline_mode=pl.Buffered(1))


def _ffn_body(*refs, mode, nblk):
    x_ref, g_ref, win_ref, wout_ref = refs[:4]
    x = x_ref[...]
    h = _rms(x, g_ref[...]).astype(BF16)
    acc = jnp.zeros(x.shape, F32)
    for c in range(D_FF // FFN_COLS):
        lo = c * FFN_COLS
        gate = _dot(h, win_ref[:, lo:lo + FFN_COLS])
        up = _dot(h, win_ref[:, D_FF + lo:D_FF + lo + FFN_COLS])
        act = (gate * jax.nn.sigmoid(gate)) * up
        acc = acc + _dot(act.astype(BF16), wout_ref[lo:lo + FFN_COLS, :])
    y = x + 0.5 * acc
    if mode == 'final':
        fg_ref, o_ref = refs[4:]
        y = _rms(y, fg_ref[...])
    elif mode == 'kv':
        kvg_ref, wkv_ref, invf_ref, tab_ref, o_ref, ka_ref, vt_ref, km_ref = refs[4:]
        for sub in range(FFN_ROWS // MOBA_BLOCK):
            rows = slice(sub * MOBA_BLOCK, (sub + 1) * MOBA_BLOCK)
            blk = (pl.program_id(0) * (FFN_ROWS // MOBA_BLOCK) + sub) % nblk
            _kv_tile(y[rows], blk, kvg_ref[...], wkv_ref, invf_ref[...], tab_ref,
                     ka_ref, vt_ref, km_ref, sub)
    else:
        o_ref, = refs[4:]
    o_ref[...] = y


def _ffn_call(x2, g, w_in_all, w_out_all, layer, idx, final_g=None, kv=None, nblk=None):
    t = x2.shape[0]
    mode = 'final' if final_g is not None else ('kv' if kv is not None else 'plain')
    row_spec = pl.BlockSpec((FFN_ROWS, D_MODEL), lambda i: (i, 0))
    pick = lambda i: (layer, idx, 0, 0)
    in_specs = [row_spec, _resident((1, D_MODEL)),
                pl.BlockSpec((None, None, D_MODEL, 2 * D_FF), pick, pipeline_mode=pl.Buffered(1)),
                pl.BlockSpec((None, None, D_FF, D_MODEL), pick, pipeline_mode=pl.Buffered(1))]
    args = [x2, g.reshape(1, D_MODEL), w_in_all, w_out_all]
    out_specs = row_spec
    out_shape = jax.ShapeDtypeStruct((t, D_MODEL), F32)
    if mode == 'final':
        in_specs.append(_resident((1, D_MODEL)))
        args.append(final_g.reshape(1, D_MODEL))
    elif mode == 'kv':
        kvg, wkv, invf, rope_tab = kv
        kdim = N_KV_HEADS * HEAD_DIM
        sub = FFN_ROWS // MOBA_BLOCK
        ntile = t // MOBA_BLOCK
        in_specs += [_resident((1, D_MODEL)), _resident((D_MODEL, 2 * kdim)),
                     _resident((1, HEAD_DIM)), _resident(rope_tab.shape)]
        args += [kvg.reshape(1, D_MODEL), wkv, invf, rope_tab]
        out_specs = [row_spec,
                     pl.BlockSpec((N_KV_HEADS, FFN_ROWS, 2 * HEAD_DIM), lambda i: (0, i, 0)),
                     pl.BlockSpec((sub, N_KV_HEADS, V_ROWS, MOBA_BLOCK), lambda i: (i, 0, 0, 0)),
                     pl.BlockSpec((sub, 1, kdim), lambda i: (i, 0, 0))]
        out_shape = [out_shape,
                     jax.ShapeDtypeStruct((N_KV_HEADS, t, 2 * HEAD_DIM), BF16),
                     jax.ShapeDtypeStruct((ntile, N_KV_HEADS, V_ROWS, MOBA_BLOCK), BF16),
                     jax.ShapeDtypeStruct((ntile, 1, kdim), F32)]
    return pl.pallas_call(
        functools.partial(_ffn_body, mode=mode, nblk=nblk),
        grid=(t // FFN_ROWS,),
        in_specs=in_specs,
        out_specs=out_specs,
        out_shape=out_shape,
        compiler_params=pltpu.CompilerParams(
            dimension_semantics=("arbitrary",), vmem_limit_bytes=VMEM_LIMIT),
        name="ffn_" + mode,
    )(*args)


def _s5_prep_body(lr_ref, li_ref, ls_ref, bre_ref, bim_ref, cim_ref,
                  ar_ref, ai_ref, amr_ref, ami_ref, pr_ref, pi_ref, btr_ref, bti_ref, cneg_ref):
    lr = lr_ref[...]
    li = li_ref[...]
    dt = jnp.exp(ls_ref[...])
    mag = jnp.exp(lr * dt)
    abar_re = mag * jnp.cos(li * dt)
    abar_im = mag * jnp.sin(li * dt)
    ar_ref[...] = abar_re
    ai_ref[...] = abar_im
    nr, ni = abar_re - 1.0, abar_im
    den = lr * lr + li * li
    coef_re = (nr * lr + ni * li) / den
    coef_im = (ni * lr - nr * li) / den
    k = (lax.broadcasted_iota(jnp.int32, (S5_STEPS, 1), 0) + 1).astype(F32)
    pmag = jnp.exp((lr * dt) * k)
    pang = (li * dt) * k
    pr = pmag * jnp.cos(pang)
    pi = pmag * jnp.sin(pang)
    pr_ref[...] = pr
    pi_ref[...] = pi
    amr_ref[...] = pr[S5_STEPS - 1:S5_STEPS, :]
    ami_ref[...] = pi[S5_STEPS - 1:S5_STEPS, :]
    bre = bre_ref[...]
    bim = bim_ref[...]
    btr_ref[...] = coef_re * bre - coef_im * bim
    bti_ref[...] = coef_re * bim + coef_im * bre
    cneg_ref[...] = -cim_ref[...]


def _s5_prep_call(a_re, a_im, log_step, b_re, b_im, c_im):
    n = S5_NSTATE
    row = lambda v: v.reshape(1, n)
    chan_major = lambda v: v.transpose(2, 0, 1).reshape(S5_GROUP, n)
    ls = jnp.repeat(log_step, S5_STATE)
    outs = pl.pallas_call(
        _s5_prep_body,
        out_shape=[jax.ShapeDtypeStruct((1, n), F32)] * 4
        + [jax.ShapeDtypeStruct((S5_STEPS, n), F32)] * 2
        + [jax.ShapeDtypeStruct((S5_GROUP, n), F32)] * 3,
        name="s5_prep",
    )(row(a_re), row(a_im), row(ls), chan_major(b_re), chan_major(b_im),
      c_im.transpose(1, 0, 2).reshape(S5_GROUP, n))
    return outs


def _s5_body(x_ref, xprev_ref, g_ref, perm_ref, permt_ref, bblk_ref, cblk_ref, ar_ref, ai_ref,
             amr_ref, ami_ref, pr_ref, pi_ref, d_ref, wglu_ref, o_ref,
             xs_ref, st_ref, c_ref, hb_ref, y_ref, *, nchunk):
    step = pl.program_id(0)

    @pl.when(step % nchunk == 0)
    def _():
        st_ref[...] = jnp.zeros(st_ref.shape, F32)

    @pl.when(step == 0)
    def _():
        y_ref[...] = jnp.zeros(y_ref.shape, F32)

    yp = y_ref[...]
    yp = 0.5 * yp * (1.0 + jnp.tanh(np.sqrt(2.0 / np.pi).astype(np.float32)
                                    * (yp + 0.044715 * (yp * yp * yp))))
    yn = _dot(permt_ref[...], yp.astype(BF16)).astype(BF16)

    def glu_piece(c):
        cols = slice(c * S5_GLU_COLS, (c + 1) * S5_GLU_COLS)
        gcols = slice(D_MODEL + c * S5_GLU_COLS, D_MODEL + (c + 1) * S5_GLU_COLS)
        val = _dot(yn, wglu_ref[:, cols])
        gate = _dot(yn, wglu_ref[:, gcols])
        o_ref[:, cols] = xprev_ref[:, cols] + val * jax.nn.sigmoid(gate)

    x = x_ref[...]
    u = _rms(x, g_ref[...])
    u_hi = u.astype(BF16)
    u_lo = (u - u_hi.astype(F32)).astype(BF16)
    perm = perm_ref[...]
    up_hi = _dot(perm, u_hi)
    up = up_hi + _dot(perm, u_lo)
    ub = up_hi.astype(BF16)

    last = SUBLANES * (S5_STEPS - 1)
    nb = S5_LANE_BLOCKS
    re, im = slice(0, S5_HALF), slice(S5_HALF, 2 * S5_HALF)

    def project_in(j):
        xs_ref[j] = _dot(ub[:, j * LANES:(j + 1) * LANES], bblk_ref[j])

    for j in range(nb):
        project_in(j)

    def scan_steps(j):
        a_r = jnp.broadcast_to(ar_ref[j], (SUBLANES, S5_HALF))
        a_i = jnp.broadcast_to(ai_ref[j], (SUBLANES, S5_HALF))
        state = [xs_ref[j, 0:SUBLANES, re], xs_ref[j, 0:SUBLANES, im]]

        def step(k):
            rows = slice(k * SUBLANES, (k + 1) * SUBLANES)
            h_r, h_i = state
            state[0] = a_r * h_r - a_i * h_i + xs_ref[j, rows, re]
            state[1] = a_r * h_i + a_i * h_r + xs_ref[j, rows, im]
            xs_ref[j, rows, re] = state[0]
            xs_ref[j, rows, im] = state[1]

        return [functools.partial(step, k) for k in range(1, S5_STEPS)]

    def entering_states(j):
        am_r, am_i = amr_ref[j], ami_ref[j]
        c_r, c_i = st_ref[j, :, re], st_ref[j, :, im]
        for i in range(S5_SUBSEQ):
            c_ref[j, i:i + 1, re] = c_r
            c_ref[j, i:i + 1, im] = c_i
            e_r = xs_ref[j, last + i:last + i + 1, re]
            e_i = xs_ref[j, last + i:last + i + 1, im]
            c_r, c_i = am_r * c_r - am_i * c_i + e_r, am_r * c_i + am_i * c_r + e_i
        st_ref[j, :, re] = c_r
        st_ref[j, :, im] = c_i

    def fix_steps(j):
        cc_r = jnp.concatenate([c_ref[j, :, re]] * 2, axis=0)
        cc_i = jnp.concatenate([c_ref[j, :, im]] * 2, axis=0)

        def step(k):
            rows = slice(2 * k * SUBLANES, 2 * (k + 1) * SUBLANES)
            p_r, p_i = pr_ref[j, rows, :], pi_ref[j, rows, :]
            t_r = xs_ref[j, rows, re] + (p_r * cc_r - p_i * cc_i)
            t_i = xs_ref[j, rows, im] + (p_r * cc_i + p_i * cc_r)
            hb_ref[j, rows, :] = jnp.concatenate([t_r, t_i], axis=1).astype(BF16)

        return [functools.partial(step, k) for k in range(S5_STEPS // 2)]

    ys = []
    for j in range(nb + 1):
        p1 = scan_steps(j) if j < nb else []
        p2 = fix_steps(j - 1) if j >= 1 else []
        for k in range(max(len(p2), (len(p1) + 1) // 2)):
            for f in p1[2 * k:2 * k + 2]:
                f()
            if k < len(p2):
                p2[k]()
        if j < nb:
            entering_states(j)
        if j >= 1:
            ys.append(_dot(hb_ref[j - 1], cblk_ref[j - 1]))
            if j % 2 == 1:
                glu_piece(j // 2)

    y_ref[...] = jnp.concatenate(ys, axis=1) + d_ref[...] * up


def _s5_call(x2, bsz, seq, g, prep, b_unused, c_re, d_skip, w_glu):
    del b_unused
    ar, ai, amr, ami, pr, pi, btr, bti, cneg = prep
    nb, gb, ns = S5_LANE_BLOCKS, S5_GROUPS_PER_BLOCK, S5_STATE
    eye = jnp.eye(gb, dtype=F32)

    def per_block(v, rows):
        return v.reshape(rows, nb, S5_HALF).transpose(1, 0, 2)

    bt = jnp.stack([btr, bti], axis=1).reshape(S5_GROUP, 2, nb, gb, ns)
    bblk = jnp.einsum('crjgp,gh->jgcrhp', bt, eye).reshape(nb, LANES, 2 * S5_HALF).astype(BF16)
    cmat = jnp.stack([c_re.transpose(1, 0, 2).reshape(S5_GROUP, S5_NSTATE), cneg], axis=1)
    cmat = cmat.reshape(S5_GROUP, 2, nb, gb, ns)
    cblk = jnp.einsum('crjgp,gh->jrgphc', cmat, eye).reshape(nb, 2 * S5_HALF, LANES).astype(BF16)

    r = np.arange(S5_CHUNK)
    perm_np = np.zeros((S5_CHUNK, S5_CHUNK), np.float32)
    perm_np[r, (r % SUBLANES) * S5_STEPS + r // SUBLANES] = 1.0
    perm = jnp.asarray(perm_np, BF16)
    permt = jnp.asarray(perm_np.T, BF16)

    nchunk = seq // S5_CHUNK
    total = bsz * nchunk
    cur_spec = pl.BlockSpec((S5_CHUNK, D_MODEL), lambda s: (jnp.minimum(s, total - 1), 0))
    prev_spec = pl.BlockSpec((S5_CHUNK, D_MODEL), lambda s: (jnp.maximum(s - 1, 0), 0))
    in_specs = [
        cur_spec, prev_spec, _resident((1, D_MODEL)),
        _resident((S5_CHUNK, S5_CHUNK)), _resident((S5_CHUNK, S5_CHUNK)),
        _resident((nb, LANES, 2 * S5_HALF)), _resident((nb, 2 * S5_HALF, LANES)),
        _resident((nb, 1, S5_HALF)), _resident((nb, 1, S5_HALF)),
        _resident((nb, 1, S5_HALF)), _resident((nb, 1, S5_HALF)),
        _resident((nb, S5_CHUNK, S5_HALF)), _resident((nb, S5_CHUNK, S5_HALF)),
        _resident((1, D_MODEL)), _resident((D_MODEL, 2 * D_MODEL)),
    ]
    return pl.pallas_call(
        functools.partial(_s5_body, nchunk=nchunk),
        grid=(total + 1,),
        in_specs=in_specs,
        out_specs=prev_spec,
        out_shape=jax.ShapeDtypeStruct(x2.shape, F32),
        scratch_shapes=[
            pltpu.VMEM((nb, S5_CHUNK, 2 * S5_HALF), F32),
            pltpu.VMEM((nb, 1, 2 * S5_HALF), F32),
            pltpu.VMEM((nb, S5_SUBSEQ, 2 * S5_HALF), F32),
            pltpu.VMEM((nb, S5_CHUNK, 2 * S5_HALF), BF16),
            pltpu.VMEM((S5_CHUNK, D_MODEL), F32),
        ],
        compiler_params=pltpu.CompilerParams(
            dimension_semantics=("arbitrary",), vmem_limit_bytes=VMEM_LIMIT),
        name="s5",
    )(x2, x2, g.reshape(1, D_MODEL), perm, permt, bblk, cblk,
      per_block(ar, 1), per_block(ai, 1), per_block(amr, 1), per_block(ami, 1),
      per_block(jnp.repeat(pr, SUBLANES, axis=0), S5_CHUNK),
      per_block(jnp.repeat(pi, SUBLANES, axis=0), S5_CHUNK),
      d_skip.reshape(1, D_MODEL), w_glu.astype(BF16))


def _rope_tables(pos0, invf, tab_ref):
    ang0 = pos0.astype(F32) * invf
    c0, s0 = jnp.cos(ang0), jnp.sin(ang0)
    cos_t = c0 * tab_ref[0] - s0 * tab_ref[1]
    sin_t = s0 * tab_ref[2] + c0 * tab_ref[3]
    return cos_t, sin_t


def _rope_head(xh, cos_t, sin_t, low_half):
    half = ROPE_DIM // 2
    swapped = jnp.where(low_half, pltpu.roll(xh, LANES - half, axis=1), pltpu.roll(xh, half, axis=1))
    return xh * cos_t + swapped * sin_t


def _rope_consts():
    half = ROPE_DIM // 2
    inv_freq = ROPE_THETA ** (-jnp.arange(0, ROPE_DIM, 2, dtype=F32) / ROPE_DIM)
    pad = jnp.zeros((HEAD_DIM - ROPE_DIM,), F32)
    invf = jnp.concatenate([inv_freq, inv_freq, pad]).reshape(1, HEAD_DIM)
    sign = jnp.concatenate([-jnp.ones((half,), F32), jnp.ones((half,), F32), pad]).reshape(1, HEAD_DIM)
    ang_r = jnp.arange(MOBA_BLOCK, dtype=F32)[:, None] * invf
    cos_r, sin_r = jnp.cos(ang_r), jnp.sin(ang_r)
    tables = jnp.stack([cos_r, sin_r, cos_r * sign, sin_r * sign])
    return invf, tables


def _kv_tile(x, blk, g, wkv_ref, invf, tab_ref, ka_ref, vt_ref, km_ref, sub):
    rows = slice(sub * MOBA_BLOCK, (sub + 1) * MOBA_BLOCK)
    h = _rms(x, g).astype(BF16)
    kv = _dot(h, wkv_ref[...])
    kdim = N_KV_HEADS * HEAD_DIM
    cos_t, sin_t = _rope_tables(blk * MOBA_BLOCK, invf, tab_ref)
    lane = lax.broadcasted_iota(jnp.int32, (MOBA_BLOCK, HEAD_DIM), 1)
    low_half = lane < ROPE_DIM // 2
    onehot = jnp.where(lane == blk, 1.0, 0.0).astype(BF16)
    pad_row = lax.broadcasted_iota(jnp.int32, (V_ROWS - HEAD_DIM, MOBA_BLOCK), 0)
    ones_rows = jnp.where(pad_row == 0, 1.0, 0.0).astype(BF16)
    means = []
    for hh in range(N_KV_HEADS):
        kh = _rope_head(kv[:, hh * HEAD_DIM:(hh + 1) * HEAD_DIM], cos_t, sin_t, low_half)
        means.append(jnp.mean(kh, axis=0, keepdims=True))
        ka_ref[hh, rows, :] = jnp.concatenate([kh.astype(BF16), onehot], axis=1)
        vt = kv[:, kdim + hh * HEAD_DIM:kdim + (hh + 1) * HEAD_DIM].T.astype(BF16)
        vt_ref[sub, hh] = jnp.concatenate([vt, ones_rows], axis=0)
    km_ref[sub] = jnp.concatenate(means, axis=1)


def _split_bf16(v):
    hi = v.astype(BF16)
    return hi, (v - hi.astype(F32)).astype(BF16)


def _attn_body(x_ref, g_ref, wq_ref, wo_ref, ka_ref, vt_ref, km_ref, invf_ref, o_ref,
               qa_ref, acc_ref, s0_ref, s1_ref, *, nblk):
    own = pl.program_id(1)
    items = KV_GROUP * MOBA_BLOCK
    group_keys = ATT_GROUP * MOBA_BLOCK
    half = ROPE_DIM // 2
    x = x_ref[0]
    h = _rms(x, g_ref[...]).astype(BF16)
    q = _dot(h, wq_ref[...])
    pos = (own * MOBA_BLOCK + lax.broadcasted_iota(jnp.int32, (1, MOBA_BLOCK), 1)).astype(F32)
    ang = invf_ref[...] * pos
    cos_t, sin_t = jnp.cos(ang), jnp.sin(ang)

    def head_t(i):
        qt = q[:, i * HEAD_DIM:(i + 1) * HEAD_DIM].T
        x1, x2 = qt[0:half], qt[half:2 * half]
        rot = jnp.concatenate([x1 * cos_t - x2 * sin_t, x2 * cos_t + x1 * sin_t, qt[2 * half:]], axis=0)
        return rot * (HEAD_DIM ** -0.5)

    blk = lax.broadcasted_iota(jnp.int32, (nblk, items), 0)
    blk_f = blk.astype(F32)
    past = blk < own
    causal = (lax.broadcasted_iota(jnp.int32, (MOBA_BLOCK, items), 0)
              <= lax.broadcasted_iota(jnp.int32, (MOBA_BLOCK, items), 1) % MOBA_BLOCK)
    feat_pad = jnp.zeros((HEAD_DIM - nblk, items), BF16)

    outs = []
    for kh in range(N_KV_HEADS):
        qt = jnp.concatenate([head_t(kh * KV_GROUP + i) for i in range(KV_GROUP)], axis=1)
        q_hi, q_lo = _split_bf16(qt)
        k_hi, k_lo = _split_bf16(km_ref[0, kh])
        gate = _dot(k_hi, q_hi) + (_dot(k_hi, q_lo) + _dot(k_lo, q_hi))
        cur = jnp.where(past, gate, -jnp.inf)
        bias = jnp.full((nblk, items), NEG_INF, F32)
        for _ in range(MOBA_TOPK):
            best = jnp.max(cur, axis=0, keepdims=True)
            cand = jnp.where((cur == best) & (best > -jnp.inf), blk_f, float(nblk))
            pick = blk_f == jnp.min(cand, axis=0, keepdims=True)
            bias = jnp.where(pick, 0.0, bias)
            cur = jnp.where(pick, -jnp.inf, cur)
        q_feat = (qt * LOG2E).astype(BF16)
        qa_ref[...] = jnp.concatenate([q_feat, bias.astype(BF16), feat_pad], axis=0)

        own_keys = pl.ds(pl.multiple_of(own * MOBA_BLOCK, MOBA_BLOCK), MOBA_BLOCK)
        s = _dot(ka_ref[kh, 0, own_keys, 0:HEAD_DIM], q_feat)
        s = jnp.where(causal, s, NEG_INF)
        m = jnp.max(s, axis=0, keepdims=True)
        acc_ref[...] = _dot(vt_ref[0, own, kh], jnp.exp2((s - m).astype(BF16)))

        def score_group(gi, kh=kh):
            keys = pl.ds(pl.multiple_of(gi * group_keys, group_keys), group_keys)
            return _dot(ka_ref[kh, 0, keys, :], qa_ref[...])

        def consume(s_cur_ref, gi, m_prev, kh=kh):
            s = s_cur_ref[...]
            m_new = jnp.maximum(m_prev, jnp.max(s, axis=0, keepdims=True))
            alpha = jnp.exp2(m_prev - m_new)
            pb = jnp.exp2((s - m_new).astype(BF16))
            n0 = gi * ATT_GROUP
            pv = _dot(vt_ref[0, n0, kh], pb[0:MOBA_BLOCK])
            for j in range(1, ATT_GROUP):
                pv = pv + _dot(vt_ref[0, n0 + j, kh], pb[j * MOBA_BLOCK:(j + 1) * MOBA_BLOCK])
            acc_ref[...] = alpha * acc_ref[...] + pv
            return m_new

        n_groups = (own + ATT_GROUP - 1) // ATT_GROUP
        last_group = nblk // ATT_GROUP - 1
        s0_ref[...] = score_group(0)

        def group_pair(pi, m_prev):
            g = 2 * pi
            s1_ref[...] = score_group(g + 1)
            m_mid = consume(s0_ref, g, m_prev)
            s0_ref[...] = score_group(jnp.minimum(g + 2, last_group))
            return consume(s1_ref, g + 1, m_mid)

        m = lax.fori_loop(0, n_groups // 2, group_pair, m)

        @pl.when(n_groups % 2 == 1)
        def _(m=m):
            consume(s0_ref, n_groups - 1, m)

        ot = acc_ref[0:HEAD_DIM, :] / acc_ref[HEAD_DIM:HEAD_DIM + 1, :]
        outs.extend(ot[:, i * MOBA_BLOCK:(i + 1) * MOBA_BLOCK].T for i in range(KV_GROUP))

    attn = jnp.concatenate(outs, axis=1).astype(BF16)
    o_ref[0] = x + _dot(attn, wo_ref[...])


def _attn_call(x3, g, w_q, w_o, ka, vt, km, invf_col):
    bsz, seq, _ = x3.shape
    nblk = seq // MOBA_BLOCK
    items = KV_GROUP * MOBA_BLOCK
    x_spec = pl.BlockSpec((1, MOBA_BLOCK, D_MODEL), lambda b, i: (b, i, 0))
    return pl.pallas_call(
        functools.partial(_attn_body, nblk=nblk),
        grid=(bsz, nblk),
        in_specs=[x_spec, _resident((1, D_MODEL)),
                  _resident((D_MODEL, D_MODEL)), _resident((D_MODEL, D_MODEL)),
                  pl.BlockSpec((N_KV_HEADS, 1, seq, 2 * HEAD_DIM), lambda b, i: (0, b, 0, 0),
                               pipeline_mode=pl.Buffered(1)),
                  pl.BlockSpec((1, nblk, N_KV_HEADS, V_ROWS, MOBA_BLOCK),
                               lambda b, i: (b, 0, 0, 0, 0), pipeline_mode=pl.Buffered(1)),
                  pl.BlockSpec((1, N_KV_HEADS, nblk, HEAD_DIM), lambda b, i: (b, 0, 0, 0)),
                  _resident((ROPE_DIM // 2, 1))],
        out_specs=x_spec,
        out_shape=jax.ShapeDtypeStruct(x3.shape, F32),
        scratch_shapes=[pltpu.VMEM((2 * HEAD_DIM, items), BF16),
                        pltpu.VMEM((V_ROWS, items), F32),
                        pltpu.VMEM((ATT_GROUP * MOBA_BLOCK, items), F32),
                        pltpu.VMEM((ATT_GROUP * MOBA_BLOCK, items), F32)],
        compiler_params=pltpu.CompilerParams(
            dimension_semantics=("arbitrary", "arbitrary"), vmem_limit_bytes=VMEM_LIMIT),
        name="moba_attn",
    )(x3, g.reshape(1, D_MODEL), w_q.astype(BF16), w_o.astype(BF16), ka, vt, km, invf_col)


def kernel(x, norm_g, ffn_w_in, ffn_w_out, s5_a_re, s5_a_im, s5_log_step, s5_b_re, s5_b_im,
           s5_c_re, s5_c_im, s5_d, s5_w_glu, kv_norm_g, w_k, w_v, w_q, w_o, final_g):
    bsz, seq, _ = x.shape
    assert seq % S5_CHUNK == 0 and seq % MOBA_BLOCK == 0
    assert (seq // MOBA_BLOCK) % (2 * SUBLANES) == 0 and seq // MOBA_BLOCK <= HEAD_DIM
    assert (seq // MOBA_BLOCK) % ATT_GROUP == 0
    assert (bsz * seq) % FFN_ROWS == 0 and FFN_ROWS % MOBA_BLOCK == 0
    nblk = seq // MOBA_BLOCK
    x2 = x.reshape(bsz * seq, D_MODEL)
    invf, rope_tab = _rope_consts()

    w_in_b = ffn_w_in.astype(BF16)
    w_out_b = ffn_w_out.astype(BF16)
    wkv = jnp.concatenate([w_k, w_v], axis=1).astype(BF16)

    x2 = _ffn_call(x2, norm_g[0, 0], w_in_b, w_out_b, 0, 0)
    prep = _s5_prep_call(s5_a_re[0], s5_a_im[0], s5_log_step[0], s5_b_re[0], s5_b_im[0], s5_c_im[0])
    x2 = _s5_call(x2, bsz, seq, norm_g[0, 1], prep, None, s5_c_re[0], s5_d[0], s5_w_glu[0])
    x2, ka, vt, km = _ffn_call(x2, norm_g[0, 2], w_in_b, w_out_b, 0, 1,
                               kv=(kv_norm_g, wkv, invf, rope_tab), nblk=nblk)
    ka = ka.reshape(N_KV_HEADS, bsz, seq, 2 * HEAD_DIM)
    vt = vt.reshape(bsz, nblk, N_KV_HEADS, V_ROWS, MOBA_BLOCK)
    km = km.reshape(bsz, nblk, N_KV_HEADS, HEAD_DIM).transpose(0, 2, 1, 3)

    x2 = _ffn_call(x2, norm_g[1, 0], w_in_b, w_out_b, 1, 0)
    x3 = _attn_call(x2.reshape(bsz, seq, D_MODEL), norm_g[1, 1], w_q[0], w_o[0], ka, vt, km,
                    invf[0, 0:ROPE_DIM // 2].reshape(ROPE_DIM // 2, 1))
    x2 = _ffn_call(x3.reshape(bsz * seq, D_MODEL), norm_g[1, 2], w_in_b, w_out_b, 1, 1,
                   final_g=final_g)
    return x2.reshape(bsz, seq, D_MODEL)
```

```python
import functools

import jax
import jax.numpy as jnp
import numpy as np
from jax import lax
from jax.experimental import pallas as pl
from jax.experimental.pallas import tpu as pltpu

F32 = jnp.float32
BF16 = jnp.bfloat16

D_MODEL = 1024
D_FF = 2816
RMS_EPS = 1e-6
S5_GROUP = 16
S5_GROUPS = D_MODEL // S5_GROUP
S5_STATE = 64
N_HEADS = 8
HEAD_DIM = 128
N_KV_HEADS = 2
KV_GROUP = N_HEADS // N_KV_HEADS
ROPE_DIM = HEAD_DIM // 4
ROPE_THETA = 500000.0
MOBA_BLOCK = 256
MOBA_TOPK = 3
NEG_INF = -1e30
LOG2E = 1.4426950408889634

LANES = 128
SUBLANES = 8
VMEM_LIMIT = 56 * 1024 * 1024

FFN_ROWS = 512
FFN_COLS = 256

ATT_GROUP = 4
V_ROWS = HEAD_DIM + 2 * SUBLANES

S5_CHUNK = 256
S5_SUBSEQ = SUBLANES
S5_STEPS = S5_CHUNK // S5_SUBSEQ
S5_LANE_BLOCKS = D_MODEL // LANES
S5_GROUPS_PER_BLOCK = LANES // S5_GROUP
S5_HALF = S5_GROUPS_PER_BLOCK * S5_STATE
S5_NSTATE = S5_GROUPS * S5_STATE
S5_GLU_COLS = 2 * D_MODEL // S5_LANE_BLOCKS


def _rms(x, g):
    ms = jnp.mean(x * x, axis=-1, keepdims=True)
    return (x * lax.rsqrt(ms + RMS_EPS)) * g


def _dot(a, b):
    return jnp.dot(a, b, preferred_element_type=F32)


def _resident(shape):
    nd = len(shape)
    return pl.BlockSpec(shape, lambda *_: (0,) * nd, pipeline_mode=pl.Buffered(1))


def _ffn_body(*refs, mode, nblk):
    x_ref, g_ref, win_ref, wout_ref = refs[:4]
    x = x_ref[...]
    h = _rms(x, g_ref[...]).astype(BF16)
    acc = jnp.zeros(x.shape, F32)
    for c in range(D_FF // FFN_COLS):
        lo = c * FFN_COLS
        gate = _dot(h, win_ref[:, lo:lo + FFN_COLS].astype(BF16))
        up = _dot(h, win_ref[:, D_FF + lo:D_FF + lo + FFN_COLS].astype(BF16))
        act = (gate * jax.nn.sigmoid(gate)) * up
        acc = acc + _dot(act.astype(BF16), wout_ref[lo:lo + FFN_COLS, :].astype(BF16))
    y = x + 0.5 * acc
    if mode == 'final':
        fg_ref, o_ref = refs[4:]
        y = _rms(y, fg_ref[...])
    elif mode == 'kv':
        kvg_ref, wkv_ref, invf_ref, tab_ref, o_ref, ka_ref, vt_ref, km_ref = refs[4:]
        for sub in range(FFN_ROWS // MOBA_BLOCK):
            rows = slice(sub * MOBA_BLOCK, (sub + 1) * MOBA_BLOCK)
            blk = (pl.program_id(0) * (FFN_ROWS // MOBA_BLOCK) + sub) % nblk
            _kv_tile(y[rows], blk, kvg_ref[...], wkv_ref, invf_ref[...], tab_ref,
                     ka_ref, vt_ref, km_ref, sub)
    else:
        o_ref, = refs[4:]
    o_ref[...] = y


def _ffn_call(x2, g, w_in_all, w_out_all, layer, idx, final_g=None, kv=None, nblk=None):
    t = x2.shape[0]
    mode = 'final' if final_g is not None else ('kv' if kv is not None else 'plain')
    row_spec = pl.BlockSpec((FFN_ROWS, D_MODEL), lambda i: (i, 0))
    pick = lambda i: (layer, idx, 0, 0)
    in_specs = [row_spec, _resident((1, D_MODEL)),
                pl.BlockSpec((None, None, D_MODEL, 2 * D_FF), pick, pipeline_mode=pl.Buffered(1)),
                pl.BlockSpec((None, None, D_FF, D_MODEL), pick, pipeline_mode=pl.Buffered(1))]
    args = [x2, g.reshape(1, D_MODEL), w_in_all, w_out_all]
    out_specs = row_spec
    out_shape = jax.ShapeDtypeStruct((t, D_MODEL), F32)
    if mode == 'final':
        in_specs.append(_resident((1, D_MODEL)))
        args.append(final_g.reshape(1, D_MODEL))
    elif mode == 'kv':
        kvg, wkv, invf, rope_tab = kv
        kdim = N_KV_HEADS * HEAD_DIM
        sub = FFN_ROWS // MOBA_BLOCK
        ntile = t // MOBA_BLOCK
        in_specs += [_resident((1, D_MODEL)), _resident((D_MODEL, 2 * kdim)),
                     _resident((1, HEAD_DIM)), _resident(rope_tab.shape)]
        args += [kvg.reshape(1, D_MODEL), wkv, invf, rope_tab]
        out_specs = [row_spec,
                     pl.BlockSpec((N_KV_HEADS, FFN_ROWS, 2 * HEAD_DIM), lambda i: (0, i, 0)),
                     pl.BlockSpec((sub, N_KV_HEADS, V_ROWS, MOBA_BLOCK), lambda i: (i, 0, 0, 0)),
                     pl.BlockSpec((sub, 1, kdim), lambda i: (i, 0, 0))]
        out_shape = [out_shape,
                     jax.ShapeDtypeStruct((N_KV_HEADS, t, 2 * HEAD_DIM), BF16),
                     jax.ShapeDtypeStruct((ntile, N_KV_HEADS, V_ROWS, MOBA_BLOCK), BF16),
                     jax.ShapeDtypeStruct((ntile, 1, kdim), F32)]
    return pl.pallas_call(
        functools.partial(_ffn_body, mode=mode, nblk=nblk),
        grid=(t // FFN_ROWS,),
        in_specs=in_specs,
        out_specs=out_specs,
        out_shape=out_shape,
        compiler_params=pltpu.CompilerParams(
            dimension_semantics=("arbitrary",), vmem_limit_bytes=VMEM_LIMIT),
        name="ffn_" + mode,
    )(*args)


def _s5_prep_body(lr_ref, li_ref, ls_ref, bre_ref, bim_ref, cim_ref,
                  ar_ref, ai_ref, amr_ref, ami_ref, pr_ref, pi_ref, btr_ref, bti_ref, cneg_ref):
    lr = lr_ref[...]
    li = li_ref[...]
    dt = jnp.exp(ls_ref[...])
    mag = jnp.exp(lr * dt)
    abar_re = mag * jnp.cos(li * dt)
    abar_im = mag * jnp.sin(li * dt)
    ar_ref[...] = abar_re
    ai_ref[...] = abar_im
    nr, ni = abar_re - 1.0, abar_im
    den = lr * lr + li * li
    coef_re = (nr * lr + ni * li) / den
    coef_im = (ni * lr - nr * li) / den
    k = (lax.broadcasted_iota(jnp.int32, (S5_STEPS, 1), 0) + 1).astype(F32)
    pmag = jnp.exp((lr * dt) * k)
    pang = (li * dt) * k
    pr = pmag * jnp.cos(pang)
    pi = pmag * jnp.sin(pang)
    pr_ref[...] = pr
    pi_ref[...] = pi
    amr_ref[...] = pr[S5_STEPS - 1:S5_STEPS, :]
    ami_ref[...] = pi[S5_STEPS - 1:S5_STEPS, :]
    bre = bre_ref[...]
    bim = bim_ref[...]
    btr_ref[...] = coef_re * bre - coef_im * bim
    bti_ref[...] = coef_re * bim + coef_im * bre
    cneg_ref[...] = -cim_ref[...]


def _s5_prep_call(a_re, a_im, log_step, b_re, b_im, c_im):
    n = S5_NSTATE
    row = lambda v: v.reshape(1, n)
    chan_major = lambda v: v.transpose(2, 0, 1).reshape(S5_GROUP, n)
    ls = jnp.repeat(log_step, S5_STATE)
    outs = pl.pallas_call(
        _s5_prep_body,
        out_shape=[jax.ShapeDtypeStruct((1, n), F32)] * 4
        + [jax.ShapeDtypeStruct((S5_STEPS, n), F32)] * 2
        + [jax.ShapeDtypeStruct((S5_GROUP, n), F32)] * 3,
        name="s5_prep",
    )(row(a_re), row(a_im), row(ls), chan_major(b_re), chan_major(b_im),
      c_im.transpose(1, 0, 2).reshape(S5_GROUP, n))
    return outs


def _s5_body(x_ref, xprev_ref, g_ref, perm_ref, permt_ref, bblk_ref, cblk_ref, ar_ref, ai_ref,
             amr_ref, ami_ref, pr_ref, pi_ref, d_ref, wglu_ref, o_ref,
             xs_ref, st_ref, c_ref, hb_ref, y_ref, *, nchunk):
    step = pl.program_id(0)

    @pl.when(step % nchunk == 0)
    def _():
        st_ref[...] = jnp.zeros(st_ref.shape, F32)

    @pl.when(step == 0)
    def _():
        y_ref[...] = jnp.zeros(y_ref.shape, F32)

    yp = y_ref[...]
    yp = 0.5 * yp * (1.0 + jnp.tanh(np.sqrt(2.0 / np.pi).astype(np.float32)
                                    * (yp + 0.044715 * (yp * yp * yp))))
    yn = _dot(permt_ref[...], yp.astype(BF16)).astype(BF16)

    def glu_piece(c):
        cols = slice(c * S5_GLU_COLS, (c + 1) * S5_GLU_COLS)
        gcols = slice(D_MODEL + c * S5_GLU_COLS, D_MODEL + (c + 1) * S5_GLU_COLS)
        val = _dot(yn, wglu_ref[:, cols])
        gate = _dot(yn, wglu_ref[:, gcols])
        o_ref[:, cols] = xprev_ref[:, cols] + val * jax.nn.sigmoid(gate)

    x = x_ref[...]
    u = _rms(x, g_ref[...])
    u_hi = u.astype(BF16)
    u_lo = (u - u_hi.astype(F32)).astype(BF16)
    perm = perm_ref[...]
    up_hi = _dot(perm, u_hi)
    up = up_hi + _dot(perm, u_lo)
    ub = up_hi.astype(BF16)

    last = SUBLANES * (S5_STEPS - 1)
    nb = S5_LANE_BLOCKS
    re, im = slice(0, S5_HALF), slice(S5_HALF, 2 * S5_HALF)

    def project_in(j):
        xs_ref[j] = _dot(ub[:, j * LANES:(j + 1) * LANES], bblk_ref[j])

    for j in range(nb):
        project_in(j)

    def scan_steps(j):
        a_r = jnp.broadcast_to(ar_ref[j], (SUBLANES, S5_HALF))
        a_i = jnp.broadcast_to(ai_ref[j], (SUBLANES, S5_HALF))
        state = [xs_ref[j, 0:SUBLANES, re], xs_ref[j, 0:SUBLANES, im]]

        def step(k):
            rows = slice(k * SUBLANES, (k + 1) * SUBLANES)
            h_r, h_i = state
            state[0] = a_r * h_r - a_i * h_i + xs_ref[j, rows, re]
            state[1] = a_r * h_i + a_i * h_r + xs_ref[j, rows, im]
            xs_ref[j, rows, re] = state[0]
            xs_ref[j, rows, im] = state[1]

        return [functools.partial(step, k) for k in range(1, S5_STEPS)]

    def entering_states(j):
        am_r, am_i = amr_ref[j], ami_ref[j]
        c_r, c_i = st_ref[j, :, re], st_ref[j, :, im]
        for i in range(S5_SUBSEQ):
            c_ref[j, i:i + 1, re] = c_r
            c_ref[j, i:i + 1, im] = c_i
            e_r = xs_ref[j, last + i:last + i + 1, re]
            e_i = xs_ref[j, last + i:last + i + 1, im]
            c_r, c_i = am_r * c_r - am_i * c_i + e_r, am_r * c_i + am_i * c_r + e_i
        st_ref[j, :, re] = c_r
        st_ref[j, :, im] = c_i

    def fix_steps(j):
        cc_r = jnp.concatenate([c_ref[j, :, re]] * 2, axis=0)
        cc_i = jnp.concatenate([c_ref[j, :, im]] * 2, axis=0)

        def step(k):
            rows = slice(2 * k * SUBLANES, 2 * (k + 1) * SUBLANES)
            p_r, p_i = pr_ref[j, rows, :], pi_ref[j, rows, :]
            t_r = xs_ref[j, rows, re] + (p_r * cc_r - p_i * cc_i)
            t_i = xs_ref[j, rows, im] + (p_r * cc_i + p_i * cc_r)
            hb_ref[j, rows, :] = jnp.concatenate([t_r, t_i], axis=1).astype(BF16)

        return [functools.partial(step, k) for k in range(S5_STEPS // 2)]

    ys = []
    for j in range(nb + 1):
        p1 = scan_steps(j) if j < nb else []
        p2 = fix_steps(j - 1) if j >= 1 else []
        for k in range(max(len(p2), (len(p1) + 1) // 2)):
            for f in p1[2 * k:2 * k + 2]:
                f()
            if k < len(p2):
                p2[k]()
        if j < nb:
            entering_states(j)
        if j >= 1:
            ys.append(_dot(hb_ref[j - 1], cblk_ref[j - 1]))
            if j % 2 == 1:
                glu_piece(j // 2)

    y_ref[...] = jnp.concatenate(ys, axis=1) + d_ref[...] * up


def _s5_call(x2, bsz, seq, g, prep, b_unused, c_re, d_skip, w_glu):
    del b_unused
    ar, ai, amr, ami, pr, pi, btr, bti, cneg = prep
    nb, gb, ns = S5_LANE_BLOCKS, S5_GROUPS_PER_BLOCK, S5_STATE
    eye = jnp.eye(gb, dtype=F32)

    def per_block(v, rows):
        return v.reshape(rows, nb, S5_HALF).transpose(1, 0, 2)

    bt = jnp.stack([btr, bti], axis=1).reshape(S5_GROUP, 2, nb, gb, ns)
    bblk = jnp.einsum('crjgp,gh->jgcrhp', bt, eye).reshape(nb, LANES, 2 * S5_HALF).astype(BF16)
    cmat = jnp.stack([c_re.transpose(1, 0, 2).reshape(S5_GROUP, S5_NSTATE), cneg], axis=1)
    cmat = cmat.reshape(S5_GROUP, 2, nb, gb, ns)
    cblk = jnp.einsum('crjgp,gh->jrgphc', cmat, eye).reshape(nb, 2 * S5_HALF, LANES).astype(BF16)

    r = np.arange(S5_CHUNK)
    perm_np = np.zeros((S5_CHUNK, S5_CHUNK), np.float32)
    perm_np[r, (r % SUBLANES) * S5_STEPS + r // SUBLANES] = 1.0
    perm = jnp.asarray(perm_np, BF16)
    permt = jnp.asarray(perm_np.T, BF16)

    nchunk = seq // S5_CHUNK
    total = bsz * nchunk
    cur_spec = pl.BlockSpec((S5_CHUNK, D_MODEL), lambda s: (jnp.minimum(s, total - 1), 0))
    prev_spec = pl.BlockSpec((S5_CHUNK, D_MODEL), lambda s: (jnp.maximum(s - 1, 0), 0))
    in_specs = [
        cur_spec, prev_spec, _resident((1, D_MODEL)),
        _resident((S5_CHUNK, S5_CHUNK)), _resident((S5_CHUNK, S5_CHUNK)),
        _resident((nb, LANES, 2 * S5_HALF)), _resident((nb, 2 * S5_HALF, LANES)),
        _resident((nb, 1, S5_HALF)), _resident((nb, 1, S5_HALF)),
        _resident((nb, 1, S5_HALF)), _resident((nb, 1, S5_HALF)),
        _resident((nb, S5_CHUNK, S5_HALF)), _resident((nb, S5_CHUNK, S5_HALF)),
        _resident((1, D_MODEL)), _resident((D_MODEL, 2 * D_MODEL)),
    ]
    return pl.pallas_call(
        functools.partial(_s5_body, nchunk=nchunk),
        grid=(total + 1,),
        in_specs=in_specs,
        out_specs=prev_spec,
        out_shape=jax.ShapeDtypeStruct(x2.shape, F32),
        scratch_shapes=[
            pltpu.VMEM((nb, S5_CHUNK, 2 * S5_HALF), F32),
            pltpu.VMEM((nb, 1, 2 * S5_HALF), F32),
            pltpu.VMEM((nb, S5_SUBSEQ, 2 * S5_HALF), F32),
            pltpu.VMEM((nb, S5_CHUNK, 2 * S5_HALF), BF16),
            pltpu.VMEM((S5_CHUNK, D_MODEL), F32),
        ],
        compiler_params=pltpu.CompilerParams(
            dimension_semantics=("arbitrary",), vmem_limit_bytes=VMEM_LIMIT),
        name="s5",
    )(x2, x2, g.reshape(1, D_MODEL), perm, permt, bblk, cblk,
      per_block(ar, 1), per_block(ai, 1), per_block(amr, 1), per_block(ami, 1),
      per_block(jnp.repeat(pr, SUBLANES, axis=0), S5_CHUNK),
      per_block(jnp.repeat(pi, SUBLANES, axis=0), S5_CHUNK),
      d_skip.reshape(1, D_MODEL), w_glu.astype(BF16))


def _rope_tables(pos0, invf, tab_ref):
    ang0 = pos0.astype(F32) * invf
    c0, s0 = jnp.cos(ang0), jnp.sin(ang0)
    cos_t = c0 * tab_ref[0] - s0 * tab_ref[1]
    sin_t = s0 * tab_ref[2] + c0 * tab_ref[3]
    return cos_t, sin_t


def _rope_head(xh, cos_t, sin_t, low_half):
    half = ROPE_DIM // 2
    swapped = jnp.where(low_half, pltpu.roll(xh, LANES - half, axis=1), pltpu.roll(xh, half, axis=1))
    return xh * cos_t + swapped * sin_t


def _rope_consts():
    half = ROPE_DIM // 2
    inv_freq = ROPE_THETA ** (-jnp.arange(0, ROPE_DIM, 2, dtype=F32) / ROPE_DIM)
    pad = jnp.zeros((HEAD_DIM - ROPE_DIM,), F32)
    invf = jnp.concatenate([inv_freq, inv_freq, pad]).reshape(1, HEAD_DIM)
    sign = jnp.concatenate([-jnp.ones((half,), F32), jnp.ones((half,), F32), pad]).reshape(1, HEAD_DIM)
    ang_r = jnp.arange(MOBA_BLOCK, dtype=F32)[:, None] * invf
    cos_r, sin_r = jnp.cos(ang_r), jnp.sin(ang_r)
    tables = jnp.stack([cos_r, sin_r, cos_r * sign, sin_r * sign])
    return invf, tables


def _kv_tile(x, blk, g, wkv_ref, invf, tab_ref, ka_ref, vt_ref, km_ref, sub):
    rows = slice(sub * MOBA_BLOCK, (sub + 1) * MOBA_BLOCK)
    h = _rms(x, g).astype(BF16)
    kv = _dot(h, wkv_ref[...])
    kdim = N_KV_HEADS * HEAD_DIM
    cos_t, sin_t = _rope_tables(blk * MOBA_BLOCK, invf, tab_ref)
    lane = lax.broadcasted_iota(jnp.int32, (MOBA_BLOCK, HEAD_DIM), 1)
    low_half = lane < ROPE_DIM // 2
    onehot = jnp.where(lane == blk, 1.0, 0.0).astype(BF16)
    pad_row = lax.broadcasted_iota(jnp.int32, (V_ROWS - HEAD_DIM, MOBA_BLOCK), 0)
    ones_rows = jnp.where(pad_row == 0, 1.0, 0.0).astype(BF16)
    means = []
    for hh in range(N_KV_HEADS):
        kh = _rope_head(kv[:, hh * HEAD_DIM:(hh + 1) * HEAD_DIM], cos_t, sin_t, low_half)
        means.append(jnp.mean(kh, axis=0, keepdims=True))
        ka_ref[hh, rows, :] = jnp.concatenate([kh.astype(BF16), onehot], axis=1)
        vt = kv[:, kdim + hh * HEAD_DIM:kdim + (hh + 1) * HEAD_DIM].T.astype(BF16)
        vt_ref[sub, hh] = jnp.concatenate([vt, ones_rows], axis=0)
    km_ref[sub] = jnp.concatenate(means, axis=1)


def _split_bf16(v):
    hi = v.astype(BF16)
    return hi, (v - hi.astype(F32)).astype(BF16)


def _attn_body(x_ref, g_ref, wq_ref, wo_ref, ka_ref, vt_ref, km_ref, invf_ref, o_ref,
               qa_ref, acc_ref, s0_ref, s1_ref, *, nblk):
    own = pl.program_id(1)
    items = KV_GROUP * MOBA_BLOCK
    group_keys = ATT_GROUP * MOBA_BLOCK
    half = ROPE_DIM // 2
    x = x_ref[0]
    h = _rms(x, g_ref[...]).astype(BF16)
    q = _dot(h, wq_ref[...])
    pos = (own * MOBA_BLOCK + lax.broadcasted_iota(jnp.int32, (1, MOBA_BLOCK), 1)).astype(F32)
    ang = invf_ref[...] * pos
    cos_t, sin_t = jnp.cos(ang), jnp.sin(ang)

    def head_t(i):
        qt = q[:, i * HEAD_DIM:(i + 1) * HEAD_DIM].T
        x1, x2 = qt[0:half], qt[half:2 * half]
        rot = jnp.concatenate([x1 * cos_t - x2 * sin_t, x2 * cos_t + x1 * sin_t, qt[2 * half:]], axis=0)
        return rot * (HEAD_DIM ** -0.5)

    blk = lax.broadcasted_iota(jnp.int32, (nblk, items), 0)
    blk_f = blk.astype(F32)
    past = blk < own
    causal = (lax.broadcasted_iota(jnp.int32, (MOBA_BLOCK, items), 0)
              <= lax.broadcasted_iota(jnp.int32, (MOBA_BLOCK, items), 1) % MOBA_BLOCK)
    feat_pad = jnp.zeros((HEAD_DIM - nblk, items), BF16)

    outs = []
    for kh in range(N_KV_HEADS):
        qt = jnp.concatenate([head_t(kh * KV_GROUP + i) for i in range(KV_GROUP)], axis=1)
        q_hi, q_lo = _split_bf16(qt)
        k_hi, k_lo = _split_bf16(km_ref[0, kh])
        g_hi = _dot(jnp.concatenate([k_hi, k_lo], axis=0), q_hi)
        gate = g_hi[0:nblk] + (_dot(k_hi, q_lo) + g_hi[nblk:2 * nblk])
        cur = jnp.where(past, gate, -jnp.inf)
        bias = jnp.full((nblk, items), NEG_INF, F32)
        for _ in range(MOBA_TOPK):
            best = jnp.max(cur, axis=0, keepdims=True)
            cand = jnp.where((cur == best) & (best > -jnp.inf), blk_f, float(nblk))
            pick = blk_f == jnp.min(cand, axis=0, keepdims=True)
            bias = jnp.where(pick, 0.0, bias)
            cur = jnp.where(pick, -jnp.inf, cur)
        q_feat = (qt * LOG2E).astype(BF16)
        qa_ref[...] = jnp.concatenate([q_feat, bias.astype(BF16), feat_pad], axis=0)

        own_keys = pl.ds(pl.multiple_of(own * MOBA_BLOCK, MOBA_BLOCK), MOBA_BLOCK)
        s = _dot(ka_ref[kh, 0, own_keys, 0:HEAD_DIM], q_feat)
        s = jnp.where(causal, s, NEG_INF)
        m = jnp.max(s, axis=0, keepdims=True)
        acc_ref[...] = _dot(vt_ref[0, own, kh], jnp.exp2((s - m).astype(BF16)))

        def score_group(gi, kh=kh):
            keys = pl.ds(pl.multiple_of(gi * group_keys, group_keys), group_keys)
            return _dot(ka_ref[kh, 0, keys, :], qa_ref[...])

        def consume(s_cur_ref, gi, m_prev, kh=kh):
            s = s_cur_ref[...]
            m_new = jnp.maximum(m_prev, jnp.max(s, axis=0, keepdims=True))
            alpha = jnp.exp2(m_prev - m_new)
            pb = jnp.exp2((s - m_new).astype(BF16))
            n0 = gi * ATT_GROUP
            pv = _dot(vt_ref[0, n0, kh], pb[0:MOBA_BLOCK])
            for j in range(1, ATT_GROUP):
                pv = pv + _dot(vt_ref[0, n0 + j, kh], pb[j * MOBA_BLOCK:(j + 1) * MOBA_BLOCK])
            acc_ref[...] = alpha * acc_ref[...] + pv
            return m_new

        n_groups = (own + ATT_GROUP - 1) // ATT_GROUP
        last_group = nblk // ATT_GROUP - 1
        s0_ref[...] = score_group(0)

        def group_pair(pi, m_prev):
            g = 2 * pi
            s1_ref[...] = score_group(g + 1)
            m_mid = consume(s0_ref, g, m_prev)
            s0_ref[...] = score_group(jnp.minimum(g + 2, last_group))
            return consume(s1_ref, g + 1, m_mid)

        m = lax.fori_loop(0, n_groups // 2, group_pair, m)

        @pl.when(n_groups % 2 == 1)
        def _(m=m):
            consume(s0_ref, n_groups - 1, m)

        ot = acc_ref[0:HEAD_DIM, :] / acc_ref[HEAD_DIM:HEAD_DIM + 1, :]
        outs.extend(ot[:, i * MOBA_BLOCK:(i + 1) * MOBA_BLOCK].T for i in range(KV_GROUP))

    attn = jnp.concatenate(outs, axis=1).astype(BF16)
    o_ref[0] = x + _dot(attn, wo_ref[...])


def _attn_call(x3, g, w_q, w_o, ka, vt, km, invf_col):
    bsz, seq, _ = x3.shape
    nblk = seq // MOBA_BLOCK
    items = KV_GROUP * MOBA_BLOCK
    x_spec = pl.BlockSpec((1, MOBA_BLOCK, D_MODEL), lambda b, i: (b, i, 0))
    return pl.pallas_call(
        functools.partial(_attn_body, nblk=nblk),
        grid=(bsz, nblk),
        in_specs=[x_spec, _resident((1, D_MODEL)),
                  _resident((D_MODEL, D_MODEL)), _resident((D_MODEL, D_MODEL)),
                  pl.BlockSpec((N_KV_HEADS, 1, seq, 2 * HEAD_DIM), lambda b, i: (0, b, 0, 0),
                               pipeline_mode=pl.Buffered(1)),
                  pl.BlockSpec((1, nblk, N_KV_HEADS, V_ROWS, MOBA_BLOCK),
                               lambda b, i: (b, 0, 0, 0, 0), pipeline_mode=pl.Buffered(1)),
                  pl.BlockSpec((1, N_KV_HEADS, nblk, HEAD_DIM), lambda b, i: (b, 0, 0, 0)),
                  _resident((ROPE_DIM // 2, 1))],
        out_specs=x_spec,
        out_shape=jax.ShapeDtypeStruct(x3.shape, F32),
        scratch_shapes=[pltpu.VMEM((2 * HEAD_DIM, items), BF16),
                        pltpu.VMEM((V_ROWS, items), F32),
                        pltpu.VMEM((ATT_GROUP * MOBA_BLOCK, items), F32),
                        pltpu.VMEM((ATT_GROUP * MOBA_BLOCK, items), F32)],
        compiler_params=pltpu.CompilerParams(
            dimension_semantics=("arbitrary", "arbitrary"), vmem_limit_bytes=VMEM_LIMIT),
        name="moba_attn",
    )(x3, g.reshape(1, D_MODEL), w_q.astype(BF16), w_o.astype(BF16), ka, vt, km, invf_col)


def kernel(x, norm_g, ffn_w_in, ffn_w_out, s5_a_re, s5_a_im, s5_log_step, s5_b_re, s5_b_im,
           s5_c_re, s5_c_im, s5_d, s5_w_glu, kv_norm_g, w_k, w_v, w_q, w_o, final_g):
    bsz, seq, _ = x.shape
    assert seq % S5_CHUNK == 0 and seq % MOBA_BLOCK == 0
    assert (seq // MOBA_BLOCK) % (2 * SUBLANES) == 0 and seq // MOBA_BLOCK <= HEAD_DIM
    assert (seq // MOBA_BLOCK) % ATT_GROUP == 0
    assert (bsz * seq) % FFN_ROWS == 0 and FFN_ROWS % MOBA_BLOCK == 0
    nblk = seq // MOBA_BLOCK
    x2 = x.reshape(bsz * seq, D_MODEL)
    invf, rope_tab = _rope_consts()

    w_in_b = ffn_w_in
    w_out_b = ffn_w_out
    wkv = jnp.concatenate([w_k, w_v], axis=1).astype(BF16)

    x2 = _ffn_call(x2, norm_g[0, 0], w_in_b, w_out_b, 0, 0)
    prep = _s5_prep_call(s5_a_re[0], s5_a_im[0], s5_log_step[0], s5_b_re[0], s5_b_im[0], s5_c_im[0])
    x2 = _s5_call(x2, bsz, seq, norm_g[0, 1], prep, None, s5_c_re[0], s5_d[0], s5_w_glu[0])
    x2, ka, vt, km = _ffn_call(x2, norm_g[0, 2], w_in_b, w_out_b, 0, 1,
                               kv=(kv_norm_g, wkv, invf, rope_tab), nblk=nblk)
    ka = ka.reshape(N_KV_HEADS, bsz, seq, 2 * HEAD_DIM)
    vt = vt.reshape(bsz, nblk, N_KV_HEADS, V_ROWS, MOBA_BLOCK)
    km = km.reshape(bsz, nblk, N_KV_HEADS, HEAD_DIM).transpose(0, 2, 1, 3)

    x2 = _ffn_call(x2, norm_g[1, 0], w_in_b, w_out_b, 1, 0)
    x3 = _attn_call(x2.reshape(bsz, seq, D_MODEL), norm_g[1, 1], w_q[0], w_o[0], ka, vt, km,
                    invf[0, 0:ROPE_DIM // 2].reshape(ROPE_DIM // 2, 1))
    x2 = _ffn_call(x3.reshape(bsz * seq, D_MODEL), norm_g[1, 2], w_in_b, w_out_b, 1, 1,
                   final_g=final_g)
    return x2.reshape(bsz, seq, D_MODEL)
```

```python
import functools

import jax
import jax.numpy as jnp
import numpy as np
from jax import lax
from jax.experimental import pallas as pl
from jax.experimental.pallas import tpu as pltpu

F32 = jnp.float32
BF16 = jnp.bfloat16

D_MODEL = 1024
D_FF = 2816
RMS_EPS = 1e-6
S5_GROUP = 16
S5_GROUPS = D_MODEL // S5_GROUP
S5_STATE = 64
N_HEADS = 8
HEAD_DIM = 128
N_KV_HEADS = 2
KV_GROUP = N_HEADS // N_KV_HEADS
ROPE_DIM = HEAD_DIM // 4
ROPE_THETA = 500000.0
MOBA_BLOCK = 256
MOBA_TOPK = 3
NEG_INF = -1e30
SCORE_FLOOR = 0.5 * NEG_INF
LOG2E = 1.4426950408889634

LANES = 128
SUBLANES = 8
VMEM_LIMIT = 56 * 1024 * 1024

FFN_ROWS = 512
FFN_COLS = 256

ATT_GROUP = 4
V_ROWS = HEAD_DIM + 2 * SUBLANES

S5_CHUNK = 256
S5_SUBSEQ = SUBLANES
S5_STEPS = S5_CHUNK // S5_SUBSEQ
S5_LANE_BLOCKS = D_MODEL // LANES
S5_GROUPS_PER_BLOCK = LANES // S5_GROUP
S5_HALF = S5_GROUPS_PER_BLOCK * S5_STATE
S5_NSTATE = S5_GROUPS * S5_STATE
S5_GLU_COLS = 2 * D_MODEL // S5_LANE_BLOCKS


def _rms(x, g):
    ms = jnp.mean(x * x, axis=-1, keepdims=True)
    return (x * lax.rsqrt(ms + RMS_EPS)) * g


def _dot(a, b):
    return jnp.dot(a, b, preferred_element_type=F32)


def _resident(shape):
    nd = len(shape)
    return pl.BlockSpec(shape, lambda *_: (0,) * nd, pipeline_mode=pl.Buffered(1))


def _ffn_body(*refs, mode, nblk):
    x_ref, g_ref, win_ref, wout_ref = refs[:4]
    x = x_ref[...]
    h = _rms(x, g_ref[...]).astype(BF16)
    acc = jnp.zeros(x.shape, F32)
    for c in range(D_FF // FFN_COLS):
        lo = c * FFN_COLS
        gate = _dot(h, win_ref[:, lo:lo + FFN_COLS].astype(BF16))
        up = _dot(h, win_ref[:, D_FF + lo:D_FF + lo + FFN_COLS].astype(BF16))
        act = (gate * jax.nn.sigmoid(gate)) * up
        acc = acc + _dot(act.astype(BF16), wout_ref[lo:lo + FFN_COLS, :].astype(BF16))
    y = x + 0.5 * acc
    if mode == 'final':
        fg_ref, o_ref = refs[4:]
        y = _rms(y, fg_ref[...])
    elif mode == 'kv':
        kvg_ref, wkv_ref, invf_ref, tab_ref, o_ref, ka_ref, vt_ref, km_ref = refs[4:]
        for sub in range(FFN_ROWS // MOBA_BLOCK):
            rows = slice(sub * MOBA_BLOCK, (sub + 1) * MOBA_BLOCK)
            blk = (pl.program_id(0) * (FFN_ROWS // MOBA_BLOCK) + sub) % nblk
            _kv_tile(y[rows], blk, kvg_ref[...], wkv_ref, invf_ref[...], tab_ref,
                     ka_ref, vt_ref, km_ref, sub)
    else:
        o_ref, = refs[4:]
    o_ref[...] = y


def _ffn_call(x2, g, w_in_all, w_out_all, layer, idx, final_g=None, kv=None, nblk=None):
    t = x2.shape[0]
    mode = 'final' if final_g is not None else ('kv' if kv is not None else 'plain')
    row_spec = pl.BlockSpec((FFN_ROWS, D_MODEL), lambda i: (i, 0))
    pick = lambda i: (layer, idx, 0, 0)
    in_specs = [row_spec, _resident((1, D_MODEL)),
                pl.BlockSpec((None, None, D_MODEL, 2 * D_FF), pick, pipeline_mode=pl.Buffered(1)),
                pl.BlockSpec((None, None, D_FF, D_MODEL), pick, pipeline_mode=pl.Buffered(1))]
    args = [x2, g.reshape(1, D_MODEL), w_in_all, w_out_all]
    out_specs = row_spec
    out_shape = jax.ShapeDtypeStruct((t, D_MODEL), F32)
    if mode == 'final':
        in_specs.append(_resident((1, D_MODEL)))
        args.append(final_g.reshape(1, D_MODEL))
    elif mode == 'kv':
        kvg, wkv, invf, rope_tab = kv
        kdim = N_KV_HEADS * HEAD_DIM
        sub = FFN_ROWS // MOBA_BLOCK
        ntile = t // MOBA_BLOCK
        in_specs += [_resident((1, D_MODEL)), _resident((D_MODEL, 2 * kdim)),
                     _resident((1, HEAD_DIM)), _resident(rope_tab.shape)]
        args += [kvg.reshape(1, D_MODEL), wkv, invf, rope_tab]
        out_specs = [row_spec,
                     pl.BlockSpec((N_KV_HEADS, FFN_ROWS, 2 * HEAD_DIM), lambda i: (0, i, 0)),
                     pl.BlockSpec((sub, N_KV_HEADS, V_ROWS, MOBA_BLOCK), lambda i: (i, 0, 0, 0)),
                     pl.BlockSpec((sub, 1, kdim), lambda i: (i, 0, 0))]
        out_shape = [out_shape,
                     jax.ShapeDtypeStruct((N_KV_HEADS, t, 2 * HEAD_DIM), BF16),
                     jax.ShapeDtypeStruct((ntile, N_KV_HEADS, V_ROWS, MOBA_BLOCK), BF16),
                     jax.ShapeDtypeStruct((ntile, 1, kdim), F32)]
    return pl.pallas_call(
        functools.partial(_ffn_body, mode=mode, nblk=nblk),
        grid=(t // FFN_ROWS,),
        in_specs=in_specs,
        out_specs=out_specs,
        out_shape=out_shape,
        compiler_params=pltpu.CompilerParams(
            dimension_semantics=("arbitrary",), vmem_limit_bytes=VMEM_LIMIT),
        name="ffn_" + mode,
    )(*args)


def _s5_prep_body(lr_ref, li_ref, ls_ref, bre_ref, bim_ref, cim_ref,
                  ar_ref, ai_ref, amr_ref, ami_ref, pr_ref, pi_ref, btr_ref, bti_ref, cneg_ref):
    lr = lr_ref[...]
    li = li_ref[...]
    dt = jnp.exp(ls_ref[...])
    mag = jnp.exp(lr * dt)
    abar_re = mag * jnp.cos(li * dt)
    abar_im = mag * jnp.sin(li * dt)
    ar_ref[...] = abar_re
    ai_ref[...] = abar_im
    nr, ni = abar_re - 1.0, abar_im
    den = lr * lr + li * li
    coef_re = (nr * lr + ni * li) / den
    coef_im = (ni * lr - nr * li) / den
    k = (lax.broadcasted_iota(jnp.int32, (S5_STEPS, 1), 0) + 1).astype(F32)
    pmag = jnp.exp((lr * dt) * k)
    pang = (li * dt) * k
    pr = pmag * jnp.cos(pang)
    pi = pmag * jnp.sin(pang)
    pr_ref[...] = pr
    pi_ref[...] = pi
    amr_ref[...] = pr[S5_STEPS - 1:S5_STEPS, :]
    ami_ref[...] = pi[S5_STEPS - 1:S5_STEPS, :]
    bre = bre_ref[...]
    bim = bim_ref[...]
    btr_ref[...] = coef_re * bre - coef_im * bim
    bti_ref[...] = coef_re * bim + coef_im * bre
    cneg_ref[...] = -cim_ref[...]


def _s5_prep_call(a_re, a_im, log_step, b_re, b_im, c_im):
    n = S5_NSTATE
    row = lambda v: v.reshape(1, n)
    chan_major = lambda v: v.transpose(2, 0, 1).reshape(S5_GROUP, n)
    ls = jnp.repeat(log_step, S5_STATE)
    outs = pl.pallas_call(
        _s5_prep_body,
        out_shape=[jax.ShapeDtypeStruct((1, n), F32)] * 4
        + [jax.ShapeDtypeStruct((S5_STEPS, n), F32)] * 2
        + [jax.ShapeDtypeStruct((S5_GROUP, n), F32)] * 3,
        name="s5_prep",
    )(row(a_re), row(a_im), row(ls), chan_major(b_re), chan_major(b_im),
      c_im.transpose(1, 0, 2).reshape(S5_GROUP, n))
    return outs


def _s5_body(x_ref, xprev_ref, g_ref, perm_ref, permt_ref, bblk_ref, cblk_ref, ar_ref, ai_ref,
             amr_ref, ami_ref, pr_ref, pi_ref, d_ref, wglu_ref, o_ref,
             xs_ref, st_ref, c_ref, hb_ref, y_ref, *, nchunk):
    step = pl.program_id(0)

    @pl.when(step % nchunk == 0)
    def _():
        st_ref[...] = jnp.zeros(st_ref.shape, F32)

    @pl.when(step == 0)
    def _():
        y_ref[...] = jnp.zeros(y_ref.shape, F32)

    yp = y_ref[...]
    yp = 0.5 * yp * (1.0 + jnp.tanh(np.sqrt(2.0 / np.pi).astype(np.float32)
                                    * (yp + 0.044715 * (yp * yp * yp))))
    yn = _dot(permt_ref[...], yp.astype(BF16)).astype(BF16)

    def glu_piece(c):
        cols = slice(c * S5_GLU_COLS, (c + 1) * S5_GLU_COLS)
        gcols = slice(D_MODEL + c * S5_GLU_COLS, D_MODEL + (c + 1) * S5_GLU_COLS)
        val = _dot(yn, wglu_ref[:, cols])
        gate = _dot(yn, wglu_ref[:, gcols])
        o_ref[:, cols] = xprev_ref[:, cols] + val * jax.nn.sigmoid(gate)

    x = x_ref[...]
    u = _rms(x, g_ref[...])
    u_hi = u.astype(BF16)
    u_lo = (u - u_hi.astype(F32)).astype(BF16)
    perm = perm_ref[...]
    up_hi = _dot(perm, u_hi)
    up = up_hi + _dot(perm, u_lo)
    ub = up_hi.astype(BF16)

    last = SUBLANES * (S5_STEPS - 1)
    nb = S5_LANE_BLOCKS
    re, im = slice(0, S5_HALF), slice(S5_HALF, 2 * S5_HALF)

    def project_in(j):
        xs_ref[j] = _dot(ub[:, j * LANES:(j + 1) * LANES], bblk_ref[j])

    for j in range(nb):
        project_in(j)

    def scan_steps(j):
        a_r = jnp.broadcast_to(ar_ref[j], (SUBLANES, S5_HALF))
        a_i = jnp.broadcast_to(ai_ref[j], (SUBLANES, S5_HALF))
        state = [xs_ref[j, 0:SUBLANES, re], xs_ref[j, 0:SUBLANES, im]]

        def step(k):
            rows = slice(k * SUBLANES, (k + 1) * SUBLANES)
            h_r, h_i = state
            state[0] = a_r * h_r - a_i * h_i + xs_ref[j, rows, re]
            state[1] = a_r * h_i + a_i * h_r + xs_ref[j, rows, im]
            xs_ref[j, rows, re] = state[0]
            xs_ref[j, rows, im] = state[1]

        return [functools.partial(step, k) for k in range(1, S5_STEPS)]

    def entering_states(j):
        am_r, am_i = amr_ref[j], ami_ref[j]
        c_r, c_i = st_ref[j, :, re], st_ref[j, :, im]
        for i in range(S5_SUBSEQ):
            c_ref[j, i:i + 1, re] = c_r
            c_ref[j, i:i + 1, im] = c_i
            e_r = xs_ref[j, last + i:last + i + 1, re]
            e_i = xs_ref[j, last + i:last + i + 1, im]
            c_r, c_i = am_r * c_r - am_i * c_i + e_r, am_r * c_i + am_i * c_r + e_i
        st_ref[j, :, re] = c_r
        st_ref[j, :, im] = c_i

    def fix_steps(j):
        cc_r = jnp.concatenate([c_ref[j, :, re]] * 2, axis=0)
        cc_i = jnp.concatenate([c_ref[j, :, im]] * 2, axis=0)

        def step(k):
            rows = slice(2 * k * SUBLANES, 2 * (k + 1) * SUBLANES)
            p_r, p_i = pr_ref[j, rows, :], pi_ref[j, rows, :]
            t_r = xs_ref[j, rows, re] + (p_r * cc_r - p_i * cc_i)
            t_i = xs_ref[j, rows, im] + (p_r * cc_i + p_i * cc_r)
            hb_ref[j, rows, :] = jnp.concatenate([t_r, t_i], axis=1).astype(BF16)

        return [functools.partial(step, k) for k in range(S5_STEPS // 2)]

    ys = []
    for j in range(nb + 1):
        p1 = scan_steps(j) if j < nb else []
        p2 = fix_steps(j - 1) if j >= 1 else []
        for k in range(max(len(p2), (len(p1) + 1) // 2)):
            for f in p1[2 * k:2 * k + 2]:
                f()
            if k < len(p2):
                p2[k]()
        if j < nb:
            entering_states(j)
        if j >= 1:
            ys.append(_dot(hb_ref[j - 1], cblk_ref[j - 1]))
            if j % 2 == 1:
                glu_piece(j // 2)

    y_ref[...] = jnp.concatenate(ys, axis=1) + d_ref[...] * up


def _s5_call(x2, bsz, seq, g, prep, b_unused, c_re, d_skip, w_glu):
    del b_unused
    ar, ai, amr, ami, pr, pi, btr, bti, cneg = prep
    nb, gb, ns = S5_LANE_BLOCKS, S5_GROUPS_PER_BLOCK, S5_STATE
    eye = jnp.eye(gb, dtype=F32)

    def per_block(v, rows):
        return v.reshape(rows, nb, S5_HALF).transpose(1, 0, 2)

    bt = jnp.stack([btr, bti], axis=1).reshape(S5_GROUP, 2, nb, gb, ns)
    bblk = jnp.einsum('crjgp,gh->jgcrhp', bt, eye).reshape(nb, LANES, 2 * S5_HALF).astype(BF16)
    cmat = jnp.stack([c_re.transpose(1, 0, 2).reshape(S5_GROUP, S5_NSTATE), cneg], axis=1)
    cmat = cmat.reshape(S5_GROUP, 2, nb, gb, ns)
    cblk = jnp.einsum('crjgp,gh->jrgphc', cmat, eye).reshape(nb, 2 * S5_HALF, LANES).astype(BF16)

    r = np.arange(S5_CHUNK)
    perm_np = np.zeros((S5_CHUNK, S5_CHUNK), np.float32)
    perm_np[r, (r % SUBLANES) * S5_STEPS + r // SUBLANES] = 1.0
    perm = jnp.asarray(perm_np, BF16)
    permt = jnp.asarray(perm_np.T, BF16)

    nchunk = seq // S5_CHUNK
    total = bsz * nchunk
    cur_spec = pl.BlockSpec((S5_CHUNK, D_MODEL), lambda s: (jnp.minimum(s, total - 1), 0))
    prev_spec = pl.BlockSpec((S5_CHUNK, D_MODEL), lambda s: (jnp.maximum(s - 1, 0), 0))
    in_specs = [
        cur_spec, prev_spec, _resident((1, D_MODEL)),
        _resident((S5_CHUNK, S5_CHUNK)), _resident((S5_CHUNK, S5_CHUNK)),
        _resident((nb, LANES, 2 * S5_HALF)), _resident((nb, 2 * S5_HALF, LANES)),
        _resident((nb, 1, S5_HALF)), _resident((nb, 1, S5_HALF)),
        _resident((nb, 1, S5_HALF)), _resident((nb, 1, S5_HALF)),
        _resident((nb, S5_CHUNK, S5_HALF)), _resident((nb, S5_CHUNK, S5_HALF)),
        _resident((1, D_MODEL)), _resident((D_MODEL, 2 * D_MODEL)),
    ]
    return pl.pallas_call(
        functools.partial(_s5_body, nchunk=nchunk),
        grid=(total + 1,),
        in_specs=in_specs,
        out_specs=prev_spec,
        out_shape=jax.ShapeDtypeStruct(x2.shape, F32),
        scratch_shapes=[
            pltpu.VMEM((nb, S5_CHUNK, 2 * S5_HALF), F32),
            pltpu.VMEM((nb, 1, 2 * S5_HALF), F32),
            pltpu.VMEM((nb, S5_SUBSEQ, 2 * S5_HALF), F32),
            pltpu.VMEM((nb, S5_CHUNK, 2 * S5_HALF), BF16),
            pltpu.VMEM((S5_CHUNK, D_MODEL), F32),
        ],
        compiler_params=pltpu.CompilerParams(
            dimension_semantics=("arbitrary",), vmem_limit_bytes=VMEM_LIMIT),
        name="s5",
    )(x2, x2, g.reshape(1, D_MODEL), perm, permt, bblk, cblk,
      per_block(ar, 1), per_block(ai, 1), per_block(amr, 1), per_block(ami, 1),
      per_block(jnp.repeat(pr, SUBLANES, axis=0), S5_CHUNK),
      per_block(jnp.repeat(pi, SUBLANES, axis=0), S5_CHUNK),
      d_skip.reshape(1, D_MODEL), w_glu.astype(BF16))


def _rope_tables(pos0, invf, tab_ref):
    ang0 = pos0.astype(F32) * invf
    c0, s0 = jnp.cos(ang0), jnp.sin(ang0)
    cos_t = c0 * tab_ref[0] - s0 * tab_ref[1]
    sin_t = s0 * tab_ref[2] + c0 * tab_ref[3]
    return cos_t, sin_t


def _rope_head(xh, cos_t, sin_t, low_half):
    half = ROPE_DIM // 2
    swapped = jnp.where(low_half, pltpu.roll(xh, LANES - half, axis=1), pltpu.roll(xh, half, axis=1))
    return xh * cos_t + swapped * sin_t


def _rope_consts():
    half = ROPE_DIM // 2
    inv_freq = ROPE_THETA ** (-jnp.arange(0, ROPE_DIM, 2, dtype=F32) / ROPE_DIM)
    pad = jnp.zeros((HEAD_DIM - ROPE_DIM,), F32)
    invf = jnp.concatenate([inv_freq, inv_freq, pad]).reshape(1, HEAD_DIM)
    sign = jnp.concatenate([-jnp.ones((half,), F32), jnp.ones((half,), F32), pad]).reshape(1, HEAD_DIM)
    ang_r = jnp.arange(MOBA_BLOCK, dtype=F32)[:, None] * invf
    cos_r, sin_r = jnp.cos(ang_r), jnp.sin(ang_r)
    tables = jnp.stack([cos_r, sin_r, cos_r * sign, sin_r * sign])
    return invf, tables


def _kv_tile(x, blk, g, wkv_ref, invf, tab_ref, ka_ref, vt_ref, km_ref, sub):
    rows = slice(sub * MOBA_BLOCK, (sub + 1) * MOBA_BLOCK)
    h = _rms(x, g).astype(BF16)
    kv = _dot(h, wkv_ref[...])
    kdim = N_KV_HEADS * HEAD_DIM
    cos_t, sin_t = _rope_tables(blk * MOBA_BLOCK, invf, tab_ref)
    lane = lax.broadcasted_iota(jnp.int32, (MOBA_BLOCK, HEAD_DIM), 1)
    low_half = lane < ROPE_DIM // 2
    onehot = jnp.where(lane == blk, 1.0, 0.0).astype(BF16)
    pad_row = lax.broadcasted_iota(jnp.int32, (V_ROWS - HEAD_DIM, MOBA_BLOCK), 0)
    ones_rows = jnp.where(pad_row == 0, 1.0, 0.0).astype(BF16)
    means = []
    for hh in range(N_KV_HEADS):
        kh = _rope_head(kv[:, hh * HEAD_DIM:(hh + 1) * HEAD_DIM], cos_t, sin_t, low_half)
        means.append(jnp.mean(kh, axis=0, keepdims=True))
        ka_ref[hh, rows, :] = jnp.concatenate([kh.astype(BF16), onehot], axis=1)
        vt = kv[:, kdim + hh * HEAD_DIM:kdim + (hh + 1) * HEAD_DIM].T.astype(BF16)
        vt_ref[sub, hh] = jnp.concatenate([vt, ones_rows], axis=0)
    km_ref[sub] = jnp.concatenate(means, axis=1)


def _split_bf16(v):
    hi = v.astype(BF16)
    return hi, (v - hi.astype(F32)).astype(BF16)


def _attn_body(x_ref, g_ref, wq_ref, wo_ref, ka_ref, vt_ref, km_ref, invf_ref, o_ref,
               qa_ref, acc_ref, s0_ref, s1_ref, *, nblk):
    own = pl.program_id(1)
    items = KV_GROUP * MOBA_BLOCK
    group_keys = ATT_GROUP * MOBA_BLOCK
    half = ROPE_DIM // 2
    x = x_ref[0]
    h = _rms(x, g_ref[...]).astype(BF16)
    q = _dot(h, wq_ref[...])
    pos = (own * MOBA_BLOCK + lax.broadcasted_iota(jnp.int32, (1, MOBA_BLOCK), 1)).astype(F32)
    ang = invf_ref[...] * pos
    cos_t, sin_t = jnp.cos(ang), jnp.sin(ang)

    def head_t(i):
        qt = q[:, i * HEAD_DIM:(i + 1) * HEAD_DIM].T
        x1, x2 = qt[0:half], qt[half:2 * half]
        rot = jnp.concatenate([x1 * cos_t - x2 * sin_t, x2 * cos_t + x1 * sin_t, qt[2 * half:]], axis=0)
        return rot * (HEAD_DIM ** -0.5)

    blk = lax.broadcasted_iota(jnp.int32, (nblk, items), 0)
    blk_f = blk.astype(F32)
    past = blk < own
    causal = (lax.broadcasted_iota(jnp.int32, (MOBA_BLOCK, items), 0)
              <= lax.broadcasted_iota(jnp.int32, (MOBA_BLOCK, items), 1) % MOBA_BLOCK)
    feat_pad = jnp.zeros((HEAD_DIM - nblk, items), BF16)

    outs = []
    for kh in range(N_KV_HEADS):
        qt = jnp.concatenate([head_t(kh * KV_GROUP + i) for i in range(KV_GROUP)], axis=1)
        q_hi, q_lo = _split_bf16(qt)
        k_hi, k_lo = _split_bf16(km_ref[0, kh])
        g_hi = _dot(jnp.concatenate([k_hi, k_lo], axis=0), q_hi)
        gate = g_hi[0:nblk] + (_dot(k_hi, q_lo) + g_hi[nblk:2 * nblk])
        cur = jnp.where(past, gate, -jnp.inf)
        bias = jnp.where(blk == own, 0.0, NEG_INF)
        for _ in range(MOBA_TOPK):
            best = jnp.max(cur, axis=0, keepdims=True)
            cand = jnp.where((cur == best) & (best > -jnp.inf), blk_f, float(nblk))
            pick = blk_f == jnp.min(cand, axis=0, keepdims=True)
            bias = jnp.where(pick, 0.0, bias)
            cur = jnp.where(pick, -jnp.inf, cur)
        q_feat = (qt * LOG2E).astype(BF16)
        qa_ref[...] = jnp.concatenate([q_feat, bias.astype(BF16), feat_pad], axis=0)

        def score_group(gi, kh=kh):
            keys = pl.ds(pl.multiple_of(gi * group_keys, group_keys), group_keys)
            return _dot(ka_ref[kh, 0, keys, :], qa_ref[...])

        def consume(s_cur_ref, gi, m_prev, kh=kh):
            s = s_cur_ref[...]
            m_new = jnp.maximum(m_prev, jnp.max(s, axis=0, keepdims=True))
            alpha = jnp.exp2(m_prev - m_new)
            pb = jnp.exp2((s - m_new).astype(BF16))
            n0 = gi * ATT_GROUP
            pv = _dot(vt_ref[0, n0, kh], pb[0:MOBA_BLOCK])
            for j in range(1, ATT_GROUP):
                pv = pv + _dot(vt_ref[0, n0 + j, kh], pb[j * MOBA_BLOCK:(j + 1) * MOBA_BLOCK])
            acc_ref[...] = alpha * acc_ref[...] + pv
            return m_new

        def consume_own(s_cur_ref, m_prev):
            slot = pl.ds(pl.multiple_of((own % ATT_GROUP) * MOBA_BLOCK, MOBA_BLOCK), MOBA_BLOCK)
            s_cur_ref[slot, :] = jnp.where(causal, s_cur_ref[slot, :], NEG_INF)
            consume(s_cur_ref, own_group, m_prev)

        own_group = own // ATT_GROUP
        acc_ref[...] = jnp.zeros(acc_ref.shape, F32)
        m = jnp.full((1, items), SCORE_FLOOR, F32)
        s0_ref[...] = score_group(0)

        def group_pair(pi, m_prev):
            g = 2 * pi
            s1_ref[...] = score_group(g + 1)
            m_mid = consume(s0_ref, g, m_prev)
            s0_ref[...] = score_group(g + 2)
            return consume(s1_ref, g + 1, m_mid)

        m = lax.fori_loop(0, own_group // 2, group_pair, m)

        @pl.when(own_group % 2 == 0)
        def _(m=m):
            consume_own(s0_ref, m)

        @pl.when(own_group % 2 == 1)
        def _(m=m):
            s1_ref[...] = score_group(own_group)
            consume_own(s1_ref, consume(s0_ref, own_group - 1, m))

        ot = acc_ref[0:HEAD_DIM, :] / acc_ref[HEAD_DIM:HEAD_DIM + 1, :]
        outs.extend(ot[:, i * MOBA_BLOCK:(i + 1) * MOBA_BLOCK].T for i in range(KV_GROUP))

    attn = jnp.concatenate(outs, axis=1).astype(BF16)
    o_ref[0] = x + _dot(attn, wo_ref[...])


def _attn_call(x3, g, w_q, w_o, ka, vt, km, invf_col):
    bsz, seq, _ = x3.shape
    nblk = seq // MOBA_BLOCK
    items = KV_GROUP * MOBA_BLOCK
    x_spec = pl.BlockSpec((1, MOBA_BLOCK, D_MODEL), lambda b, i: (b, i, 0))
    return pl.pallas_call(
        functools.partial(_attn_body, nblk=nblk),
        grid=(bsz, nblk),
        in_specs=[x_spec, _resident((1, D_MODEL)),
                  _resident((D_MODEL, D_MODEL)), _resident((D_MODEL, D_MODEL)),
                  pl.BlockSpec((N_KV_HEADS, 1, seq, 2 * HEAD_DIM), lambda b, i: (0, b, 0, 0),
                               pipeline_mode=pl.Buffered(1)),
                  pl.BlockSpec((1, nblk, N_KV_HEADS, V_ROWS, MOBA_BLOCK),
                               lambda b, i: (b, 0, 0, 0, 0), pipeline_mode=pl.Buffered(1)),
                  pl.BlockSpec((1, N_KV_HEADS, nblk, HEAD_DIM), lambda b, i: (b, 0, 0, 0)),
                  _resident((ROPE_DIM // 2, 1))],
        out_specs=x_spec,
        out_shape=jax.ShapeDtypeStruct(x3.shape, F32),
        scratch_shapes=[pltpu.VMEM((2 * HEAD_DIM, items), BF16),
                        pltpu.VMEM((V_ROWS, items), F32),
                        pltpu.VMEM((ATT_GROUP * MOBA_BLOCK, items), F32),
                        pltpu.VMEM((ATT_GROUP * MOBA_BLOCK, items), F32)],
        compiler_params=pltpu.CompilerParams(
            dimension_semantics=("arbitrary", "arbitrary"), vmem_limit_bytes=VMEM_LIMIT),
        name="moba_attn",
    )(x3, g.reshape(1, D_MODEL), w_q.astype(BF16), w_o.astype(BF16), ka, vt, km, invf_col)


def kernel(x, norm_g, ffn_w_in, ffn_w_out, s5_a_re, s5_a_im, s5_log_step, s5_b_re, s5_b_im,
           s5_c_re, s5_c_im, s5_d, s5_w_glu, kv_norm_g, w_k, w_v, w_q, w_o, final_g):
    bsz, seq, _ = x.shape
    assert seq % S5_CHUNK == 0 and seq % MOBA_BLOCK == 0
    assert (seq // MOBA_BLOCK) % (2 * SUBLANES) == 0 and seq // MOBA_BLOCK <= HEAD_DIM
    assert (seq // MOBA_BLOCK) % ATT_GROUP == 0
    assert (bsz * seq) % FFN_ROWS == 0 and FFN_ROWS % MOBA_BLOCK == 0
    nblk = seq // MOBA_BLOCK
    x2 = x.reshape(bsz * seq, D_MODEL)
    invf, rope_tab = _rope_consts()

    w_in_b = ffn_w_in
    w_out_b = ffn_w_out
    wkv = jnp.concatenate([w_k, w_v], axis=1).astype(BF16)

    x2 = _ffn_call(x2, norm_g[0, 0], w_in_b, w_out_b, 0, 0)
    prep = _s5_prep_call(s5_a_re[0], s5_a_im[0], s5_log_step[0], s5_b_re[0], s5_b_im[0], s5_c_im[0])
    x2 = _s5_call(x2, bsz, seq, norm_g[0, 1], prep, None, s5_c_re[0], s5_d[0], s5_w_glu[0])
    x2, ka, vt, km = _ffn_call(x2, norm_g[0, 2], w_in_b, w_out_b, 0, 1,
                               kv=(kv_norm_g, wkv, invf, rope_tab), nblk=nblk)
    ka = ka.reshape(N_KV_HEADS, bsz, seq, 2 * HEAD_DIM)
    vt = vt.reshape(bsz, nblk, N_KV_HEADS, V_ROWS, MOBA_BLOCK)
    km = km.reshape(bsz, nblk, N_KV_HEADS, HEAD_DIM).transpose(0, 2, 1, 3)

    x2 = _ffn_call(x2, norm_g[1, 0], w_in_b, w_out_b, 1, 0)
    x3 = _attn_call(x2.reshape(bsz, seq, D_MODEL), norm_g[1, 1], w_q[0], w_o[0], ka, vt, km,
                    invf[0, 0:ROPE_DIM // 2].reshape(ROPE_DIM // 2, 1))
    x2 = _ffn_call(x3.reshape(bsz * seq, D_MODEL), norm_g[1, 2], w_in_b, w_out_b, 1, 1,
                   final_g=final_g)
    return x2.reshape(bsz, seq, D_MODEL)
```

```python
import functools

import jax
import jax.numpy as jnp
import numpy as np
from jax import lax
from jax.experimental import pallas as pl
from jax.experimental.pallas import tpu as pltpu

F32 = jnp.float32
BF16 = jnp.bfloat16

D_MODEL = 1024
D_FF = 2816
RMS_EPS = 1e-6
S5_GROUP = 16
S5_GROUPS = D_MODEL // S5_GROUP
S5_STATE = 64
N_HEADS = 8
HEAD_DIM = 128
N_KV_HEADS = 2
KV_GROUP = N_HEADS // N_KV_HEADS
ROPE_DIM = HEAD_DIM // 4
ROPE_THETA = 500000.0
MOBA_BLOCK = 256
MOBA_TOPK = 3
NEG_INF = -1e30
SCORE_FLOOR = 0.5 * NEG_INF
LOG2E = 1.4426950408889634

LANES = 128
SUBLANES = 8
VMEM_LIMIT = 56 * 1024 * 1024

FFN_ROWS = 512
FFN_COLS = 256

ATT_GROUP = 4
V_ROWS = HEAD_DIM + 2 * SUBLANES

S5_CHUNK = 256
S5_SUBSEQ = SUBLANES
S5_STEPS = S5_CHUNK // S5_SUBSEQ
S5_LANE_BLOCKS = D_MODEL // LANES
S5_GROUPS_PER_BLOCK = LANES // S5_GROUP
S5_HALF = S5_GROUPS_PER_BLOCK * S5_STATE
S5_NSTATE = S5_GROUPS * S5_STATE
S5_GLU_COLS = 2 * D_MODEL // S5_LANE_BLOCKS

def _rms(x, g):
    ms = jnp.mean(x * x, axis=-1, keepdims=True)
    return (x * lax.rsqrt(ms + RMS_EPS)) * g


def _dot(a, b):
    return jnp.dot(a, b, preferred_element_type=F32)


def _resident(shape):
    nd = len(shape)
    return pl.BlockSpec(shape, lambda *_: (0,) * nd, pipeline_mode=pl.Buffered(1))


def _ffn_body(*refs, mode, nblk):
    x_ref, g_ref, win_ref, wout_ref = refs[:4]
    x = x_ref[...]
    h = _rms(x, g_ref[...]).astype(BF16)
    acc = jnp.zeros(x.shape, F32)
    for c in range(D_FF // FFN_COLS):
        lo = c * FFN_COLS
        gate = _dot(h, win_ref[:, lo:lo + FFN_COLS].astype(BF16))
        up = _dot(h, win_ref[:, D_FF + lo:D_FF + lo + FFN_COLS].astype(BF16))
        act = (gate * jax.nn.sigmoid(gate)) * up
        acc = acc + _dot(act.astype(BF16), wout_ref[lo:lo + FFN_COLS, :].astype(BF16))
    y = x + 0.5 * acc
    if mode == 'final':
        fg_ref, o_ref = refs[4:]
        y = _rms(y, fg_ref[...])
    elif mode == 'kv':
        kvg_ref, wkv_ref, invf_ref, tab_ref, o_ref, ka_ref, vt_ref, km_ref = refs[4:]
        for sub in range(FFN_ROWS // MOBA_BLOCK):
            rows = slice(sub * MOBA_BLOCK, (sub + 1) * MOBA_BLOCK)
            blk = (pl.program_id(0) * (FFN_ROWS // MOBA_BLOCK) + sub) % nblk
            _kv_tile(y[rows], blk, kvg_ref[...], wkv_ref, invf_ref[...], tab_ref,
                     ka_ref, vt_ref, km_ref, sub)
    else:
        o_ref, = refs[4:]
    o_ref[...] = y


def _ffn_call(x2, g, w_in_all, w_out_all, layer, idx, final_g=None, kv=None, nblk=None):
    t = x2.shape[0]
    mode = 'final' if final_g is not None else ('kv' if kv is not None else 'plain')
    row_spec = pl.BlockSpec((FFN_ROWS, D_MODEL), lambda i: (i, 0))
    pick = lambda i: (layer, idx, 0, 0)
    in_specs = [row_spec, _resident((1, D_MODEL)),
                pl.BlockSpec((None, None, D_MODEL, 2 * D_FF), pick, pipeline_mode=pl.Buffered(1)),
                pl.BlockSpec((None, None, D_FF, D_MODEL), pick, pipeline_mode=pl.Buffered(1))]
    args = [x2, g.reshape(1, D_MODEL), w_in_all, w_out_all]
    out_specs = row_spec
    out_shape = jax.ShapeDtypeStruct((t, D_MODEL), F32)
    if mode == 'final':
        in_specs.append(_resident((1, D_MODEL)))
        args.append(final_g.reshape(1, D_MODEL))
    elif mode == 'kv':
        kvg, wkv, invf, rope_tab = kv
        kdim = N_KV_HEADS * HEAD_DIM
        sub = FFN_ROWS // MOBA_BLOCK
        ntile = t // MOBA_BLOCK
        in_specs += [_resident((1, D_MODEL)), _resident((D_MODEL, 2 * kdim)),
                     _resident((1, HEAD_DIM)), _resident(rope_tab.shape)]
        args += [kvg.reshape(1, D_MODEL), wkv, invf, rope_tab]
        out_specs = [row_spec,
                     pl.BlockSpec((N_KV_HEADS, FFN_ROWS, 2 * HEAD_DIM), lambda i: (0, i, 0)),
                     pl.BlockSpec((sub, N_KV_HEADS, V_ROWS, MOBA_BLOCK), lambda i: (i, 0, 0, 0)),
                     pl.BlockSpec((sub, 1, kdim), lambda i: (i, 0, 0))]
        out_shape = [out_shape,
                     jax.ShapeDtypeStruct((N_KV_HEADS, t, 2 * HEAD_DIM), BF16),
                     jax.ShapeDtypeStruct((ntile, N_KV_HEADS, V_ROWS, MOBA_BLOCK), BF16),
                     jax.ShapeDtypeStruct((ntile, 1, kdim), F32)]
    return pl.pallas_call(
        functools.partial(_ffn_body, mode=mode, nblk=nblk),
        grid=(t // FFN_ROWS,),
        in_specs=in_specs,
        out_specs=out_specs,
        out_shape=out_shape,
        compiler_params=pltpu.CompilerParams(
            dimension_semantics=("arbitrary",), vmem_limit_bytes=VMEM_LIMIT),
        name="ffn_" + mode,
    )(*args)


def _s5_prep_body(lr_ref, li_ref, ls_ref, bre_ref, bim_ref, cim_ref,
                  ar_ref, ai_ref, amr_ref, ami_ref, pr_ref, pi_ref, btr_ref, bti_ref, cneg_ref):
    lr = lr_ref[...]
    li = li_ref[...]
    dt = jnp.exp(ls_ref[...])
    mag = jnp.exp(lr * dt)
    abar_re = mag * jnp.cos(li * dt)
    abar_im = mag * jnp.sin(li * dt)
    ar_ref[...] = abar_re
    ai_ref[...] = abar_im
    nr, ni = abar_re - 1.0, abar_im
    den = lr * lr + li * li
    coef_re = (nr * lr + ni * li) / den
    coef_im = (ni * lr - nr * li) / den
    k = (lax.broadcasted_iota(jnp.int32, (S5_STEPS, 1), 0) + 1).astype(F32)
    pmag = jnp.exp((lr * dt) * k)
    pang = (li * dt) * k
    pr = pmag * jnp.cos(pang)
    pi = pmag * jnp.sin(pang)
    for j in range(S5_LANE_BLOCKS):
        cols = slice(j * S5_HALF, (j + 1) * S5_HALF)
        for step in range(S5_STEPS):
            rows = slice(step * SUBLANES, (step + 1) * SUBLANES)
            pr_ref[j, rows, :] = jnp.broadcast_to(pr[step:step + 1, cols], (SUBLANES, S5_HALF))
            pi_ref[j, rows, :] = jnp.broadcast_to(pi[step:step + 1, cols], (SUBLANES, S5_HALF))
    amr_ref[...] = pr[S5_STEPS - 1:S5_STEPS, :]
    ami_ref[...] = pi[S5_STEPS - 1:S5_STEPS, :]
    bre = bre_ref[...]
    bim = bim_ref[...]
    btr_ref[...] = coef_re * bre - coef_im * bim
    bti_ref[...] = coef_re * bim + coef_im * bre
    cneg_ref[...] = -cim_ref[...]


def _s5_prep_call(a_re, a_im, log_step, b_re, b_im, c_im):
    n = S5_NSTATE
    row = lambda v: v.reshape(1, n)
    chan_major = lambda v: v.transpose(2, 0, 1).reshape(S5_GROUP, n)
    ls = jnp.repeat(log_step, S5_STATE)
    outs = pl.pallas_call(
        _s5_prep_body,
        out_shape=[jax.ShapeDtypeStruct((1, n), F32)] * 4
        + [jax.ShapeDtypeStruct((S5_LANE_BLOCKS, S5_CHUNK, S5_HALF), F32)] * 2
        + [jax.ShapeDtypeStruct((S5_GROUP, n), F32)] * 3,
        name="s5_prep",
    )(row(a_re), row(a_im), row(ls), chan_major(b_re), chan_major(b_im),
      c_im.transpose(1, 0, 2).reshape(S5_GROUP, n))
    return outs


def _s5_body(x_ref, xprev_ref, g_ref, perm_ref, permt_ref, bblk_ref, cblk_ref, ar_ref, ai_ref,
             amr_ref, ami_ref, pr_ref, pi_ref, d_ref, wglu_ref, o_ref,
             xs_ref, st_ref, c_ref, hb_ref, y_ref, *, nchunk):
    step = pl.program_id(0)

    @pl.when(step % nchunk == 0)
    def _():
        st_ref[...] = jnp.zeros(st_ref.shape, F32)

    @pl.when(step == 0)
    def _():
        y_ref[...] = jnp.zeros(y_ref.shape, F32)

    yp = y_ref[...]
    yp = 0.5 * yp * (1.0 + jnp.tanh(np.sqrt(2.0 / np.pi).astype(np.float32)
                                    * (yp + 0.044715 * (yp * yp * yp))))
    yn = _dot(permt_ref[...], yp.astype(BF16)).astype(BF16)

    def glu_piece(c):
        cols = slice(c * S5_GLU_COLS, (c + 1) * S5_GLU_COLS)
        gcols = slice(D_MODEL + c * S5_GLU_COLS, D_MODEL + (c + 1) * S5_GLU_COLS)
        val = _dot(yn, wglu_ref[:, cols])
        gate = _dot(yn, wglu_ref[:, gcols])
        o_ref[:, cols] = xprev_ref[:, cols] + val * jax.nn.sigmoid(gate)

    x = x_ref[...]
    u = _rms(x, g_ref[...])
    u_hi = u.astype(BF16)
    u_lo = (u - u_hi.astype(F32)).astype(BF16)
    perm = perm_ref[...]
    up_hi = _dot(perm, u_hi)
    up = up_hi + _dot(perm, u_lo)
    ub = up_hi.astype(BF16)

    last = SUBLANES * (S5_STEPS - 1)
    nb = S5_LANE_BLOCKS
    re, im = slice(0, S5_HALF), slice(S5_HALF, 2 * S5_HALF)

    def project_in(j):
        xs_ref[j] = _dot(ub[:, j * LANES:(j + 1) * LANES], bblk_ref[j])

    for j in range(nb):
        project_in(j)

    def scan_steps(j):
        a_r = jnp.broadcast_to(ar_ref[j], (SUBLANES, S5_HALF))
        a_i = jnp.broadcast_to(ai_ref[j], (SUBLANES, S5_HALF))
        state = [xs_ref[j, 0:SUBLANES, re], xs_ref[j, 0:SUBLANES, im]]

        def step(k):
            rows = slice(k * SUBLANES, (k + 1) * SUBLANES)
            h_r, h_i = state
            state[0] = a_r * h_r - a_i * h_i + xs_ref[j, rows, re]
            state[1] = a_r * h_i + a_i * h_r + xs_ref[j, rows, im]
            xs_ref[j, rows, re] = state[0]
            xs_ref[j, rows, im] = state[1]

        return [functools.partial(step, k) for k in range(1, S5_STEPS)]

    def entering_states(j):
        am_r, am_i = amr_ref[j], ami_ref[j]
        c_r, c_i = st_ref[j, :, re], st_ref[j, :, im]
        for i in range(S5_SUBSEQ):
            c_ref[j, i:i + 1, re] = c_r
            c_ref[j, i:i + 1, im] = c_i
            e_r = xs_ref[j, last + i:last + i + 1, re]
            e_i = xs_ref[j, last + i:last + i + 1, im]
            c_r, c_i = am_r * c_r - am_i * c_i + e_r, am_r * c_i + am_i * c_r + e_i
        st_ref[j, :, re] = c_r
        st_ref[j, :, im] = c_i

    def fix_steps(j):
        cc_r = jnp.concatenate([c_ref[j, :, re]] * 2, axis=0)
        cc_i = jnp.concatenate([c_ref[j, :, im]] * 2, axis=0)

        def step(k):
            rows = slice(2 * k * SUBLANES, 2 * (k + 1) * SUBLANES)
            p_r, p_i = pr_ref[j, rows, :], pi_ref[j, rows, :]
            t_r = xs_ref[j, rows, re] + (p_r * cc_r - p_i * cc_i)
            t_i = xs_ref[j, rows, im] + (p_r * cc_i + p_i * cc_r)
            hb_ref[j, rows, :] = jnp.concatenate([t_r, t_i], axis=1).astype(BF16)

        return [functools.partial(step, k) for k in range(S5_STEPS // 2)]

    ys = []
    for j in range(nb + 1):
        if j % 2 == 0 and j < nb:
            glu_piece(j // 2)
        p1 = scan_steps(j) if j < nb else []
        p2 = fix_steps(j - 1) if j >= 1 else []
        for k in range(max(len(p2), (len(p1) + 1) // 2)):
            for f in p1[2 * k:2 * k + 2]:
                f()
            if k < len(p2):
                p2[k]()
        if j < nb:
            entering_states(j)
        if j >= 1:
            ys.append(_dot(hb_ref[j - 1], cblk_ref[j - 1]))

    y_ref[...] = jnp.concatenate(ys, axis=1) + d_ref[...] * up


def _s5_call(x2, bsz, seq, g, prep, c_re, d_skip, w_glu):
    ar, ai, amr, ami, pr, pi, btr, bti, cneg = prep
    nb, gb, ns = S5_LANE_BLOCKS, S5_GROUPS_PER_BLOCK, S5_STATE
    eye = jnp.eye(gb, dtype=F32)

    def per_block(v, rows):
        return v.reshape(rows, nb, S5_HALF).transpose(1, 0, 2)

    bt = jnp.stack([btr, bti], axis=1).reshape(S5_GROUP, 2, nb, gb, ns)
    bblk = jnp.einsum('crjgp,gh->jgcrhp', bt, eye).reshape(nb, LANES, 2 * S5_HALF).astype(BF16)
    cmat = jnp.stack([c_re.transpose(1, 0, 2).reshape(S5_GROUP, S5_NSTATE), cneg], axis=1)
    cmat = cmat.reshape(S5_GROUP, 2, nb, gb, ns)
    cblk = jnp.einsum('crjgp,gh->jrgphc', cmat, eye).reshape(nb, 2 * S5_HALF, LANES).astype(BF16)

    r = np.arange(S5_CHUNK)
    perm_np = np.zeros((S5_CHUNK, S5_CHUNK), np.float32)
    perm_np[r, (r % SUBLANES) * S5_STEPS + r // SUBLANES] = 1.0
    perm = jnp.asarray(perm_np, BF16)
    permt = jnp.asarray(perm_np.T, BF16)

    nchunk = seq // S5_CHUNK
    total = bsz * nchunk
    cur_spec = pl.BlockSpec((S5_CHUNK, D_MODEL), lambda s: (jnp.minimum(s, total - 1), 0))
    prev_spec = pl.BlockSpec((S5_CHUNK, D_MODEL), lambda s: (jnp.maximum(s - 1, 0), 0))
    in_specs = [
        cur_spec, prev_spec, _resident((1, D_MODEL)),
        _resident((S5_CHUNK, S5_CHUNK)), _resident((S5_CHUNK, S5_CHUNK)),
        _resident((nb, LANES, 2 * S5_HALF)), _resident((nb, 2 * S5_HALF, LANES)),
        _resident((nb, 1, S5_HALF)), _resident((nb, 1, S5_HALF)),
        _resident((nb, 1, S5_HALF)), _resident((nb, 1, S5_HALF)),
        _resident((nb, S5_CHUNK, S5_HALF)), _resident((nb, S5_CHUNK, S5_HALF)),
        _resident((1, D_MODEL)), _resident((D_MODEL, 2 * D_MODEL)),
    ]
    return pl.pallas_call(
        functools.partial(_s5_body, nchunk=nchunk),
        grid=(total + 1,),
        in_specs=in_specs,
        out_specs=prev_spec,
        out_shape=jax.ShapeDtypeStruct(x2.shape, F32),
        scratch_shapes=[
            pltpu.VMEM((nb, S5_CHUNK, 2 * S5_HALF), F32),
            pltpu.VMEM((nb, 1, 2 * S5_HALF), F32),
            pltpu.VMEM((nb, S5_SUBSEQ, 2 * S5_HALF), F32),
            pltpu.VMEM((nb, S5_CHUNK, 2 * S5_HALF), BF16),
            pltpu.VMEM((S5_CHUNK, D_MODEL), F32),
        ],
        compiler_params=pltpu.CompilerParams(
            dimension_semantics=("arbitrary",), vmem_limit_bytes=VMEM_LIMIT),
        name="s5",
    )(x2, x2, g.reshape(1, D_MODEL), perm, permt, bblk, cblk,
      per_block(ar, 1), per_block(ai, 1), per_block(amr, 1), per_block(ami, 1),
      pr, pi,
      d_skip.reshape(1, D_MODEL), w_glu.astype(BF16))


def _rope_tables(pos0, invf, tab_ref):
    ang0 = pos0.astype(F32) * invf
    c0, s0 = jnp.cos(ang0), jnp.sin(ang0)
    cos_t = c0 * tab_ref[0] - s0 * tab_ref[1]
    sin_t = s0 * tab_ref[2] + c0 * tab_ref[3]
    return cos_t, sin_t


def _rope_head(xh, cos_t, sin_t, low_half):
    half = ROPE_DIM // 2
    swapped = jnp.where(low_half, pltpu.roll(xh, LANES - half, axis=1), pltpu.roll(xh, half, axis=1))
    return xh * cos_t + swapped * sin_t


def _rope_consts():
    half = ROPE_DIM // 2
    inv_freq = ROPE_THETA ** (-jnp.arange(0, ROPE_DIM, 2, dtype=F32) / ROPE_DIM)
    pad = jnp.zeros((HEAD_DIM - ROPE_DIM,), F32)
    invf = jnp.concatenate([inv_freq, inv_freq, pad]).reshape(1, HEAD_DIM)
    sign = jnp.concatenate([-jnp.ones((half,), F32), jnp.ones((half,), F32), pad]).reshape(1, HEAD_DIM)
    ang_r = jnp.arange(MOBA_BLOCK, dtype=F32)[:, None] * invf
    cos_r, sin_r = jnp.cos(ang_r), jnp.sin(ang_r)
    tables = jnp.stack([cos_r, sin_r, cos_r * sign, sin_r * sign])
    return invf, tables


def _kv_tile(x, blk, g, wkv_ref, invf, tab_ref, ka_ref, vt_ref, km_ref, sub):
    rows = slice(sub * MOBA_BLOCK, (sub + 1) * MOBA_BLOCK)
    h = _rms(x, g).astype(BF16)
    kv = _dot(h, wkv_ref[...])
    kdim = N_KV_HEADS * HEAD_DIM
    cos_t, sin_t = _rope_tables(blk * MOBA_BLOCK, invf, tab_ref)
    lane = lax.broadcasted_iota(jnp.int32, (MOBA_BLOCK, HEAD_DIM), 1)
    low_half = lane < ROPE_DIM // 2
    onehot = jnp.where(lane == blk, 1.0, 0.0).astype(BF16)
    pad_row = lax.broadcasted_iota(jnp.int32, (V_ROWS - HEAD_DIM, MOBA_BLOCK), 0)
    ones_rows = jnp.where(pad_row == 0, 1.0, 0.0).astype(BF16)
    means = []
    for hh in range(N_KV_HEADS):
        kh = _rope_head(kv[:, hh * HEAD_DIM:(hh + 1) * HEAD_DIM], cos_t, sin_t, low_half)
        means.append(jnp.mean(kh, axis=0, keepdims=True))
        ka_ref[hh, rows, :] = jnp.concatenate([kh.astype(BF16), onehot], axis=1)
        vt = kv[:, kdim + hh * HEAD_DIM:kdim + (hh + 1) * HEAD_DIM].T.astype(BF16)
        vt_ref[sub, hh] = jnp.concatenate([vt, ones_rows], axis=0)
    km_ref[sub] = jnp.concatenate(means, axis=1)


def _split_bf16(v):
    hi = v.astype(BF16)
    return hi, (v - hi.astype(F32)).astype(BF16)


def _attn_body(x_ref, g_ref, wq_ref, wo_ref, ka_ref, vt_ref, km_ref, invf_ref, o_ref,
               qa_ref, acc_ref, s0_ref, s1_ref, *, nblk):
    own = pl.program_id(1)
    items = KV_GROUP * MOBA_BLOCK
    group_keys = ATT_GROUP * MOBA_BLOCK
    half = ROPE_DIM // 2
    x = x_ref[0]
    h = _rms(x, g_ref[...]).astype(BF16)
    q = _dot(h, wq_ref[...])
    pos = (own * MOBA_BLOCK + lax.broadcasted_iota(jnp.int32, (1, MOBA_BLOCK), 1)).astype(F32)
    ang = invf_ref[...] * pos
    cos_t, sin_t = jnp.cos(ang), jnp.sin(ang)

    def head_t(i):
        qt = q[:, i * HEAD_DIM:(i + 1) * HEAD_DIM].T
        x1, x2 = qt[0:half], qt[half:2 * half]
        rot = jnp.concatenate([x1 * cos_t - x2 * sin_t, x2 * cos_t + x1 * sin_t, qt[2 * half:]], axis=0)
        return rot * (HEAD_DIM ** -0.5)

    blk = lax.broadcasted_iota(jnp.int32, (nblk, items), 0)
    blk_f = blk.astype(F32)
    past = blk < own
    causal = (lax.broadcasted_iota(jnp.int32, (MOBA_BLOCK, items), 0)
              <= lax.broadcasted_iota(jnp.int32, (MOBA_BLOCK, items), 1) % MOBA_BLOCK)
    feat_pad = jnp.zeros((HEAD_DIM - nblk, items), BF16)

    outs = []
    for kh in range(N_KV_HEADS):
        qt = jnp.concatenate([head_t(kh * KV_GROUP + i) for i in range(KV_GROUP)], axis=1)
        q_hi, q_lo = _split_bf16(qt)
        k_hi, k_lo = _split_bf16(km_ref[0, kh])
        g_hi = _dot(jnp.concatenate([k_hi, k_lo], axis=0), q_hi)
        gate = g_hi[0:nblk] + (_dot(k_hi, q_lo) + g_hi[nblk:2 * nblk])
        cur = jnp.where(past, gate, -jnp.inf)
        bias = jnp.where(blk == own, 0.0, NEG_INF)
        for _ in range(MOBA_TOPK):
            best = jnp.max(cur, axis=0, keepdims=True)
            cand = jnp.where((cur == best) & (best > -jnp.inf), blk_f, float(nblk))
            pick = blk_f == jnp.min(cand, axis=0, keepdims=True)
            bias = jnp.where(pick, 0.0, bias)
            cur = jnp.where(pick, -jnp.inf, cur)
        q_feat = (qt * LOG2E).astype(BF16)
        qa_ref[...] = jnp.concatenate([q_feat, bias.astype(BF16), feat_pad], axis=0)

        def score_group(gi, kh=kh):
            keys = pl.ds(pl.multiple_of(gi * group_keys, group_keys), group_keys)
            return _dot(ka_ref[kh, 0, keys, :], qa_ref[...])

        def consume(s_cur_ref, gi, m_prev, kh=kh):
            s = s_cur_ref[...]
            m_new = jnp.maximum(m_prev, jnp.max(s, axis=0, keepdims=True))
            alpha = jnp.exp2(m_prev - m_new)
            pb = jnp.exp2((s - m_new).astype(BF16))
            n0 = gi * ATT_GROUP
            pv = _dot(vt_ref[0, n0, kh], pb[0:MOBA_BLOCK])
            for j in range(1, ATT_GROUP):
                pv = pv + _dot(vt_ref[0, n0 + j, kh], pb[j * MOBA_BLOCK:(j + 1) * MOBA_BLOCK])
            acc_ref[...] = alpha * acc_ref[...] + pv
            return m_new

        def consume_own(s_cur_ref, m_prev):
            slot = pl.ds(pl.multiple_of((own % ATT_GROUP) * MOBA_BLOCK, MOBA_BLOCK), MOBA_BLOCK)
            s_cur_ref[slot, :] = jnp.where(causal, s_cur_ref[slot, :], NEG_INF)
            consume(s_cur_ref, own_group, m_prev)

        own_group = own // ATT_GROUP
        acc_ref[...] = jnp.zeros(acc_ref.shape, F32)
        m = jnp.full((1, items), SCORE_FLOOR, F32)
        s0_ref[...] = score_group(0)

        def group_pair(pi, m_prev):
            g = 2 * pi
            s1_ref[...] = score_group(g + 1)
            m_mid = consume(s0_ref, g, m_prev)
            s0_ref[...] = score_group(g + 2)
            return consume(s1_ref, g + 1, m_mid)

        m = lax.fori_loop(0, own_group // 2, group_pair, m)

        @pl.when(own_group % 2 == 0)
        def _(m=m):
            consume_own(s0_ref, m)

        @pl.when(own_group % 2 == 1)
        def _(m=m):
            s1_ref[...] = score_group(own_group)
            consume_own(s1_ref, consume(s0_ref, own_group - 1, m))

        ot = acc_ref[0:HEAD_DIM, :] / acc_ref[HEAD_DIM:HEAD_DIM + 1, :]
        outs.extend(ot[:, i * MOBA_BLOCK:(i + 1) * MOBA_BLOCK].T for i in range(KV_GROUP))

    attn = jnp.concatenate(outs, axis=1).astype(BF16)
    o_ref[0] = x + _dot(attn, wo_ref[...])


def _attn_call(x3, g, w_q, w_o, ka, vt, km, invf_col):
    bsz, seq, _ = x3.shape
    nblk = seq // MOBA_BLOCK
    items = KV_GROUP * MOBA_BLOCK
    x_spec = pl.BlockSpec((1, MOBA_BLOCK, D_MODEL), lambda b, i: (b, i, 0))
    return pl.pallas_call(
        functools.partial(_attn_body, nblk=nblk),
        grid=(bsz, nblk),
        in_specs=[x_spec, _resident((1, D_MODEL)),
                  _resident((D_MODEL, D_MODEL)), _resident((D_MODEL, D_MODEL)),
                  pl.BlockSpec((N_KV_HEADS, 1, seq, 2 * HEAD_DIM), lambda b, i: (0, b, 0, 0),
                               pipeline_mode=pl.Buffered(1)),
                  pl.BlockSpec((1, nblk, N_KV_HEADS, V_ROWS, MOBA_BLOCK),
                               lambda b, i: (b, 0, 0, 0, 0), pipeline_mode=pl.Buffered(1)),
                  pl.BlockSpec((1, N_KV_HEADS, nblk, HEAD_DIM), lambda b, i: (b, 0, 0, 0)),
                  _resident((ROPE_DIM // 2, 1))],
        out_specs=x_spec,
        out_shape=jax.ShapeDtypeStruct(x3.shape, F32),
        scratch_shapes=[pltpu.VMEM((2 * HEAD_DIM, items), BF16),
                        pltpu.VMEM((V_ROWS, items), F32),
                        pltpu.VMEM((ATT_GROUP * MOBA_BLOCK, items), F32),
                        pltpu.VMEM((ATT_GROUP * MOBA_BLOCK, items), F32)],
        compiler_params=pltpu.CompilerParams(
            dimension_semantics=("arbitrary", "arbitrary"), vmem_limit_bytes=VMEM_LIMIT),
        name="moba_attn",
    )(x3, g.reshape(1, D_MODEL), w_q.astype(BF16), w_o.astype(BF16), ka, vt, km, invf_col)


def kernel(x, norm_g, ffn_w_in, ffn_w_out, s5_a_re, s5_a_im, s5_log_step, s5_b_re, s5_b_im,
           s5_c_re, s5_c_im, s5_d, s5_w_glu, kv_norm_g, w_k, w_v, w_q, w_o, final_g):
    bsz, seq, _ = x.shape
    assert seq % S5_CHUNK == 0 and seq % MOBA_BLOCK == 0
    assert (seq // MOBA_BLOCK) % (2 * SUBLANES) == 0 and seq // MOBA_BLOCK <= HEAD_DIM
    assert (seq // MOBA_BLOCK) % ATT_GROUP == 0
    assert (bsz * seq) % FFN_ROWS == 0 and FFN_ROWS % MOBA_BLOCK == 0
    nblk = seq // MOBA_BLOCK
    x2 = x.reshape(bsz * seq, D_MODEL)
    invf, rope_tab = _rope_consts()

    w_in_b = ffn_w_in
    w_out_b = ffn_w_out
    wkv = jnp.concatenate([w_k, w_v], axis=1).astype(BF16)

    x2 = _ffn_call(x2, norm_g[0, 0], w_in_b, w_out_b, 0, 0)
    prep = _s5_prep_call(s5_a_re[0], s5_a_im[0], s5_log_step[0], s5_b_re[0], s5_b_im[0], s5_c_im[0])
    x2 = _s5_call(x2, bsz, seq, norm_g[0, 1], prep, s5_c_re[0], s5_d[0], s5_w_glu[0])
    x2, ka, vt, km = _ffn_call(x2, norm_g[0, 2], w_in_b, w_out_b, 0, 1,
                               kv=(kv_norm_g, wkv, invf, rope_tab), nblk=nblk)
    ka = ka.reshape(N_KV_HEADS, bsz, seq, 2 * HEAD_DIM)
    vt = vt.reshape(bsz, nblk, N_KV_HEADS, V_ROWS, MOBA_BLOCK)
    km = km.reshape(bsz, nblk, N_KV_HEADS, HEAD_DIM).transpose(0, 2, 1, 3)

    x2 = _ffn_call(x2, norm_g[1, 0], w_in_b, w_out_b, 1, 0)
    x3 = _attn_call(x2.reshape(bsz, seq, D_MODEL), norm_g[1, 1], w_q[0], w_o[0], ka, vt, km,
                    invf[0, 0:ROPE_DIM // 2].reshape(ROPE_DIM // 2, 1))
    x2 = _ffn_call(x3.reshape(bsz * seq, D_MODEL), norm_g[1, 2], w_in_b, w_out_b, 1, 1,
                   final_g=final_g)
    return x2.reshape(bsz, seq, D_MODEL)
```

```python
import functools

import jax
import jax.numpy as jnp
import numpy as np
from jax import lax
from jax.experimental import pallas as pl
from jax.experimental.pallas import tpu as pltpu

F32 = jnp.float32
BF16 = jnp.bfloat16

D_MODEL = 1024
D_FF = 2816
RMS_EPS = 1e-6
S5_GROUP = 16
S5_GROUPS = D_MODEL // S5_GROUP
S5_STATE = 64
N_HEADS = 8
HEAD_DIM = 128
N_KV_HEADS = 2
KV_GROUP = N_HEADS // N_KV_HEADS
ROPE_DIM = HEAD_DIM // 4
ROPE_THETA = 500000.0
MOBA_BLOCK = 256
MOBA_TOPK = 3
NEG_INF = -1e30
SCORE_FLOOR = 0.5 * NEG_INF
LOG2E = 1.4426950408889634

LANES = 128
SUBLANES = 8
VMEM_LIMIT = 56 * 1024 * 1024

FFN_ROWS = 512
FFN_COLS = 256

ATT_GROUP = 4
V_ROWS = HEAD_DIM + 2 * SUBLANES

S5_CHUNK = 256
S5_SUBSEQ = SUBLANES
S5_STEPS = S5_CHUNK // S5_SUBSEQ
S5_LANE_BLOCKS = D_MODEL // LANES
S5_GROUPS_PER_BLOCK = LANES // S5_GROUP
S5_HALF = S5_GROUPS_PER_BLOCK * S5_STATE
S5_NSTATE = S5_GROUPS * S5_STATE
S5_GLU_COLS = 2 * D_MODEL // S5_LANE_BLOCKS

def _rms(x, g):
    ms = jnp.mean(x * x, axis=-1, keepdims=True)
    return (x * lax.rsqrt(ms + RMS_EPS)) * g


def _dot(a, b):
    return jnp.dot(a, b, preferred_element_type=F32)


def _resident(shape):
    nd = len(shape)
    return pl.BlockSpec(shape, lambda *_: (0,) * nd, pipeline_mode=pl.Buffered(1))


def _ffn_body(*refs, mode, nblk):
    x_ref, g_ref, win_ref, wout_ref = refs[:4]
    x = x_ref[...]
    h = _rms(x, g_ref[...]).astype(BF16)
    acc = jnp.zeros(x.shape, F32)
    for c in range(D_FF // FFN_COLS):
        lo = c * FFN_COLS
        gate = _dot(h, win_ref[:, lo:lo + FFN_COLS].astype(BF16))
        up = _dot(h, win_ref[:, D_FF + lo:D_FF + lo + FFN_COLS].astype(BF16))
        act = (gate * jax.nn.sigmoid(gate)) * up
        acc = acc + _dot(act.astype(BF16), wout_ref[lo:lo + FFN_COLS, :].astype(BF16))
    y = x + 0.5 * acc
    if mode == 'final':
        fg_ref, o_ref = refs[4:]
        y = _rms(y, fg_ref[...])
    elif mode == 'kv':
        kvg_ref, wkv_ref, invf_ref, tab_ref, o_ref, ka_ref, vt_ref, km_ref = refs[4:]
        for sub in range(FFN_ROWS // MOBA_BLOCK):
            rows = slice(sub * MOBA_BLOCK, (sub + 1) * MOBA_BLOCK)
            blk = (pl.program_id(0) * (FFN_ROWS // MOBA_BLOCK) + sub) % nblk
            _kv_tile(y[rows], blk, kvg_ref[...], wkv_ref, invf_ref[...], tab_ref,
                     ka_ref, vt_ref, km_ref, sub)
    else:
        o_ref, = refs[4:]
    o_ref[...] = y


def _ffn_call(x2, g, w_in_all, w_out_all, layer, idx, final_g=None, kv=None, nblk=None):
    t = x2.shape[0]
    mode = 'final' if final_g is not None else ('kv' if kv is not None else 'plain')
    row_spec = pl.BlockSpec((FFN_ROWS, D_MODEL), lambda i: (i, 0))
    pick = lambda i: (layer, idx, 0, 0)
    in_specs = [row_spec, _resident((1, D_MODEL)),
                pl.BlockSpec((None, None, D_MODEL, 2 * D_FF), pick, pipeline_mode=pl.Buffered(1)),
                pl.BlockSpec((None, None, D_FF, D_MODEL), pick, pipeline_mode=pl.Buffered(1))]
    args = [x2, g.reshape(1, D_MODEL), w_in_all, w_out_all]
    out_specs = row_spec
    out_shape = jax.ShapeDtypeStruct((t, D_MODEL), F32)
    if mode == 'final':
        in_specs.append(_resident((1, D_MODEL)))
        args.append(final_g.reshape(1, D_MODEL))
    elif mode == 'kv':
        kvg, wkv, invf, rope_tab = kv
        kdim = N_KV_HEADS * HEAD_DIM
        sub = FFN_ROWS // MOBA_BLOCK
        ntile = t // MOBA_BLOCK
        in_specs += [_resident((1, D_MODEL)), _resident((D_MODEL, 2 * kdim)),
                     _resident((1, HEAD_DIM)), _resident(rope_tab.shape)]
        args += [kvg.reshape(1, D_MODEL), wkv, invf, rope_tab]
        out_specs = [row_spec,
                     pl.BlockSpec((N_KV_HEADS, FFN_ROWS, 2 * HEAD_DIM), lambda i: (0, i, 0)),
                     pl.BlockSpec((sub, N_KV_HEADS, V_ROWS, MOBA_BLOCK), lambda i: (i, 0, 0, 0)),
                     pl.BlockSpec((sub, 1, kdim), lambda i: (i, 0, 0))]
        out_shape = [out_shape,
                     jax.ShapeDtypeStruct((N_KV_HEADS, t, 2 * HEAD_DIM), BF16),
                     jax.ShapeDtypeStruct((ntile, N_KV_HEADS, V_ROWS, MOBA_BLOCK), BF16),
                     jax.ShapeDtypeStruct((ntile, 1, kdim), F32)]
    return pl.pallas_call(
        functools.partial(_ffn_body, mode=mode, nblk=nblk),
        grid=(t // FFN_ROWS,),
        in_specs=in_specs,
        out_specs=out_specs,
        out_shape=out_shape,
        compiler_params=pltpu.CompilerParams(
            dimension_semantics=("arbitrary",), vmem_limit_bytes=VMEM_LIMIT),
        name="ffn_" + mode,
    )(*args)


def _s5_prep_body(lr_ref, li_ref, ls_ref, bre_ref, bim_ref, cim_ref,
                  ar_ref, ai_ref, amr_ref, ami_ref, pr_ref, pi_ref, btr_ref, bti_ref, cneg_ref):
    lr = lr_ref[...]
    li = li_ref[...]
    dt = jnp.exp(ls_ref[...])
    mag = jnp.exp(lr * dt)
    abar_re = mag * jnp.cos(li * dt)
    abar_im = mag * jnp.sin(li * dt)
    ar_ref[...] = abar_re
    ai_ref[...] = abar_im
    nr, ni = abar_re - 1.0, abar_im
    den = lr * lr + li * li
    coef_re = (nr * lr + ni * li) / den
    coef_im = (ni * lr - nr * li) / den
    k = (lax.broadcasted_iota(jnp.int32, (S5_STEPS, 1), 0) + 1).astype(F32)
    pmag = jnp.exp((lr * dt) * k)
    pang = (li * dt) * k
    pr = pmag * jnp.cos(pang)
    pi = pmag * jnp.sin(pang)
    for j in range(S5_LANE_BLOCKS):
        cols = slice(j * S5_HALF, (j + 1) * S5_HALF)
        for step in range(S5_STEPS):
            rows = slice(step * SUBLANES, (step + 1) * SUBLANES)
            pr_ref[j, rows, :] = jnp.broadcast_to(pr[step:step + 1, cols], (SUBLANES, S5_HALF))
            pi_ref[j, rows, :] = jnp.broadcast_to(pi[step:step + 1, cols], (SUBLANES, S5_HALF))
    amr_ref[...] = pr[S5_STEPS - 1:S5_STEPS, :]
    ami_ref[...] = pi[S5_STEPS - 1:S5_STEPS, :]
    bre = bre_ref[...]
    bim = bim_ref[...]
    btr_ref[...] = coef_re * bre - coef_im * bim
    bti_ref[...] = coef_re * bim + coef_im * bre
    cneg_ref[...] = -cim_ref[...]


def _s5_prep_call(a_re, a_im, log_step, b_re, b_im, c_im):
    n = S5_NSTATE
    row = lambda v: v.reshape(1, n)
    chan_major = lambda v: v.transpose(2, 0, 1).reshape(S5_GROUP, n)
    ls = jnp.repeat(log_step, S5_STATE)
    outs = pl.pallas_call(
        _s5_prep_body,
        out_shape=[jax.ShapeDtypeStruct((1, n), F32)] * 4
        + [jax.ShapeDtypeStruct((S5_LANE_BLOCKS, S5_CHUNK, S5_HALF), F32)] * 2
        + [jax.ShapeDtypeStruct((S5_GROUP, n), F32)] * 3,
        name="s5_prep",
    )(row(a_re), row(a_im), row(ls), chan_major(b_re), chan_major(b_im),
      c_im.transpose(1, 0, 2).reshape(S5_GROUP, n))
    return outs


def _s5_body(x_ref, xprev_ref, g_ref, perm_ref, permt_ref, bblk_ref, cblk_ref, ar_ref, ai_ref,
             amr_ref, ami_ref, pr_ref, pi_ref, d_ref, wglu_ref, o_ref,
             xs_ref, st_ref, c_ref, hb_ref, y_ref, *, nchunk):
    step = pl.program_id(0)

    @pl.when(step % nchunk == 0)
    def _():
        st_ref[...] = jnp.zeros(st_ref.shape, F32)

    @pl.when(step == 0)
    def _():
        y_ref[...] = jnp.zeros(y_ref.shape, F32)

    yp = y_ref[...]
    yp = 0.5 * yp * (1.0 + jnp.tanh(np.sqrt(2.0 / np.pi).astype(np.float32)
                                    * (yp + 0.044715 * (yp * yp * yp))))
    yn = _dot(permt_ref[...], yp.astype(BF16)).astype(BF16)

    def glu_piece(c):
        cols = slice(c * S5_GLU_COLS, (c + 1) * S5_GLU_COLS)
        gcols = slice(D_MODEL + c * S5_GLU_COLS, D_MODEL + (c + 1) * S5_GLU_COLS)
        val = _dot(yn, wglu_ref[:, cols])
        gate = _dot(yn, wglu_ref[:, gcols])
        o_ref[:, cols] = xprev_ref[:, cols] + val * jax.nn.sigmoid(gate)

    x = x_ref[...]
    u = _rms(x, g_ref[...])
    u_hi = u.astype(BF16)
    u_lo = (u - u_hi.astype(F32)).astype(BF16)
    perm = perm_ref[...]
    up_hi = _dot(perm, u_hi)
    up = up_hi + _dot(perm, u_lo)
    ub = up_hi.astype(BF16)

    last = SUBLANES * (S5_STEPS - 1)
    nb = S5_LANE_BLOCKS
    re, im = slice(0, S5_HALF), slice(S5_HALF, 2 * S5_HALF)

    def project_in(j):
        xs_ref[j] = _dot(ub[:, j * LANES:(j + 1) * LANES], bblk_ref[j])

    for j in range(nb):
        project_in(j)

    def scan_steps(j):
        a_r = jnp.broadcast_to(ar_ref[j], (SUBLANES, S5_HALF))
        a_i = jnp.broadcast_to(ai_ref[j], (SUBLANES, S5_HALF))
        state = [xs_ref[j, 0:SUBLANES, re], xs_ref[j, 0:SUBLANES, im]]

        def step(k):
            rows = slice(k * SUBLANES, (k + 1) * SUBLANES)
            h_r, h_i = state
            state[0] = a_r * h_r - a_i * h_i + xs_ref[j, rows, re]
            state[1] = a_r * h_i + a_i * h_r + xs_ref[j, rows, im]
            xs_ref[j, rows, re] = state[0]
            xs_ref[j, rows, im] = state[1]

        return [functools.partial(step, k) for k in range(1, S5_STEPS)]

    def entering_states(j):
        am_r, am_i = amr_ref[j], ami_ref[j]
        c_r, c_i = st_ref[j, :, re], st_ref[j, :, im]
        for i in range(S5_SUBSEQ):
            c_ref[j, i:i + 1, re] = c_r
            c_ref[j, i:i + 1, im] = c_i
            e_r = xs_ref[j, last + i:last + i + 1, re]
            e_i = xs_ref[j, last + i:last + i + 1, im]
            c_r, c_i = am_r * c_r - am_i * c_i + e_r, am_r * c_i + am_i * c_r + e_i
        st_ref[j, :, re] = c_r
        st_ref[j, :, im] = c_i

    def fix_steps(j):
        cc_r = jnp.concatenate([c_ref[j, :, re]] * 2, axis=0)
        cc_i = jnp.concatenate([c_ref[j, :, im]] * 2, axis=0)

        def step(k):
            rows = slice(2 * k * SUBLANES, 2 * (k + 1) * SUBLANES)
            p_r, p_i = pr_ref[j, rows, :], pi_ref[j, rows, :]
            t_r = xs_ref[j, rows, re] + (p_r * cc_r - p_i * cc_i)
            t_i = xs_ref[j, rows, im] + (p_r * cc_i + p_i * cc_r)
            hb_ref[j, rows, :] = jnp.concatenate([t_r, t_i], axis=1).astype(BF16)

        return [functools.partial(step, k) for k in range(S5_STEPS // 2)]

    ys = []
    for j in range(nb + 1):
        if j % 2 == 0 and j < nb:
            glu_piece(j // 2)
        p1 = scan_steps(j) if j < nb else []
        p2 = fix_steps(j - 1) if j >= 1 else []
        for k in range(max(len(p2), (len(p1) + 1) // 2)):
            for f in p1[2 * k:2 * k + 2]:
                f()
            if k < len(p2):
                p2[k]()
        if j < nb:
            entering_states(j)
        if j >= 1:
            ys.append(_dot(hb_ref[j - 1], cblk_ref[j - 1]))

    y_ref[...] = jnp.concatenate(ys, axis=1) + d_ref[...] * up


def _s5_call(x2, bsz, seq, g, prep, c_re, d_skip, w_glu):
    ar, ai, amr, ami, pr, pi, btr, bti, cneg = prep
    nb, gb, ns = S5_LANE_BLOCKS, S5_GROUPS_PER_BLOCK, S5_STATE
    eye = jnp.eye(gb, dtype=F32)

    def per_block(v, rows):
        return v.reshape(rows, nb, S5_HALF).transpose(1, 0, 2)

    bt = jnp.stack([btr, bti], axis=1).reshape(S5_GROUP, 2, nb, gb, ns)
    bblk = jnp.einsum('crjgp,gh->jgcrhp', bt, eye).reshape(nb, LANES, 2 * S5_HALF).astype(BF16)
    cmat = jnp.stack([c_re.transpose(1, 0, 2).reshape(S5_GROUP, S5_NSTATE), cneg], axis=1)
    cmat = cmat.reshape(S5_GROUP, 2, nb, gb, ns)
    cblk = jnp.einsum('crjgp,gh->jrgphc', cmat, eye).reshape(nb, 2 * S5_HALF, LANES).astype(BF16)

    r = np.arange(S5_CHUNK)
    perm_np = np.zeros((S5_CHUNK, S5_CHUNK), np.float32)
    perm_np[r, (r % SUBLANES) * S5_STEPS + r // SUBLANES] = 1.0
    perm = jnp.asarray(perm_np, BF16)
    permt = jnp.asarray(perm_np.T, BF16)

    nchunk = seq // S5_CHUNK
    total = bsz * nchunk
    cur_spec = pl.BlockSpec((S5_CHUNK, D_MODEL), lambda s: (jnp.minimum(s, total - 1), 0))
    prev_spec = pl.BlockSpec((S5_CHUNK, D_MODEL), lambda s: (jnp.maximum(s - 1, 0), 0))
    in_specs = [
        cur_spec, prev_spec, _resident((1, D_MODEL)),
        _resident((S5_CHUNK, S5_CHUNK)), _resident((S5_CHUNK, S5_CHUNK)),
        _resident((nb, LANES, 2 * S5_HALF)), _resident((nb, 2 * S5_HALF, LANES)),
        _resident((nb, 1, S5_HALF)), _resident((nb, 1, S5_HALF)),
        _resident((nb, 1, S5_HALF)), _resident((nb, 1, S5_HALF)),
        _resident((nb, S5_CHUNK, S5_HALF)), _resident((nb, S5_CHUNK, S5_HALF)),
        _resident((1, D_MODEL)), _resident((D_MODEL, 2 * D_MODEL)),
    ]
    return pl.pallas_call(
        functools.partial(_s5_body, nchunk=nchunk),
        grid=(total + 1,),
        in_specs=in_specs,
        out_specs=prev_spec,
        out_shape=jax.ShapeDtypeStruct(x2.shape, F32),
        scratch_shapes=[
            pltpu.VMEM((nb, S5_CHUNK, 2 * S5_HALF), F32),
            pltpu.VMEM((nb, 1, 2 * S5_HALF), F32),
            pltpu.VMEM((nb, S5_SUBSEQ, 2 * S5_HALF), F32),
            pltpu.VMEM((nb, S5_CHUNK, 2 * S5_HALF), BF16),
            pltpu.VMEM((S5_CHUNK, D_MODEL), F32),
        ],
        compiler_params=pltpu.CompilerParams(
            dimension_semantics=("arbitrary",), vmem_limit_bytes=VMEM_LIMIT),
        name="s5",
    )(x2, x2, g.reshape(1, D_MODEL), perm, permt, bblk, cblk,
      per_block(ar, 1), per_block(ai, 1), per_block(amr, 1), per_block(ami, 1),
      pr, pi,
      d_skip.reshape(1, D_MODEL), w_glu.astype(BF16))


def _rope_tables(pos0, invf, tab_ref):
    ang0 = pos0.astype(F32) * invf
    c0, s0 = jnp.cos(ang0), jnp.sin(ang0)
    cos_t = c0 * tab_ref[0] - s0 * tab_ref[1]
    sin_t = s0 * tab_ref[2] + c0 * tab_ref[3]
    return cos_t, sin_t


def _rope_head(xh, cos_t, sin_t, low_half):
    half = ROPE_DIM // 2
    swapped = jnp.where(low_half, pltpu.roll(xh, LANES - half, axis=1), pltpu.roll(xh, half, axis=1))
    return xh * cos_t + swapped * sin_t


def _rope_consts():
    half = ROPE_DIM // 2
    inv_freq = ROPE_THETA ** (-jnp.arange(0, ROPE_DIM, 2, dtype=F32) / ROPE_DIM)
    pad = jnp.zeros((HEAD_DIM - ROPE_DIM,), F32)
    invf = jnp.concatenate([inv_freq, inv_freq, pad]).reshape(1, HEAD_DIM)
    sign = jnp.concatenate([-jnp.ones((half,), F32), jnp.ones((half,), F32), pad]).reshape(1, HEAD_DIM)
    ang_r = jnp.arange(MOBA_BLOCK, dtype=F32)[:, None] * invf
    cos_r, sin_r = jnp.cos(ang_r), jnp.sin(ang_r)
    tables = jnp.stack([cos_r, sin_r, cos_r * sign, sin_r * sign])
    return invf, tables


def _kv_tile(x, blk, g, wkv_ref, invf, tab_ref, ka_ref, vt_ref, km_ref, sub):
    rows = slice(sub * MOBA_BLOCK, (sub + 1) * MOBA_BLOCK)
    h = _rms(x, g).astype(BF16)
    kv = _dot(h, wkv_ref[...])
    kdim = N_KV_HEADS * HEAD_DIM
    cos_t, sin_t = _rope_tables(blk * MOBA_BLOCK, invf, tab_ref)
    lane = lax.broadcasted_iota(jnp.int32, (MOBA_BLOCK, HEAD_DIM), 1)
    low_half = lane < ROPE_DIM // 2
    onehot = jnp.where(lane == blk, 1.0, 0.0).astype(BF16)
    pad_row = lax.broadcasted_iota(jnp.int32, (V_ROWS - HEAD_DIM, MOBA_BLOCK), 0)
    ones_rows = jnp.where(pad_row == 0, 1.0, 0.0).astype(BF16)
    means = []
    for hh in range(N_KV_HEADS):
        kh = _rope_head(kv[:, hh * HEAD_DIM:(hh + 1) * HEAD_DIM], cos_t, sin_t, low_half)
        means.append(jnp.mean(kh, axis=0, keepdims=True))
        ka_ref[hh, rows, :] = jnp.concatenate([kh.astype(BF16), onehot], axis=1)
        vt = kv[:, kdim + hh * HEAD_DIM:kdim + (hh + 1) * HEAD_DIM].T.astype(BF16)
        vt_ref[sub, hh] = jnp.concatenate([vt, ones_rows], axis=0)
    km_ref[sub] = jnp.concatenate(means, axis=1)


def _split_bf16(v):
    hi = v.astype(BF16)
    return hi, (v - hi.astype(F32)).astype(BF16)


def _attn_body(x_ref, g_ref, wq_ref, wo_ref, ka_ref, vt_ref, km_ref, invf_ref, o_ref,
               qa_ref, acc_ref, s0_ref, s1_ref, pb_ref, *, nblk):
    own = pl.program_id(1)
    items = KV_GROUP * MOBA_BLOCK
    group_keys = ATT_GROUP * MOBA_BLOCK
    half = ROPE_DIM // 2
    x = x_ref[0]
    h = _rms(x, g_ref[...]).astype(BF16)
    q = _dot(h, wq_ref[...])
    pos = (own * MOBA_BLOCK + lax.broadcasted_iota(jnp.int32, (1, MOBA_BLOCK), 1)).astype(F32)
    ang = invf_ref[...] * pos
    cos_t, sin_t = jnp.cos(ang), jnp.sin(ang)

    def head_t(i):
        qt = q[:, i * HEAD_DIM:(i + 1) * HEAD_DIM].T
        x1, x2 = qt[0:half], qt[half:2 * half]
        rot = jnp.concatenate([x1 * cos_t - x2 * sin_t, x2 * cos_t + x1 * sin_t, qt[2 * half:]], axis=0)
        return rot * (HEAD_DIM ** -0.5)

    blk = lax.broadcasted_iota(jnp.int32, (nblk, items), 0)
    blk_f = blk.astype(F32)
    past = blk < own
    causal = (lax.broadcasted_iota(jnp.int32, (MOBA_BLOCK, items), 0)
              <= lax.broadcasted_iota(jnp.int32, (MOBA_BLOCK, items), 1) % MOBA_BLOCK)
    feat_pad = jnp.zeros((HEAD_DIM - nblk, items), BF16)

    kv_heads = range(N_KV_HEADS)
    for kh in kv_heads:
        qt = jnp.concatenate([head_t(kh * KV_GROUP + i) for i in range(KV_GROUP)], axis=1)
        q_hi, q_lo = _split_bf16(qt)
        k_hi, k_lo = _split_bf16(km_ref[0, kh])
        g_hi = _dot(jnp.concatenate([k_hi, k_lo], axis=0), q_hi)
        gate = g_hi[0:nblk] + (_dot(k_hi, q_lo) + g_hi[nblk:2 * nblk])
        cur = jnp.where(past, gate, -jnp.inf)
        bias = jnp.where(blk == own, 0.0, NEG_INF)
        for _ in range(MOBA_TOPK):
            best = jnp.max(cur, axis=0, keepdims=True)
            cand = jnp.where((cur == best) & (best > -jnp.inf), blk_f, float(nblk))
            pick = blk_f == jnp.min(cand, axis=0, keepdims=True)
            bias = jnp.where(pick, 0.0, bias)
            cur = jnp.where(pick, -jnp.inf, cur)
        q_feat = (qt * LOG2E).astype(BF16)
        qa_ref[kh] = jnp.concatenate([q_feat, bias.astype(BF16), feat_pad], axis=0)
        acc_ref[kh] = jnp.zeros(acc_ref.shape[1:], F32)

    def score_group(kh, gi):
        keys = pl.ds(pl.multiple_of(gi * group_keys, group_keys), group_keys)
        return _dot(ka_ref[kh, 0, keys, :], qa_ref[kh])

    def weights(kh, s_cur_ref, m_prev):
        s = s_cur_ref[kh]
        m_new = jnp.maximum(m_prev, jnp.max(s, axis=0, keepdims=True))
        pb_ref[kh] = jnp.exp2((s - m_new).astype(BF16))
        return m_new, jnp.exp2(m_prev - m_new)

    def accumulate(kh, gi, alpha):
        n0 = gi * ATT_GROUP
        pv = _dot(vt_ref[0, n0, kh], pb_ref[kh, 0:MOBA_BLOCK, :])
        for j in range(1, ATT_GROUP):
            pv = pv + _dot(vt_ref[0, n0 + j, kh], pb_ref[kh, j * MOBA_BLOCK:(j + 1) * MOBA_BLOCK, :])
        acc_ref[kh] = alpha * acc_ref[kh] + pv

    own_group = own // ATT_GROUP

    def mask_own(kh, s_cur_ref):
        slot = pl.ds(pl.multiple_of((own % ATT_GROUP) * MOBA_BLOCK, MOBA_BLOCK), MOBA_BLOCK)
        s_cur_ref[kh, slot, :] = jnp.where(causal, s_cur_ref[kh, slot, :], NEG_INF)

    def softmax_steps(s_cur_ref, gi, ms, before=(None,) * N_KV_HEADS, tail=None, own_slot=False):
        out = []
        alpha_prev = None
        for kh in kv_heads:
            if before[kh] is not None:
                before[kh]()
            if alpha_prev is not None:
                accumulate(kh - 1, gi, alpha_prev)
            if own_slot:
                mask_own(kh, s_cur_ref)
            m_new, alpha_prev = weights(kh, s_cur_ref, ms[kh])
            out.append(m_new)
        if tail is not None:
            tail()
        accumulate(N_KV_HEADS - 1, gi, alpha_prev)
        return tuple(out)

    def scorer(dst_ref, kh, gi):
        def run():
            dst_ref[kh] = score_group(kh, gi)
        return run

    m0 = jnp.full((1, items), SCORE_FLOOR, F32)
    for kh in kv_heads:
        s0_ref[kh] = score_group(kh, 0)

    def group_pair(pi, ms):
        g = 2 * pi
        nxt = [scorer(s0_ref, kh, g + 2) for kh in kv_heads]
        ms = softmax_steps(s0_ref, g, ms, before=[scorer(s1_ref, kh, g + 1) for kh in kv_heads],
                           tail=nxt[0])
        return softmax_steps(s1_ref, g + 1, ms, before=[None] + nxt[1:])

    ms = lax.fori_loop(0, own_group // 2, group_pair, (m0,) * N_KV_HEADS)

    @pl.when(own_group % 2 == 0)
    def _():
        softmax_steps(s0_ref, own_group, ms, own_slot=True)

    @pl.when(own_group % 2 == 1)
    def _():
        mids = softmax_steps(s0_ref, own_group - 1, ms,
                             before=[scorer(s1_ref, kh, own_group) for kh in kv_heads])
        softmax_steps(s1_ref, own_group, mids, own_slot=True)

    outs = []
    for kh in kv_heads:
        ot = acc_ref[kh, 0:HEAD_DIM, :] / acc_ref[kh, HEAD_DIM:HEAD_DIM + 1, :]
        outs.extend(ot[:, i * MOBA_BLOCK:(i + 1) * MOBA_BLOCK].T for i in range(KV_GROUP))

    attn = jnp.concatenate(outs, axis=1).astype(BF16)
    o_ref[0] = x + _dot(attn, wo_ref[...])


def _attn_call(x3, g, w_q, w_o, ka, vt, km, invf_col):
    bsz, seq, _ = x3.shape
    nblk = seq // MOBA_BLOCK
    items = KV_GROUP * MOBA_BLOCK
    x_spec = pl.BlockSpec((1, MOBA_BLOCK, D_MODEL), lambda b, i: (b, i, 0))
    return pl.pallas_call(
        functools.partial(_attn_body, nblk=nblk),
        grid=(bsz, nblk),
        in_specs=[x_spec, _resident((1, D_MODEL)),
                  _resident((D_MODEL, D_MODEL)), _resident((D_MODEL, D_MODEL)),
                  pl.BlockSpec((N_KV_HEADS, 1, seq, 2 * HEAD_DIM), lambda b, i: (0, b, 0, 0),
                               pipeline_mode=pl.Buffered(1)),
                  pl.BlockSpec((1, nblk, N_KV_HEADS, V_ROWS, MOBA_BLOCK),
                               lambda b, i: (b, 0, 0, 0, 0), pipeline_mode=pl.Buffered(1)),
                  pl.BlockSpec((1, N_KV_HEADS, nblk, HEAD_DIM), lambda b, i: (b, 0, 0, 0)),
                  _resident((ROPE_DIM // 2, 1))],
        out_specs=x_spec,
        out_shape=jax.ShapeDtypeStruct(x3.shape, F32),
        scratch_shapes=[
            pltpu.VMEM((N_KV_HEADS, 2 * HEAD_DIM, items), BF16),
            pltpu.VMEM((N_KV_HEADS, V_ROWS, items), F32),
            pltpu.VMEM((N_KV_HEADS, ATT_GROUP * MOBA_BLOCK, items), F32),
            pltpu.VMEM((N_KV_HEADS, ATT_GROUP * MOBA_BLOCK, items), F32),
            pltpu.VMEM((N_KV_HEADS, ATT_GROUP * MOBA_BLOCK, items), BF16)],
        compiler_params=pltpu.CompilerParams(
            dimension_semantics=("arbitrary", "arbitrary"), vmem_limit_bytes=VMEM_LIMIT),
        name="moba_attn",
    )(x3, g.reshape(1, D_MODEL), w_q.astype(BF16), w_o.astype(BF16), ka, vt, km, invf_col)


def kernel(x, norm_g, ffn_w_in, ffn_w_out, s5_a_re, s5_a_im, s5_log_step, s5_b_re, s5_b_im,
           s5_c_re, s5_c_im, s5_d, s5_w_glu, kv_norm_g, w_k, w_v, w_q, w_o, final_g):
    bsz, seq, _ = x.shape
    assert seq % S5_CHUNK == 0 and seq % MOBA_BLOCK == 0
    assert (seq // MOBA_BLOCK) % (2 * SUBLANES) == 0 and seq // MOBA_BLOCK <= HEAD_DIM
    assert (seq // MOBA_BLOCK) % ATT_GROUP == 0
    assert (bsz * seq) % FFN_ROWS == 0 and FFN_ROWS % MOBA_BLOCK == 0
    nblk = seq // MOBA_BLOCK
    x2 = x.reshape(bsz * seq, D_MODEL)
    invf, rope_tab = _rope_consts()

    w_in_b = ffn_w_in
    w_out_b = ffn_w_out
    wkv = jnp.concatenate([w_k, w_v], axis=1).astype(BF16)

    x2 = _ffn_call(x2, norm_g[0, 0], w_in_b, w_out_b, 0, 0)
    prep = _s5_prep_call(s5_a_re[0], s5_a_im[0], s5_log_step[0], s5_b_re[0], s5_b_im[0], s5_c_im[0])
    x2 = _s5_call(x2, bsz, seq, norm_g[0, 1], prep, s5_c_re[0], s5_d[0], s5_w_glu[0])
    x2, ka, vt, km = _ffn_call(x2, norm_g[0, 2], w_in_b, w_out_b, 0, 1,
                               kv=(kv_norm_g, wkv, invf, rope_tab), nblk=nblk)
    ka = ka.reshape(N_KV_HEADS, bsz, seq, 2 * HEAD_DIM)
    vt = vt.reshape(bsz, nblk, N_KV_HEADS, V_ROWS, MOBA_BLOCK)
    km = km.reshape(bsz, nblk, N_KV_HEADS, HEAD_DIM).transpose(0, 2, 1, 3)

    x2 = _ffn_call(x2, norm_g[1, 0], w_in_b, w_out_b, 1, 0)
    x3 = _attn_call(x2.reshape(bsz, seq, D_MODEL), norm_g[1, 1], w_q[0], w_o[0], ka, vt, km,
                    invf[0, 0:ROPE_DIM // 2].reshape(ROPE_DIM // 2, 1))
    x2 = _ffn_call(x3.reshape(bsz * seq, D_MODEL), norm_g[1, 2], w_in_b, w_out_b, 1, 1,
                   final_g=final_g)
    return x2.reshape(bsz, seq, D_MODEL)
```

```python
import functools

import jax
import jax.numpy as jnp
import numpy as np
from jax import lax
from jax.experimental import pallas as pl
from jax.experimental.pallas import tpu as pltpu

F32 = jnp.float32
BF16 = jnp.bfloat16

D_MODEL = 1024
D_FF = 2816
RMS_EPS = 1e-6
S5_GROUP = 16
S5_GROUPS = D_MODEL // S5_GROUP
S5_STATE = 64
N_HEADS = 8
HEAD_DIM = 128
N_KV_HEADS = 2
KV_GROUP = N_HEADS // N_KV_HEADS
ROPE_DIM = HEAD_DIM // 4
ROPE_THETA = 500000.0
MOBA_BLOCK = 256
MOBA_TOPK = 3
NEG_INF = -1e30
SCORE_FLOOR = 0.5 * NEG_INF
LOG2E = 1.4426950408889634

LANES = 128
SUBLANES = 8
VMEM_LIMIT = 56 * 1024 * 1024

FFN_ROWS = 512
FFN_COLS = 256

ATT_GROUP = 4
V_ROWS = HEAD_DIM + 2 * SUBLANES

S5_CHUNK = 256
S5_SUBSEQ = SUBLANES
S5_STEPS = S5_CHUNK // S5_SUBSEQ
S5_LANE_BLOCKS = D_MODEL // LANES
S5_GROUPS_PER_BLOCK = LANES // S5_GROUP
S5_HALF = S5_GROUPS_PER_BLOCK * S5_STATE
S5_NSTATE = S5_GROUPS * S5_STATE
S5_GLU_COLS = 2 * D_MODEL // S5_LANE_BLOCKS

def _rms(x, g):
    ms = jnp.mean(x * x, axis=-1, keepdims=True)
    return (x * lax.rsqrt(ms + RMS_EPS)) * g


def _dot(a, b):
    return jnp.dot(a, b, preferred_element_type=F32)


def _resident(shape):
    nd = len(shape)
    return pl.BlockSpec(shape, lambda *_: (0,) * nd, pipeline_mode=pl.Buffered(1))


def _ffn_body(*refs, mode, nblk):
    x_ref, g_ref, win_ref, wout_ref = refs[:4]
    x = x_ref[...]
    h = _rms(x, g_ref[...]).astype(BF16)
    acc = jnp.zeros(x.shape, F32)
    for c in range(D_FF // FFN_COLS):
        lo = c * FFN_COLS
        gate = _dot(h, win_ref[:, lo:lo + FFN_COLS].astype(BF16))
        up = _dot(h, win_ref[:, D_FF + lo:D_FF + lo + FFN_COLS].astype(BF16))
        act = (gate * jax.nn.sigmoid(gate)) * up
        acc = acc + _dot(act.astype(BF16), wout_ref[lo:lo + FFN_COLS, :].astype(BF16))
    y = x + 0.5 * acc
    if mode == 'final':
        fg_ref, o_ref = refs[4:]
        y = _rms(y, fg_ref[...])
    elif mode == 'kv':
        kvg_ref, wkv_ref, invf_ref, tab_ref, o_ref, ka_ref, vt_ref, km_ref = refs[4:]
        for sub in range(FFN_ROWS // MOBA_BLOCK):
            rows = slice(sub * MOBA_BLOCK, (sub + 1) * MOBA_BLOCK)
            blk = (pl.program_id(0) * (FFN_ROWS // MOBA_BLOCK) + sub) % nblk
            _kv_tile(y[rows], blk, kvg_ref[...], wkv_ref, invf_ref[...], tab_ref,
                     ka_ref, vt_ref, km_ref, sub)
    else:
        o_ref, = refs[4:]
    o_ref[...] = y


def _ffn_call(x2, g, w_in_all, w_out_all, layer, idx, final_g=None, kv=None, nblk=None):
    t = x2.shape[0]
    mode = 'final' if final_g is not None else ('kv' if kv is not None else 'plain')
    row_spec = pl.BlockSpec((FFN_ROWS, D_MODEL), lambda i: (i, 0))
    pick = lambda i: (layer, idx, 0, 0)
    in_specs = [row_spec, _resident((1, D_MODEL)),
                pl.BlockSpec((None, None, D_MODEL, 2 * D_FF), pick, pipeline_mode=pl.Buffered(1)),
                pl.BlockSpec((None, None, D_FF, D_MODEL), pick, pipeline_mode=pl.Buffered(1))]
    args = [x2, g.reshape(1, D_MODEL), w_in_all, w_out_all]
    out_specs = row_spec
    out_shape = jax.ShapeDtypeStruct((t, D_MODEL), F32)
    if mode == 'final':
        in_specs.append(_resident((1, D_MODEL)))
        args.append(final_g.reshape(1, D_MODEL))
    elif mode == 'kv':
        kvg, wkv, invf, rope_tab = kv
        kdim = N_KV_HEADS * HEAD_DIM
        sub = FFN_ROWS // MOBA_BLOCK
        ntile = t // MOBA_BLOCK
        in_specs += [_resident((1, D_MODEL)), _resident((D_MODEL, 2 * kdim)),
                     _resident((1, HEAD_DIM)), _resident(rope_tab.shape)]
        args += [kvg.reshape(1, D_MODEL), wkv, invf, rope_tab]
        out_specs = [row_spec,
                     pl.BlockSpec((N_KV_HEADS, FFN_ROWS, 2 * HEAD_DIM), lambda i: (0, i, 0)),
                     pl.BlockSpec((sub, N_KV_HEADS, V_ROWS, MOBA_BLOCK), lambda i: (i, 0, 0, 0)),
                     pl.BlockSpec((sub, 1, kdim), lambda i: (i, 0, 0))]
        out_shape = [out_shape,
                     jax.ShapeDtypeStruct((N_KV_HEADS, t, 2 * HEAD_DIM), BF16),
                     jax.ShapeDtypeStruct((ntile, N_KV_HEADS, V_ROWS, MOBA_BLOCK), BF16),
                     jax.ShapeDtypeStruct((ntile, 1, kdim), F32)]
    return pl.pallas_call(
        functools.partial(_ffn_body, mode=mode, nblk=nblk),
        grid=(t // FFN_ROWS,),
        in_specs=in_specs,
        out_specs=out_specs,
        out_shape=out_shape,
        compiler_params=pltpu.CompilerParams(
            dimension_semantics=("arbitrary",), vmem_limit_bytes=VMEM_LIMIT),
        name="ffn_" + mode,
    )(*args)


def _s5_prep_body(lr_ref, li_ref, ls_ref, bre_ref, bim_ref, cim_ref,
                  ar_ref, ai_ref, amr_ref, ami_ref, pr_ref, pi_ref, btr_ref, bti_ref, cneg_ref):
    lr = lr_ref[...]
    li = li_ref[...]
    dt = jnp.exp(ls_ref[...])
    mag = jnp.exp(lr * dt)
    abar_re = mag * jnp.cos(li * dt)
    abar_im = mag * jnp.sin(li * dt)
    ar_ref[...] = abar_re
    ai_ref[...] = abar_im
    nr, ni = abar_re - 1.0, abar_im
    den = lr * lr + li * li
    coef_re = (nr * lr + ni * li) / den
    coef_im = (ni * lr - nr * li) / den
    k = (lax.broadcasted_iota(jnp.int32, (S5_STEPS, 1), 0) + 1).astype(F32)
    pmag = jnp.exp((lr * dt) * k)
    pang = (li * dt) * k
    pr = pmag * jnp.cos(pang)
    pi = pmag * jnp.sin(pang)
    for j in range(S5_LANE_BLOCKS):
        cols = slice(j * S5_HALF, (j + 1) * S5_HALF)
        for step in range(S5_STEPS):
            rows = slice(step * SUBLANES, (step + 1) * SUBLANES)
            pr_ref[j, rows, :] = jnp.broadcast_to(pr[step:step + 1, cols], (SUBLANES, S5_HALF))
            pi_ref[j, rows, :] = jnp.broadcast_to(pi[step:step + 1, cols], (SUBLANES, S5_HALF))
    amr_ref[...] = pr[S5_STEPS - 1:S5_STEPS, :]
    ami_ref[...] = pi[S5_STEPS - 1:S5_STEPS, :]
    bre = bre_ref[...]
    bim = bim_ref[...]
    btr_ref[...] = coef_re * bre - coef_im * bim
    bti_ref[...] = coef_re * bim + coef_im * bre
    cneg_ref[...] = -cim_ref[...]


def _s5_prep_call(a_re, a_im, log_step, b_re, b_im, c_im):
    n = S5_NSTATE
    row = lambda v: v.reshape(1, n)
    chan_major = lambda v: v.transpose(2, 0, 1).reshape(S5_GROUP, n)
    ls = jnp.repeat(log_step, S5_STATE)
    outs = pl.pallas_call(
        _s5_prep_body,
        out_shape=[jax.ShapeDtypeStruct((1, n), F32)] * 4
        + [jax.ShapeDtypeStruct((S5_LANE_BLOCKS, S5_CHUNK, S5_HALF), F32)] * 2
        + [jax.ShapeDtypeStruct((S5_GROUP, n), F32)] * 3,
        name="s5_prep",
    )(row(a_re), row(a_im), row(ls), chan_major(b_re), chan_major(b_im),
      c_im.transpose(1, 0, 2).reshape(S5_GROUP, n))
    return outs


def _s5_body(x_ref, xprev_ref, g_ref, perm_ref, permt_ref, bblk_ref, cblk_ref, ar_ref, ai_ref,
             amr_ref, ami_ref, pr_ref, pi_ref, d_ref, wglu_ref, o_ref,
             xs_ref, st_ref, c_ref, hb_ref, y_ref, *, nchunk):
    step = pl.program_id(0)

    @pl.when(step % nchunk == 0)
    def _():
        st_ref[...] = jnp.zeros(st_ref.shape, F32)

    @pl.when(step == 0)
    def _():
        y_ref[...] = jnp.zeros(y_ref.shape, F32)

    yp = y_ref[...]
    yp = 0.5 * yp * (1.0 + jnp.tanh(np.sqrt(2.0 / np.pi).astype(np.float32)
                                    * (yp + 0.044715 * (yp * yp * yp))))
    yn = _dot(permt_ref[...], yp.astype(BF16)).astype(BF16)

    def glu_piece(c):
        cols = slice(c * S5_GLU_COLS, (c + 1) * S5_GLU_COLS)
        gcols = slice(D_MODEL + c * S5_GLU_COLS, D_MODEL + (c + 1) * S5_GLU_COLS)
        val = _dot(yn, wglu_ref[:, cols])
        gate = _dot(yn, wglu_ref[:, gcols])
        o_ref[:, cols] = xprev_ref[:, cols] + val * jax.nn.sigmoid(gate)

    x = x_ref[...]
    u = _rms(x, g_ref[...])
    u_hi = u.astype(BF16)
    u_lo = (u - u_hi.astype(F32)).astype(BF16)
    perm = perm_ref[...]
    up_hi = _dot(perm, u_hi)
    up = up_hi + _dot(perm, u_lo)
    ub = up_hi.astype(BF16)

    last = SUBLANES * (S5_STEPS - 1)
    nb = S5_LANE_BLOCKS
    re, im = slice(0, S5_HALF), slice(S5_HALF, 2 * S5_HALF)

    def project_in(j):
        xs_ref[j] = _dot(ub[:, j * LANES:(j + 1) * LANES], bblk_ref[j])

    for j in range(nb):
        project_in(j)

    def scan_steps(j):
        a_r = jnp.broadcast_to(ar_ref[j], (SUBLANES, S5_HALF))
        a_i = jnp.broadcast_to(ai_ref[j], (SUBLANES, S5_HALF))
        state = [xs_ref[j, 0:SUBLANES, re], xs_ref[j, 0:SUBLANES, im]]

        def step(k):
            rows = slice(k * SUBLANES, (k + 1) * SUBLANES)
            h_r, h_i = state
            state[0] = a_r * h_r - a_i * h_i + xs_ref[j, rows, re]
            state[1] = a_r * h_i + a_i * h_r + xs_ref[j, rows, im]
            xs_ref[j, rows, re] = state[0]
            xs_ref[j, rows, im] = state[1]

        return [functools.partial(step, k) for k in range(1, S5_STEPS)]

    def entering_states(j):
        am_r, am_i = amr_ref[j], ami_ref[j]
        c_r, c_i = st_ref[j, :, re], st_ref[j, :, im]
        for i in range(S5_SUBSEQ):
            c_ref[j, i:i + 1, re] = c_r
            c_ref[j, i:i + 1, im] = c_i
            e_r = xs_ref[j, last + i:last + i + 1, re]
            e_i = xs_ref[j, last + i:last + i + 1, im]
            c_r, c_i = am_r * c_r - am_i * c_i + e_r, am_r * c_i + am_i * c_r + e_i
        st_ref[j, :, re] = c_r
        st_ref[j, :, im] = c_i

    def fix_steps(j):
        cc_r = jnp.concatenate([c_ref[j, :, re]] * 2, axis=0)
        cc_i = jnp.concatenate([c_ref[j, :, im]] * 2, axis=0)

        def step(k):
            rows = slice(2 * k * SUBLANES, 2 * (k + 1) * SUBLANES)
            p_r, p_i = pr_ref[j, rows, :], pi_ref[j, rows, :]
            t_r = xs_ref[j, rows, re] + (p_r * cc_r - p_i * cc_i)
            t_i = xs_ref[j, rows, im] + (p_r * cc_i + p_i * cc_r)
            hb_ref[j, rows, :] = jnp.concatenate([t_r, t_i], axis=1).astype(BF16)

        return [functools.partial(step, k) for k in range(S5_STEPS // 2)]

    ys = []
    for j in range(nb + 1):
        if j % 2 == 0 and j < nb:
            glu_piece(j // 2)
        p1 = scan_steps(j) if j < nb else []
        p2 = fix_steps(j - 1) if j >= 1 else []
        for k in range(max(len(p2), (len(p1) + 1) // 2)):
            for f in p1[2 * k:2 * k + 2]:
                f()
            if k < len(p2):
                p2[k]()
        if j < nb:
            entering_states(j)
        if j >= 1:
            ys.append(_dot(hb_ref[j - 1], cblk_ref[j - 1]))

    y_ref[...] = jnp.concatenate(ys, axis=1) + d_ref[...] * up


def _s5_call(x2, bsz, seq, g, prep, c_re, d_skip, w_glu):
    ar, ai, amr, ami, pr, pi, btr, bti, cneg = prep
    nb, gb, ns = S5_LANE_BLOCKS, S5_GROUPS_PER_BLOCK, S5_STATE
    eye = jnp.eye(gb, dtype=F32)

    def per_block(v, rows):
        return v.reshape(rows, nb, S5_HALF).transpose(1, 0, 2)

    bt = jnp.stack([btr, bti], axis=1).reshape(S5_GROUP, 2, nb, gb, ns)
    bblk = jnp.einsum('crjgp,gh->jgcrhp', bt, eye).reshape(nb, LANES, 2 * S5_HALF).astype(BF16)
    cmat = jnp.stack([c_re.transpose(1, 0, 2).reshape(S5_GROUP, S5_NSTATE), cneg], axis=1)
    cmat = cmat.reshape(S5_GROUP, 2, nb, gb, ns)
    cblk = jnp.einsum('crjgp,gh->jrgphc', cmat, eye).reshape(nb, 2 * S5_HALF, LANES).astype(BF16)

    r = np.arange(S5_CHUNK)
    perm_np = np.zeros((S5_CHUNK, S5_CHUNK), np.float32)
    perm_np[r, (r % SUBLANES) * S5_STEPS + r // SUBLANES] = 1.0
    perm = jnp.asarray(perm_np, BF16)
    permt = jnp.asarray(perm_np.T, BF16)

    nchunk = seq // S5_CHUNK
    total = bsz * nchunk
    cur_spec = pl.BlockSpec((S5_CHUNK, D_MODEL), lambda s: (jnp.minimum(s, total - 1), 0))
    prev_spec = pl.BlockSpec((S5_CHUNK, D_MODEL), lambda s: (jnp.maximum(s - 1, 0), 0))
    in_specs = [
        cur_spec, prev_spec, _resident((1, D_MODEL)),
        _resident((S5_CHUNK, S5_CHUNK)), _resident((S5_CHUNK, S5_CHUNK)),
        _resident((nb, LANES, 2 * S5_HALF)), _resident((nb, 2 * S5_HALF, LANES)),
        _resident((nb, 1, S5_HALF)), _resident((nb, 1, S5_HALF)),
        _resident((nb, 1, S5_HALF)), _resident((nb, 1, S5_HALF)),
        _resident((nb, S5_CHUNK, S5_HALF)), _resident((nb, S5_CHUNK, S5_HALF)),
        _resident((1, D_MODEL)), _resident((D_MODEL, 2 * D_MODEL)),
    ]
    return pl.pallas_call(
        functools.partial(_s5_body, nchunk=nchunk),
        grid=(total + 1,),
        in_specs=in_specs,
        out_specs=prev_spec,
        out_shape=jax.ShapeDtypeStruct(x2.shape, F32),
        scratch_shapes=[
            pltpu.VMEM((nb, S5_CHUNK, 2 * S5_HALF), F32),
            pltpu.VMEM((nb, 1, 2 * S5_HALF), F32),
            pltpu.VMEM((nb, S5_SUBSEQ, 2 * S5_HALF), F32),
            pltpu.VMEM((nb, S5_CHUNK, 2 * S5_HALF), BF16),
            pltpu.VMEM((S5_CHUNK, D_MODEL), F32),
        ],
        compiler_params=pltpu.CompilerParams(
            dimension_semantics=("arbitrary",), vmem_limit_bytes=VMEM_LIMIT),
        name="s5",
    )(x2, x2, g.reshape(1, D_MODEL), perm, permt, bblk, cblk,
      per_block(ar, 1), per_block(ai, 1), per_block(amr, 1), per_block(ami, 1),
      pr, pi,
      d_skip.reshape(1, D_MODEL), w_glu.astype(BF16))


def _rope_tables(pos0, invf, tab_ref):
    ang0 = pos0.astype(F32) * invf
    c0, s0 = jnp.cos(ang0), jnp.sin(ang0)
    cos_t = c0 * tab_ref[0] - s0 * tab_ref[1]
    sin_t = s0 * tab_ref[2] + c0 * tab_ref[3]
    return cos_t, sin_t


def _rope_head(xh, cos_t, sin_t, low_half):
    half = ROPE_DIM // 2
    swapped = jnp.where(low_half, pltpu.roll(xh, LANES - half, axis=1), pltpu.roll(xh, half, axis=1))
    return xh * cos_t + swapped * sin_t


def _rope_consts():
    half = ROPE_DIM // 2
    inv_freq = ROPE_THETA ** (-jnp.arange(0, ROPE_DIM, 2, dtype=F32) / ROPE_DIM)
    pad = jnp.zeros((HEAD_DIM - ROPE_DIM,), F32)
    invf = jnp.concatenate([inv_freq, inv_freq, pad]).reshape(1, HEAD_DIM)
    sign = jnp.concatenate([-jnp.ones((half,), F32), jnp.ones((half,), F32), pad]).reshape(1, HEAD_DIM)
    ang_r = jnp.arange(MOBA_BLOCK, dtype=F32)[:, None] * invf
    cos_r, sin_r = jnp.cos(ang_r), jnp.sin(ang_r)
    tables = jnp.stack([cos_r, sin_r, cos_r * sign, sin_r * sign])
    return invf, tables


def _kv_tile(x, blk, g, wkv_ref, invf, tab_ref, ka_ref, vt_ref, km_ref, sub):
    rows = slice(sub * MOBA_BLOCK, (sub + 1) * MOBA_BLOCK)
    h = _rms(x, g).astype(BF16)
    kv = _dot(h, wkv_ref[...])
    kdim = N_KV_HEADS * HEAD_DIM
    cos_t, sin_t = _rope_tables(blk * MOBA_BLOCK, invf, tab_ref)
    lane = lax.broadcasted_iota(jnp.int32, (MOBA_BLOCK, HEAD_DIM), 1)
    low_half = lane < ROPE_DIM // 2
    onehot = jnp.where(lane == blk, 1.0, 0.0).astype(BF16)
    pad_row = lax.broadcasted_iota(jnp.int32, (V_ROWS - HEAD_DIM, MOBA_BLOCK), 0)
    ones_rows = jnp.where(pad_row == 0, 1.0, 0.0).astype(BF16)
    means = []
    for hh in range(N_KV_HEADS):
        kh = _rope_head(kv[:, hh * HEAD_DIM:(hh + 1) * HEAD_DIM], cos_t, sin_t, low_half)
        means.append(jnp.mean(kh, axis=0, keepdims=True))
        ka_ref[hh, rows, :] = jnp.concatenate([kh.astype(BF16), onehot], axis=1)
        vt = kv[:, kdim + hh * HEAD_DIM:kdim + (hh + 1) * HEAD_DIM].T.astype(BF16)
        vt_ref[sub, hh] = jnp.concatenate([vt, ones_rows], axis=0)
    km_ref[sub] = jnp.concatenate(means, axis=1)


def _split_bf16(v):
    hi = v.astype(BF16)
    return hi, (v - hi.astype(F32)).astype(BF16)


def _attn_body(x_ref, g_ref, wq_ref, wo_ref, ka_ref, vt_ref, km_ref, invf_ref, o_ref,
               qa_ref, acc_ref, s0_ref, s1_ref, pb_ref, *, nblk):
    own = pl.program_id(1)
    items = KV_GROUP * MOBA_BLOCK
    group_keys = ATT_GROUP * MOBA_BLOCK
    half = ROPE_DIM // 2
    x = x_ref[0]
    h = _rms(x, g_ref[...]).astype(BF16)
    q = _dot(h, wq_ref[...])
    pos = (own * MOBA_BLOCK + lax.broadcasted_iota(jnp.int32, (1, MOBA_BLOCK), 1)).astype(F32)
    ang = invf_ref[...] * pos
    cos_t, sin_t = jnp.cos(ang), jnp.sin(ang)

    def head_t(i):
        qt = q[:, i * HEAD_DIM:(i + 1) * HEAD_DIM].T
        x1, x2 = qt[0:half], qt[half:2 * half]
        rot = jnp.concatenate([x1 * cos_t - x2 * sin_t, x2 * cos_t + x1 * sin_t, qt[2 * half:]], axis=0)
        return rot * (HEAD_DIM ** -0.5)

    blk = lax.broadcasted_iota(jnp.int32, (nblk, items), 0)
    blk_f = blk.astype(F32)
    past = blk < own
    causal = (lax.broadcasted_iota(jnp.int32, (MOBA_BLOCK, items), 0)
              <= lax.broadcasted_iota(jnp.int32, (MOBA_BLOCK, items), 1) % MOBA_BLOCK)
    feat_pad = jnp.zeros((HEAD_DIM - nblk, items), BF16)

    kv_heads = range(N_KV_HEADS)
    for kh in kv_heads:
        qt = jnp.concatenate([head_t(kh * KV_GROUP + i) for i in range(KV_GROUP)], axis=1)
        q_hi, q_lo = _split_bf16(qt)
        k_hi, k_lo = _split_bf16(km_ref[0, kh])
        g_hi = _dot(jnp.concatenate([k_hi, k_lo], axis=0), q_hi)
        gate = g_hi[0:nblk] + (_dot(k_hi, q_lo) + g_hi[nblk:2 * nblk])
        cur = jnp.where(past, gate, -jnp.inf)
        bias = jnp.where(blk == own, 0.0, NEG_INF)
        for _ in range(MOBA_TOPK):
            best = jnp.max(cur, axis=0, keepdims=True)
            cand = jnp.where((cur == best) & (best > -jnp.inf), blk_f, float(nblk))
            pick = blk_f == jnp.min(cand, axis=0, keepdims=True)
            bias = jnp.where(pick, 0.0, bias)
            cur = jnp.where(pick, -jnp.inf, cur)
        q_feat = (qt * LOG2E).astype(BF16)
        qa_ref[kh] = jnp.concatenate([q_feat, bias.astype(BF16), feat_pad], axis=0)
        acc_ref[kh] = jnp.zeros(acc_ref.shape[1:], F32)

    def score_group(kh, gi):
        keys = pl.ds(pl.multiple_of(gi * group_keys, group_keys), group_keys)
        return _dot(ka_ref[kh, 0, keys, :], qa_ref[kh])

    def weights(kh, s_cur_ref, m_prev, own_slot=False):
        if own_slot:
            slabs = []
            for j in range(ATT_GROUP):
                keep = jnp.logical_or(causal, own % ATT_GROUP != j)
                slabs.append(jnp.where(keep, s_cur_ref[kh, j * MOBA_BLOCK:(j + 1) * MOBA_BLOCK, :], NEG_INF))
            s = jnp.concatenate(slabs, axis=0)
        else:
            s = s_cur_ref[kh]
        m_new = jnp.maximum(m_prev, jnp.max(s, axis=0, keepdims=True))
        pb_ref[kh] = jnp.exp2((s - m_new).astype(BF16))
        return m_new, jnp.exp2(m_prev - m_new)

    def accumulate(kh, gi, alpha):
        n0 = gi * ATT_GROUP
        pv = _dot(vt_ref[0, n0, kh], pb_ref[kh, 0:MOBA_BLOCK, :])
        for j in range(1, ATT_GROUP):
            pv = pv + _dot(vt_ref[0, n0 + j, kh], pb_ref[kh, j * MOBA_BLOCK:(j + 1) * MOBA_BLOCK, :])
        acc_ref[kh] = alpha * acc_ref[kh] + pv

    own_group = own // ATT_GROUP

    def softmax_steps(s_cur_ref, gi, ms, before=(None,) * N_KV_HEADS, tail=None, own_slot=False):
        out = []
        alpha_prev = None
        for kh in kv_heads:
            if before[kh] is not None:
                before[kh]()
            if alpha_prev is not None:
                accumulate(kh - 1, gi, alpha_prev)
            m_new, alpha_prev = weights(kh, s_cur_ref, ms[kh], own_slot)
            out.append(m_new)
        if tail is not None:
            tail()
        accumulate(N_KV_HEADS - 1, gi, alpha_prev)
        return tuple(out)

    def scorer(dst_ref, kh, gi):
        def run():
            dst_ref[kh] = score_group(kh, gi)
        return run

    m0 = jnp.full((1, items), SCORE_FLOOR, F32)
    for kh in kv_heads:
        s0_ref[kh] = score_group(kh, 0)

    def group_pair(pi, ms):
        g = 2 * pi
        nxt = [scorer(s0_ref, kh, g + 2) for kh in kv_heads]
        ms = softmax_steps(s0_ref, g, ms, before=[scorer(s1_ref, kh, g + 1) for kh in kv_heads],
                           tail=nxt[0])
        return softmax_steps(s1_ref, g + 1, ms, before=[None] + nxt[1:])

    ms = lax.fori_loop(0, own_group // 2, group_pair, (m0,) * N_KV_HEADS)

    @pl.when(own_group % 2 == 0)
    def _():
        softmax_steps(s0_ref, own_group, ms, own_slot=True)

    @pl.when(own_group % 2 == 1)
    def _():
        mids = softmax_steps(s0_ref, own_group - 1, ms,
                             before=[scorer(s1_ref, kh, own_group) for kh in kv_heads])
        softmax_steps(s1_ref, own_group, mids, own_slot=True)

    outs = []
    for kh in kv_heads:
        ot = acc_ref[kh, 0:HEAD_DIM, :] / acc_ref[kh, HEAD_DIM:HEAD_DIM + 1, :]
        outs.extend(ot[:, i * MOBA_BLOCK:(i + 1) * MOBA_BLOCK].T for i in range(KV_GROUP))

    attn = jnp.concatenate(outs, axis=1).astype(BF16)
    o_ref[0] = x + _dot(attn, wo_ref[...])


def _attn_call(x3, g, w_q, w_o, ka, vt, km, invf_col):
    bsz, seq, _ = x3.shape
    nblk = seq // MOBA_BLOCK
    items = KV_GROUP * MOBA_BLOCK
    x_spec = pl.BlockSpec((1, MOBA_BLOCK, D_MODEL), lambda b, i: (b, i, 0))
    return pl.pallas_call(
        functools.partial(_attn_body, nblk=nblk),
        grid=(bsz, nblk),
        in_specs=[x_spec, _resident((1, D_MODEL)),
                  _resident((D_MODEL, D_MODEL)), _resident((D_MODEL, D_MODEL)),
                  pl.BlockSpec((N_KV_HEADS, 1, seq, 2 * HEAD_DIM), lambda b, i: (0, b, 0, 0),
                               pipeline_mode=pl.Buffered(1)),
                  pl.BlockSpec((1, nblk, N_KV_HEADS, V_ROWS, MOBA_BLOCK),
                               lambda b, i: (b, 0, 0, 0, 0), pipeline_mode=pl.Buffered(1)),
                  pl.BlockSpec((1, N_KV_HEADS, nblk, HEAD_DIM), lambda b, i: (b, 0, 0, 0)),
                  _resident((ROPE_DIM // 2, 1))],
        out_specs=x_spec,
        out_shape=jax.ShapeDtypeStruct(x3.shape, F32),
        scratch_shapes=[
            pltpu.VMEM((N_KV_HEADS, 2 * HEAD_DIM, items), BF16),
            pltpu.VMEM((N_KV_HEADS, V_ROWS, items), F32),
            pltpu.VMEM((N_KV_HEADS, ATT_GROUP * MOBA_BLOCK, items), F32),
            pltpu.VMEM((N_KV_HEADS, ATT_GROUP * MOBA_BLOCK, items), F32),
            pltpu.VMEM((N_KV_HEADS, ATT_GROUP * MOBA_BLOCK, items), BF16)],
        compiler_params=pltpu.CompilerParams(
            dimension_semantics=("arbitrary", "arbitrary"), vmem_limit_bytes=VMEM_LIMIT),
        name="moba_attn",
    )(x3, g.reshape(1, D_MODEL), w_q.astype(BF16), w_o.astype(BF16), ka, vt, km, invf_col)


def kernel(x, norm_g, ffn_w_in, ffn_w_out, s5_a_re, s5_a_im, s5_log_step, s5_b_re, s5_b_im,
           s5_c_re, s5_c_im, s5_d, s5_w_glu, kv_norm_g, w_k, w_v, w_q, w_o, final_g):
    bsz, seq, _ = x.shape
    assert seq % S5_CHUNK == 0 and seq % MOBA_BLOCK == 0
    assert (seq // MOBA_BLOCK) % (2 * SUBLANES) == 0 and seq // MOBA_BLOCK <= HEAD_DIM
    assert (seq // MOBA_BLOCK) % ATT_GROUP == 0
    assert (bsz * seq) % FFN_ROWS == 0 and FFN_ROWS % MOBA_BLOCK == 0
    nblk = seq // MOBA_BLOCK
    x2 = x.reshape(bsz * seq, D_MODEL)
    invf, rope_tab = _rope_consts()

    wkv = jnp.concatenate([w_k, w_v], axis=1).astype(BF16)

    x2 = _ffn_call(x2, norm_g[0, 0], ffn_w_in, ffn_w_out, 0, 0)
    prep = _s5_prep_call(s5_a_re[0], s5_a_im[0], s5_log_step[0], s5_b_re[0], s5_b_im[0], s5_c_im[0])
    x2 = _s5_call(x2, bsz, seq, norm_g[0, 1], prep, s5_c_re[0], s5_d[0], s5_w_glu[0])
    x2, ka, vt, km = _ffn_call(x2, norm_g[0, 2], ffn_w_in, ffn_w_out, 0, 1,
                               kv=(kv_norm_g, wkv, invf, rope_tab), nblk=nblk)
    ka = ka.reshape(N_KV_HEADS, bsz, seq, 2 * HEAD_DIM)
    vt = vt.reshape(bsz, nblk, N_KV_HEADS, V_ROWS, MOBA_BLOCK)
    km = km.reshape(bsz, nblk, N_KV_HEADS, HEAD_DIM).transpose(0, 2, 1, 3)

    x2 = _ffn_call(x2, norm_g[1, 0], ffn_w_in, ffn_w_out, 1, 0)
    x3 = _attn_call(x2.reshape(bsz, seq, D_MODEL), norm_g[1, 1], w_q[0], w_o[0], ka, vt, km,
                    invf[0, 0:ROPE_DIM // 2].reshape(ROPE_DIM // 2, 1))
    x2 = _ffn_call(x3.reshape(bsz * seq, D_MODEL), norm_g[1, 2], ffn_w_in, ffn_w_out, 1, 1,
                   final_g=final_g)
    return x2.reshape(bsz, seq, D_MODEL)
```

```python
import functools

import jax
import jax.numpy as jnp
import numpy as np
from jax import lax
from jax.experimental import pallas as pl
from jax.experimental.pallas import tpu as pltpu

F32 = jnp.float32
BF16 = jnp.bfloat16

D_MODEL = 1024
D_FF = 2816
RMS_EPS = 1e-6
S5_GROUP = 16
S5_GROUPS = D_MODEL // S5_GROUP
S5_STATE = 64
N_HEADS = 8
HEAD_DIM = 128
N_KV_HEADS = 2
KV_GROUP = N_HEADS // N_KV_HEADS
ROPE_DIM = HEAD_DIM // 4
ROPE_THETA = 500000.0
MOBA_BLOCK = 256
MOBA_TOPK = 3
NEG_INF = -1e30
SCORE_FLOOR = 0.5 * NEG_INF
LOG2E = 1.4426950408889634

LANES = 128
SUBLANES = 8
VMEM_LIMIT = 56 * 1024 * 1024

FFN_ROWS = 512
FFN_COLS = 256

ATT_GROUP = 4
V_ROWS = HEAD_DIM + 2 * SUBLANES

S5_CHUNK = 256
S5_SUBSEQ = SUBLANES
S5_STEPS = S5_CHUNK // S5_SUBSEQ
S5_LANE_BLOCKS = D_MODEL // LANES
S5_GROUPS_PER_BLOCK = LANES // S5_GROUP
S5_HALF = S5_GROUPS_PER_BLOCK * S5_STATE
S5_NSTATE = S5_GROUPS * S5_STATE
S5_GLU_COLS = 2 * D_MODEL // S5_LANE_BLOCKS

def _rms(x, g):
    ms = jnp.mean(x * x, axis=-1, keepdims=True)
    return (x * lax.rsqrt(ms + RMS_EPS)) * g


def _dot(a, b):
    return jnp.dot(a, b, preferred_element_type=F32)


def _resident(shape):
    nd = len(shape)
    return pl.BlockSpec(shape, lambda *_: (0,) * nd, pipeline_mode=pl.Buffered(1))


def _ffn_body(*refs, mode, nblk):
    x_ref, g_ref, win_ref, wout_ref = refs[:4]
    x = x_ref[...]
    h = _rms(x, g_ref[...]).astype(BF16)
    acc = jnp.zeros(x.shape, F32)
    for c in range(D_FF // FFN_COLS):
        lo = c * FFN_COLS
        gate = _dot(h, win_ref[:, lo:lo + FFN_COLS].astype(BF16))
        up = _dot(h, win_ref[:, D_FF + lo:D_FF + lo + FFN_COLS].astype(BF16))
        act = (gate * jax.nn.sigmoid(gate)) * up
        acc = acc + _dot(act.astype(BF16), wout_ref[lo:lo + FFN_COLS, :].astype(BF16))
    y = x + 0.5 * acc
    if mode == 'final':
        fg_ref, o_ref = refs[4:]
        y = _rms(y, fg_ref[...])
    elif mode == 'kv':
        kvg_ref, wkv_ref, invf_ref, tab_ref, o_ref, ka_ref, vt_ref, km_ref = refs[4:]
        for sub in range(FFN_ROWS // MOBA_BLOCK):
            rows = slice(sub * MOBA_BLOCK, (sub + 1) * MOBA_BLOCK)
            blk = (pl.program_id(0) * (FFN_ROWS // MOBA_BLOCK) + sub) % nblk
            _kv_tile(y[rows], blk, kvg_ref[...], wkv_ref, invf_ref[...], tab_ref,
                     ka_ref, vt_ref, km_ref, sub)
    else:
        o_ref, = refs[4:]
    o_ref[...] = y


def _ffn_call(x2, g, w_in_all, w_out_all, layer, idx, final_g=None, kv=None, nblk=None):
    t = x2.shape[0]
    mode = 'final' if final_g is not None else ('kv' if kv is not None else 'plain')
    row_spec = pl.BlockSpec((FFN_ROWS, D_MODEL), lambda i: (i, 0))
    pick = lambda i: (layer, idx, 0, 0)
    in_specs = [row_spec, _resident((1, D_MODEL)),
                pl.BlockSpec((None, None, D_MODEL, 2 * D_FF), pick, pipeline_mode=pl.Buffered(1)),
                pl.BlockSpec((None, None, D_FF, D_MODEL), pick, pipeline_mode=pl.Buffered(1))]
    args = [x2, g.reshape(1, D_MODEL), w_in_all, w_out_all]
    out_specs = row_spec
    out_shape = jax.ShapeDtypeStruct((t, D_MODEL), F32)
    if mode == 'final':
        in_specs.append(_resident((1, D_MODEL)))
        args.append(final_g.reshape(1, D_MODEL))
    elif mode == 'kv':
        kvg, wkv, invf, rope_tab = kv
        kdim = N_KV_HEADS * HEAD_DIM
        sub = FFN_ROWS // MOBA_BLOCK
        ntile = t // MOBA_BLOCK
        in_specs += [_resident((1, D_MODEL)), _resident((D_MODEL, 2 * kdim)),
                     _resident((1, HEAD_DIM)), _resident(rope_tab.shape)]
        args += [kvg.reshape(1, D_MODEL), wkv, invf, rope_tab]
        out_specs = [row_spec,
                     pl.BlockSpec((N_KV_HEADS, FFN_ROWS, 2 * HEAD_DIM), lambda i: (0, i, 0)),
                     pl.BlockSpec((sub, N_KV_HEADS, V_ROWS, MOBA_BLOCK), lambda i: (i, 0, 0, 0)),
                     pl.BlockSpec((sub, 1, kdim), lambda i: (i, 0, 0))]
        out_shape = [out_shape,
                     jax.ShapeDtypeStruct((N_KV_HEADS, t, 2 * HEAD_DIM), BF16),
                     jax.ShapeDtypeStruct((ntile, N_KV_HEADS, V_ROWS, MOBA_BLOCK), BF16),
                     jax.ShapeDtypeStruct((ntile, 1, kdim), F32)]
    return pl.pallas_call(
        functools.partial(_ffn_body, mode=mode, nblk=nblk),
        grid=(t // FFN_ROWS,),
        in_specs=in_specs,
        out_specs=out_specs,
        out_shape=out_shape,
        compiler_params=pltpu.CompilerParams(
            dimension_semantics=("arbitrary",), vmem_limit_bytes=VMEM_LIMIT),
        name="ffn_" + mode,
    )(*args)


def _s5_prep_body(lr_ref, li_ref, ls_ref, bre_ref, bim_ref, cim_ref,
                  ar_ref, ai_ref, amr_ref, ami_ref, pr_ref, pi_ref, btr_ref, bti_ref, cneg_ref):
    lr = lr_ref[...]
    li = li_ref[...]
    dt = jnp.exp(ls_ref[...])
    mag = jnp.exp(lr * dt)
    abar_re = mag * jnp.cos(li * dt)
    abar_im = mag * jnp.sin(li * dt)
    ar_ref[...] = abar_re
    ai_ref[...] = abar_im
    nr, ni = abar_re - 1.0, abar_im
    den = lr * lr + li * li
    coef_re = (nr * lr + ni * li) / den
    coef_im = (ni * lr - nr * li) / den
    k = (lax.broadcasted_iota(jnp.int32, (S5_STEPS, 1), 0) + 1).astype(F32)
    pmag = jnp.exp((lr * dt) * k)
    pang = (li * dt) * k
    pr = pmag * jnp.cos(pang)
    pi = pmag * jnp.sin(pang)
    for j in range(S5_LANE_BLOCKS):
        cols = slice(j * S5_HALF, (j + 1) * S5_HALF)
        for step in range(S5_STEPS):
            rows = slice(step * SUBLANES, (step + 1) * SUBLANES)
            pr_ref[j, rows, :] = jnp.broadcast_to(pr[step:step + 1, cols], (SUBLANES, S5_HALF))
            pi_ref[j, rows, :] = jnp.broadcast_to(pi[step:step + 1, cols], (SUBLANES, S5_HALF))
    amr_ref[...] = pr[S5_STEPS - 1:S5_STEPS, :]
    ami_ref[...] = pi[S5_STEPS - 1:S5_STEPS, :]
    bre = bre_ref[...]
    bim = bim_ref[...]
    btr_ref[...] = coef_re * bre - coef_im * bim
    bti_ref[...] = coef_re * bim + coef_im * bre
    cneg_ref[...] = -cim_ref[...]


def _s5_prep_call(a_re, a_im, log_step, b_re, b_im, c_im):
    n = S5_NSTATE
    row = lambda v: v.reshape(1, n)
    chan_major = lambda v: v.transpose(2, 0, 1).reshape(S5_GROUP, n)
    ls = jnp.repeat(log_step, S5_STATE)
    outs = pl.pallas_call(
        _s5_prep_body,
        out_shape=[jax.ShapeDtypeStruct((1, n), F32)] * 4
        + [jax.ShapeDtypeStruct((S5_LANE_BLOCKS, S5_CHUNK, S5_HALF), F32)] * 2
        + [jax.ShapeDtypeStruct((S5_GROUP, n), F32)] * 3,
        name="s5_prep",
    )(row(a_re), row(a_im), row(ls), chan_major(b_re), chan_major(b_im),
      c_im.transpose(1, 0, 2).reshape(S5_GROUP, n))
    return outs


def _s5_body(x_ref, xprev_ref, g_ref, perm_ref, permt_ref, bblk_ref, cblk_ref, ar_ref, ai_ref,
             amr_ref, ami_ref, pr_ref, pi_ref, d_ref, wglu_ref, o_ref,
             xs_ref, st_ref, c_ref, hb_ref, y_ref, *, nchunk):
    step = pl.program_id(0)

    @pl.when(step % nchunk == 0)
    def _():
        st_ref[...] = jnp.zeros(st_ref.shape, F32)

    @pl.when(step == 0)
    def _():
        y_ref[...] = jnp.zeros(y_ref.shape, F32)

    yp = y_ref[...]
    yp = 0.5 * yp * (1.0 + jnp.tanh(np.sqrt(2.0 / np.pi).astype(np.float32)
                                    * (yp + 0.044715 * (yp * yp * yp))))
    yn = _dot(permt_ref[...], yp.astype(BF16)).astype(BF16)

    def glu_piece(c):
        cols = slice(c * S5_GLU_COLS, (c + 1) * S5_GLU_COLS)
        gcols = slice(D_MODEL + c * S5_GLU_COLS, D_MODEL + (c + 1) * S5_GLU_COLS)
        val = _dot(yn, wglu_ref[:, cols])
        gate = _dot(yn, wglu_ref[:, gcols])
        o_ref[:, cols] = xprev_ref[:, cols] + val * jax.nn.sigmoid(gate)

    x = x_ref[...]
    u = _rms(x, g_ref[...])
    u_hi = u.astype(BF16)
    u_lo = (u - u_hi.astype(F32)).astype(BF16)
    perm = perm_ref[...]
    up_hi = _dot(perm, u_hi)
    up = up_hi + _dot(perm, u_lo)
    ub = up_hi.astype(BF16)

    last = SUBLANES * (S5_STEPS - 1)
    nb = S5_LANE_BLOCKS
    re, im = slice(0, S5_HALF), slice(S5_HALF, 2 * S5_HALF)

    def project_in(j):
        xs_ref[j] = _dot(ub[:, j * LANES:(j + 1) * LANES], bblk_ref[j])

    for j in range(nb):
        project_in(j)

    def scan_steps(j):
        a_r = jnp.broadcast_to(ar_ref[j], (SUBLANES, S5_HALF))
        a_i = jnp.broadcast_to(ai_ref[j], (SUBLANES, S5_HALF))
        state = [xs_ref[j, 0:SUBLANES, re], xs_ref[j, 0:SUBLANES, im]]

        def step(k):
            rows = slice(k * SUBLANES, (k + 1) * SUBLANES)
            h_r, h_i = state
            state[0] = a_r * h_r - a_i * h_i + xs_ref[j, rows, re]
            state[1] = a_r * h_i + a_i * h_r + xs_ref[j, rows, im]
            xs_ref[j, rows, re] = state[0]
            xs_ref[j, rows, im] = state[1]

        return [functools.partial(step, k) for k in range(1, S5_STEPS)]

    def entering_states(j):
        am_r, am_i = amr_ref[j], ami_ref[j]
        c_r, c_i = st_ref[j, :, re], st_ref[j, :, im]
        for i in range(S5_SUBSEQ):
            c_ref[j, i:i + 1, re] = c_r
            c_ref[j, i:i + 1, im] = c_i
            e_r = xs_ref[j, last + i:last + i + 1, re]
            e_i = xs_ref[j, last + i:last + i + 1, im]
            c_r, c_i = am_r * c_r - am_i * c_i + e_r, am_r * c_i + am_i * c_r + e_i
        st_ref[j, :, re] = c_r
        st_ref[j, :, im] = c_i

    def fix_steps(j):
        cc_r = jnp.concatenate([c_ref[j, :, re]] * 2, axis=0)
        cc_i = jnp.concatenate([c_ref[j, :, im]] * 2, axis=0)

        def step(k):
            rows = slice(2 * k * SUBLANES, 2 * (k + 1) * SUBLANES)
            p_r, p_i = pr_ref[j, rows, :], pi_ref[j, rows, :]
            t_r = xs_ref[j, rows, re] + (p_r * cc_r - p_i * cc_i)
            t_i = xs_ref[j, rows, im] + (p_r * cc_i + p_i * cc_r)
            hb_ref[j, rows, :] = jnp.concatenate([t_r, t_i], axis=1).astype(BF16)

        return [functools.partial(step, k) for k in range(S5_STEPS // 2)]

    ys = []
    for j in range(nb + 1):
        if j % 2 == 0 and j < nb:
            glu_piece(j // 2)
        p1 = scan_steps(j) if j < nb else []
        p2 = fix_steps(j - 1) if j >= 1 else []
        for k in range(max(len(p2), (len(p1) + 1) // 2)):
            for f in p1[2 * k:2 * k + 2]:
                f()
            if k < len(p2):
                p2[k]()
        if j < nb:
            entering_states(j)
        if j >= 1:
            ys.append(_dot(hb_ref[j - 1], cblk_ref[j - 1]))

    y_ref[...] = jnp.concatenate(ys, axis=1) + d_ref[...] * up


def _s5_call(x2, bsz, seq, g, prep, c_re, d_skip, w_glu):
    ar, ai, amr, ami, pr, pi, btr, bti, cneg = prep
    nb, gb, ns = S5_LANE_BLOCKS, S5_GROUPS_PER_BLOCK, S5_STATE

    def per_block(v, rows):
        return v.reshape(rows, nb, S5_HALF).transpose(1, 0, 2)

    state_group = jnp.arange(2 * S5_HALF) % S5_HALF // ns
    own_group = state_group[None, :] == jnp.arange(gb)[:, None]
    bt = jnp.stack([btr, bti], axis=1).reshape(S5_GROUP, 2, nb, S5_HALF)
    bt = bt.transpose(2, 0, 1, 3).reshape(nb, 1, S5_GROUP, 2 * S5_HALF)
    bblk = jnp.where(own_group[None, :, None, :], bt, 0.0).reshape(nb, LANES, 2 * S5_HALF).astype(BF16)
    cmat = jnp.stack([c_re.transpose(1, 0, 2).reshape(S5_GROUP, S5_NSTATE), cneg], axis=1)
    cmat = cmat.reshape(S5_GROUP, 2, nb, S5_HALF).transpose(2, 1, 3, 0)
    cmat = cmat.reshape(nb, 2 * S5_HALF, 1, S5_GROUP)
    cblk = jnp.where(own_group.T[None, :, :, None], cmat, 0.0).reshape(nb, 2 * S5_HALF, LANES).astype(BF16)

    r = np.arange(S5_CHUNK)
    perm_np = np.zeros((S5_CHUNK, S5_CHUNK), np.float32)
    perm_np[r, (r % SUBLANES) * S5_STEPS + r // SUBLANES] = 1.0
    perm = jnp.asarray(perm_np, BF16)
    permt = jnp.asarray(perm_np.T, BF16)

    nchunk = seq // S5_CHUNK
    total = bsz * nchunk
    cur_spec = pl.BlockSpec((S5_CHUNK, D_MODEL), lambda s: (jnp.minimum(s, total - 1), 0))
    prev_spec = pl.BlockSpec((S5_CHUNK, D_MODEL), lambda s: (jnp.maximum(s - 1, 0), 0))
    in_specs = [
        cur_spec, prev_spec, _resident((1, D_MODEL)),
        _resident((S5_CHUNK, S5_CHUNK)), _resident((S5_CHUNK, S5_CHUNK)),
        _resident((nb, LANES, 2 * S5_HALF)), _resident((nb, 2 * S5_HALF, LANES)),
        _resident((nb, 1, S5_HALF)), _resident((nb, 1, S5_HALF)),
        _resident((nb, 1, S5_HALF)), _resident((nb, 1, S5_HALF)),
        _resident((nb, S5_CHUNK, S5_HALF)), _resident((nb, S5_CHUNK, S5_HALF)),
        _resident((1, D_MODEL)), _resident((D_MODEL, 2 * D_MODEL)),
    ]
    return pl.pallas_call(
        functools.partial(_s5_body, nchunk=nchunk),
        grid=(total + 1,),
        in_specs=in_specs,
        out_specs=prev_spec,
        out_shape=jax.ShapeDtypeStruct(x2.shape, F32),
        scratch_shapes=[
            pltpu.VMEM((nb, S5_CHUNK, 2 * S5_HALF), F32),
            pltpu.VMEM((nb, 1, 2 * S5_HALF), F32),
            pltpu.VMEM((nb, S5_SUBSEQ, 2 * S5_HALF), F32),
            pltpu.VMEM((nb, S5_CHUNK, 2 * S5_HALF), BF16),
            pltpu.VMEM((S5_CHUNK, D_MODEL), F32),
        ],
        compiler_params=pltpu.CompilerParams(
            dimension_semantics=("arbitrary",), vmem_limit_bytes=VMEM_LIMIT),
        name="s5",
    )(x2, x2, g.reshape(1, D_MODEL), perm, permt, bblk, cblk,
      per_block(ar, 1), per_block(ai, 1), per_block(amr, 1), per_block(ami, 1),
      pr, pi,
      d_skip.reshape(1, D_MODEL), w_glu.astype(BF16))


def _rope_tables(pos0, invf, tab_ref):
    ang0 = pos0.astype(F32) * invf
    c0, s0 = jnp.cos(ang0), jnp.sin(ang0)
    cos_t = c0 * tab_ref[0] - s0 * tab_ref[1]
    sin_t = s0 * tab_ref[2] + c0 * tab_ref[3]
    return cos_t, sin_t


def _rope_head(xh, cos_t, sin_t, low_half):
    half = ROPE_DIM // 2
    swapped = jnp.where(low_half, pltpu.roll(xh, LANES - half, axis=1), pltpu.roll(xh, half, axis=1))
    return xh * cos_t + swapped * sin_t


def _rope_consts():
    half = ROPE_DIM // 2
    inv_freq = ROPE_THETA ** (-jnp.arange(0, ROPE_DIM, 2, dtype=F32) / ROPE_DIM)
    pad = jnp.zeros((HEAD_DIM - ROPE_DIM,), F32)
    invf = jnp.concatenate([inv_freq, inv_freq, pad]).reshape(1, HEAD_DIM)
    sign = jnp.concatenate([-jnp.ones((half,), F32), jnp.ones((half,), F32), pad]).reshape(1, HEAD_DIM)
    ang_r = jnp.arange(MOBA_BLOCK, dtype=F32)[:, None] * invf
    cos_r, sin_r = jnp.cos(ang_r), jnp.sin(ang_r)
    tables = jnp.stack([cos_r, sin_r, cos_r * sign, sin_r * sign])
    return invf, tables


def _kv_tile(x, blk, g, wkv_ref, invf, tab_ref, ka_ref, vt_ref, km_ref, sub):
    rows = slice(sub * MOBA_BLOCK, (sub + 1) * MOBA_BLOCK)
    h = _rms(x, g).astype(BF16)
    kv = _dot(h, wkv_ref[...])
    kdim = N_KV_HEADS * HEAD_DIM
    cos_t, sin_t = _rope_tables(blk * MOBA_BLOCK, invf, tab_ref)
    lane = lax.broadcasted_iota(jnp.int32, (MOBA_BLOCK, HEAD_DIM), 1)
    low_half = lane < ROPE_DIM // 2
    onehot = jnp.where(lane == blk, 1.0, 0.0).astype(BF16)
    pad_row = lax.broadcasted_iota(jnp.int32, (V_ROWS - HEAD_DIM, MOBA_BLOCK), 0)
    ones_rows = jnp.where(pad_row == 0, 1.0, 0.0).astype(BF16)
    means = []
    for hh in range(N_KV_HEADS):
        kh = _rope_head(kv[:, hh * HEAD_DIM:(hh + 1) * HEAD_DIM], cos_t, sin_t, low_half)
        means.append(jnp.mean(kh, axis=0, keepdims=True))
        ka_ref[hh, rows, :] = jnp.concatenate([kh.astype(BF16), onehot], axis=1)
        vt = kv[:, kdim + hh * HEAD_DIM:kdim + (hh + 1) * HEAD_DIM].T.astype(BF16)
        vt_ref[sub, hh] = jnp.concatenate([vt, ones_rows], axis=0)
    km_ref[sub] = jnp.concatenate(means, axis=1)


def _split_bf16(v):
    hi = v.astype(BF16)
    return hi, (v - hi.astype(F32)).astype(BF16)


def _attn_body(x_ref, g_ref, wq_ref, wo_ref, ka_ref, vt_ref, km_ref, invf_ref, o_ref,
               qa_ref, acc_ref, s0_ref, s1_ref, pb_ref, *, nblk):
    own = pl.program_id(1)
    items = KV_GROUP * MOBA_BLOCK
    group_keys = ATT_GROUP * MOBA_BLOCK
    half = ROPE_DIM // 2
    x = x_ref[0]
    h = _rms(x, g_ref[...]).astype(BF16)
    q = _dot(h, wq_ref[...])
    pos = (own * MOBA_BLOCK + lax.broadcasted_iota(jnp.int32, (1, MOBA_BLOCK), 1)).astype(F32)
    ang = invf_ref[...] * pos
    cos_t, sin_t = jnp.cos(ang), jnp.sin(ang)

    def head_t(i):
        qt = q[:, i * HEAD_DIM:(i + 1) * HEAD_DIM].T
        x1, x2 = qt[0:half], qt[half:2 * half]
        rot = jnp.concatenate([x1 * cos_t - x2 * sin_t, x2 * cos_t + x1 * sin_t, qt[2 * half:]], axis=0)
        return rot * (HEAD_DIM ** -0.5)

    blk = lax.broadcasted_iota(jnp.int32, (nblk, items), 0)
    blk_f = blk.astype(F32)
    past = blk < own
    causal = (lax.broadcasted_iota(jnp.int32, (MOBA_BLOCK, items), 0)
              <= lax.broadcasted_iota(jnp.int32, (MOBA_BLOCK, items), 1) % MOBA_BLOCK)
    feat_pad = jnp.zeros((HEAD_DIM - nblk, items), BF16)

    kv_heads = range(N_KV_HEADS)
    for kh in kv_heads:
        qt = jnp.concatenate([head_t(kh * KV_GROUP + i) for i in range(KV_GROUP)], axis=1)
        q_hi, q_lo = _split_bf16(qt)
        k_hi, k_lo = _split_bf16(km_ref[0, kh])
        g_hi = _dot(jnp.concatenate([k_hi, k_lo], axis=0), q_hi)
        gate = g_hi[0:nblk] + (_dot(k_hi, q_lo) + g_hi[nblk:2 * nblk])
        cur = jnp.where(past, gate, -jnp.inf)
        bias = jnp.where(blk == own, 0.0, NEG_INF)
        for _ in range(MOBA_TOPK):
            best = jnp.max(cur, axis=0, keepdims=True)
            cand = jnp.where((cur == best) & (best > -jnp.inf), blk_f, float(nblk))
            pick = blk_f == jnp.min(cand, axis=0, keepdims=True)
            bias = jnp.where(pick, 0.0, bias)
            cur = jnp.where(pick, -jnp.inf, cur)
        q_feat = (qt * LOG2E).astype(BF16)
        qa_ref[kh] = jnp.concatenate([q_feat, bias.astype(BF16), feat_pad], axis=0)
        acc_ref[kh] = jnp.zeros(acc_ref.shape[1:], F32)

    def score_group(kh, gi):
        keys = pl.ds(pl.multiple_of(gi * group_keys, group_keys), group_keys)
        return _dot(ka_ref[kh, 0, keys, :], qa_ref[kh])

    def weights(kh, s_cur_ref, m_prev, own_slot=False):
        if own_slot:
            slabs = []
            for j in range(ATT_GROUP):
                keep = jnp.logical_or(causal, own % ATT_GROUP != j)
                slabs.append(jnp.where(keep, s_cur_ref[kh, j * MOBA_BLOCK:(j + 1) * MOBA_BLOCK, :], NEG_INF))
            s = jnp.concatenate(slabs, axis=0)
        else:
            s = s_cur_ref[kh]
        m_new = jnp.maximum(m_prev, jnp.max(s, axis=0, keepdims=True))
        pb_ref[kh] = jnp.exp2((s - m_new).astype(BF16))
        return m_new, jnp.exp2(m_prev - m_new)

    def accumulate(kh, gi, alpha):
        n0 = gi * ATT_GROUP
        pv = _dot(vt_ref[0, n0, kh], pb_ref[kh, 0:MOBA_BLOCK, :])
        for j in range(1, ATT_GROUP):
            pv = pv + _dot(vt_ref[0, n0 + j, kh], pb_ref[kh, j * MOBA_BLOCK:(j + 1) * MOBA_BLOCK, :])
        acc_ref[kh] = alpha * acc_ref[kh] + pv

    own_group = own // ATT_GROUP

    def softmax_steps(s_cur_ref, gi, ms, before=(None,) * N_KV_HEADS, tail=None, own_slot=False):
        out = []
        alpha_prev = None
        for kh in kv_heads:
            if before[kh] is not None:
                before[kh]()
            if alpha_prev is not None:
                accumulate(kh - 1, gi, alpha_prev)
            m_new, alpha_prev = weights(kh, s_cur_ref, ms[kh], own_slot)
            out.append(m_new)
        if tail is not None:
            tail()
        accumulate(N_KV_HEADS - 1, gi, alpha_prev)
        return tuple(out)

    def scorer(dst_ref, kh, gi):
        def run():
            dst_ref[kh] = score_group(kh, gi)
        return run

    m0 = jnp.full((1, items), SCORE_FLOOR, F32)
    for kh in kv_heads:
        s0_ref[kh] = score_group(kh, 0)

    def group_pair(pi, ms):
        g = 2 * pi
        nxt = [scorer(s0_ref, kh, g + 2) for kh in kv_heads]
        ms = softmax_steps(s0_ref, g, ms, before=[scorer(s1_ref, kh, g + 1) for kh in kv_heads],
                           tail=nxt[0])
        return softmax_steps(s1_ref, g + 1, ms, before=[None] + nxt[1:])

    ms = lax.fori_loop(0, own_group // 2, group_pair, (m0,) * N_KV_HEADS)

    @pl.when(own_group % 2 == 0)
    def _():
        softmax_steps(s0_ref, own_group, ms, own_slot=True)

    @pl.when(own_group % 2 == 1)
    def _():
        mids = softmax_steps(s0_ref, own_group - 1, ms,
                             before=[scorer(s1_ref, kh, own_group) for kh in kv_heads])
        softmax_steps(s1_ref, own_group, mids, own_slot=True)

    outs = []
    for kh in kv_heads:
        ot = acc_ref[kh, 0:HEAD_DIM, :] / acc_ref[kh, HEAD_DIM:HEAD_DIM + 1, :]
        outs.extend(ot[:, i * MOBA_BLOCK:(i + 1) * MOBA_BLOCK].T for i in range(KV_GROUP))

    attn = jnp.concatenate(outs, axis=1).astype(BF16)
    o_ref[0] = x + _dot(attn, wo_ref[...])


def _attn_call(x3, g, w_q, w_o, ka, vt, km, invf_col):
    bsz, seq, _ = x3.shape
    nblk = seq // MOBA_BLOCK
    items = KV_GROUP * MOBA_BLOCK
    x_spec = pl.BlockSpec((1, MOBA_BLOCK, D_MODEL), lambda b, i: (b, i, 0))
    return pl.pallas_call(
        functools.partial(_attn_body, nblk=nblk),
        grid=(bsz, nblk),
        in_specs=[x_spec, _resident((1, D_MODEL)),
                  _resident((D_MODEL, D_MODEL)), _resident((D_MODEL, D_MODEL)),
                  pl.BlockSpec((N_KV_HEADS, 1, seq, 2 * HEAD_DIM), lambda b, i: (0, b, 0, 0),
                               pipeline_mode=pl.Buffered(1)),
                  pl.BlockSpec((1, nblk, N_KV_HEADS, V_ROWS, MOBA_BLOCK),
                               lambda b, i: (b, 0, 0, 0, 0), pipeline_mode=pl.Buffered(1)),
                  pl.BlockSpec((1, N_KV_HEADS, nblk, HEAD_DIM), lambda b, i: (b, 0, 0, 0)),
                  _resident((ROPE_DIM // 2, 1))],
        out_specs=x_spec,
        out_shape=jax.ShapeDtypeStruct(x3.shape, F32),
        scratch_shapes=[
            pltpu.VMEM((N_KV_HEADS, 2 * HEAD_DIM, items), BF16),
            pltpu.VMEM((N_KV_HEADS, V_ROWS, items), F32),
            pltpu.VMEM((N_KV_HEADS, ATT_GROUP * MOBA_BLOCK, items), F32),
            pltpu.VMEM((N_KV_HEADS, ATT_GROUP * MOBA_BLOCK, items), F32),
            pltpu.VMEM((N_KV_HEADS, ATT_GROUP * MOBA_BLOCK, items), BF16)],
        compiler_params=pltpu.CompilerParams(
            dimension_semantics=("arbitrary", "arbitrary"), vmem_limit_bytes=VMEM_LIMIT),
        name="moba_attn",
    )(x3, g.reshape(1, D_MODEL), w_q.astype(BF16), w_o.astype(BF16), ka, vt, km, invf_col)


def kernel(x, norm_g, ffn_w_in, ffn_w_out, s5_a_re, s5_a_im, s5_log_step, s5_b_re, s5_b_im,
           s5_c_re, s5_c_im, s5_d, s5_w_glu, kv_norm_g, w_k, w_v, w_q, w_o, final_g):
    bsz, seq, _ = x.shape
    assert seq % S5_CHUNK == 0 and seq % MOBA_BLOCK == 0
    assert (seq // MOBA_BLOCK) % (2 * SUBLANES) == 0 and seq // MOBA_BLOCK <= HEAD_DIM
    assert (seq // MOBA_BLOCK) % ATT_GROUP == 0
    assert (bsz * seq) % FFN_ROWS == 0 and FFN_ROWS % MOBA_BLOCK == 0
    nblk = seq // MOBA_BLOCK
    x2 = x.reshape(bsz * seq, D_MODEL)
    invf, rope_tab = _rope_consts()

    wkv = jnp.concatenate([w_k, w_v], axis=1).astype(BF16)

    x2 = _ffn_call(x2, norm_g[0, 0], ffn_w_in, ffn_w_out, 0, 0)
    prep = _s5_prep_call(s5_a_re[0], s5_a_im[0], s5_log_step[0], s5_b_re[0], s5_b_im[0], s5_c_im[0])
    x2 = _s5_call(x2, bsz, seq, norm_g[0, 1], prep, s5_c_re[0], s5_d[0], s5_w_glu[0])
    x2, ka, vt, km = _ffn_call(x2, norm_g[0, 2], ffn_w_in, ffn_w_out, 0, 1,
                               kv=(kv_norm_g, wkv, invf, rope_tab), nblk=nblk)
    ka = ka.reshape(N_KV_HEADS, bsz, seq, 2 * HEAD_DIM)
    vt = vt.reshape(bsz, nblk, N_KV_HEADS, V_ROWS, MOBA_BLOCK)
    km = km.reshape(bsz, nblk, N_KV_HEADS, HEAD_DIM).transpose(0, 2, 1, 3)

    x2 = _ffn_call(x2, norm_g[1, 0], ffn_w_in, ffn_w_out, 1, 0)
    x3 = _attn_call(x2.reshape(bsz, seq, D_MODEL), norm_g[1, 1], w_q[0], w_o[0], ka, vt, km,
                    invf[0, 0:ROPE_DIM // 2].reshape(ROPE_DIM // 2, 1))
    x2 = _ffn_call(x3.reshape(bsz * seq, D_MODEL), norm_g[1, 2], ffn_w_in, ffn_w_out, 1, 1,
                   final_g=final_g)
    return x2.reshape(bsz, seq, D_MODEL)
```

```python
import functools

import jax
import jax.numpy as jnp
import numpy as np
from jax import lax
from jax.experimental import pallas as pl
from jax.experimental.pallas import tpu as pltpu

F32 = jnp.float32
BF16 = jnp.bfloat16

D_MODEL = 1024
D_FF = 2816
RMS_EPS = 1e-6
S5_GROUP = 16
S5_GROUPS = D_MODEL // S5_GROUP
S5_STATE = 64
N_HEADS = 8
HEAD_DIM = 128
N_KV_HEADS = 2
KV_GROUP = N_HEADS // N_KV_HEADS
ROPE_DIM = HEAD_DIM // 4
ROPE_THETA = 500000.0
MOBA_BLOCK = 256
MOBA_TOPK = 3
NEG_INF = -1e30
SCORE_FLOOR = 0.5 * NEG_INF
LOG2E = 1.4426950408889634

LANES = 128
SUBLANES = 8
VMEM_LIMIT = 56 * 1024 * 1024
FFN_VMEM_LIMIT = 60 * 1024 * 1024

FFN_ROWS = 512
FFN_COLS = 256

ATT_GROUP = 4
V_ROWS = HEAD_DIM + 2 * SUBLANES

S5_CHUNK = 256
S5_SUBSEQ = SUBLANES
S5_STEPS = S5_CHUNK // S5_SUBSEQ
S5_LANE_BLOCKS = D_MODEL // LANES
S5_GROUPS_PER_BLOCK = LANES // S5_GROUP
S5_HALF = S5_GROUPS_PER_BLOCK * S5_STATE
S5_NSTATE = S5_GROUPS * S5_STATE
S5_GLU_COLS = 2 * D_MODEL // S5_LANE_BLOCKS

def _rms(x, g):
    ms = jnp.mean(x * x, axis=-1, keepdims=True)
    return (x * lax.rsqrt(ms + RMS_EPS)) * g


def _dot(a, b):
    return jnp.dot(a, b, preferred_element_type=F32)


def _resident(shape):
    nd = len(shape)
    return pl.BlockSpec(shape, lambda *_: (0,) * nd, pipeline_mode=pl.Buffered(1))


def _ffn_body(*refs, mode, nblk, layer, idx):
    x_ref, g_ref, win_hbm, wout_hbm = refs[:4]
    win_ref, wout_ref, sem = refs[-3:]
    refs = refs[:-3]
    n_chunks = D_FF // FFN_COLS

    def chunk_copies(c):
        lo = c * FFN_COLS
        cols = [pl.ds(lo, FFN_COLS), pl.ds(D_FF + lo, FFN_COLS)]
        cps = [pltpu.make_async_copy(win_hbm.at[layer, idx, :, cc], win_ref.at[:, cc], sem.at[k, c])
               for k, cc in enumerate(cols)]
        rows = pl.ds(lo, FFN_COLS)
        return cps + [pltpu.make_async_copy(wout_hbm.at[layer, idx, rows, :], wout_ref.at[rows, :], sem.at[2, c])]

    def run(wait_for_weights):
        x = x_ref[...]
        h = _rms(x, g_ref[...]).astype(BF16)
        acc = jnp.zeros(x.shape, F32)
        for c in range(n_chunks):
            lo = c * FFN_COLS
            if wait_for_weights:
                for cp in chunk_copies(c):
                    cp.wait()
            gate = _dot(h, win_ref[:, lo:lo + FFN_COLS].astype(BF16))
            up = _dot(h, win_ref[:, D_FF + lo:D_FF + lo + FFN_COLS].astype(BF16))
            act = (gate * jax.nn.sigmoid(gate)) * up
            acc = acc + _dot(act.astype(BF16), wout_ref[lo:lo + FFN_COLS, :].astype(BF16))
        y = x + 0.5 * acc
        if mode == 'final':
            fg_ref, o_ref = refs[4:]
            y = _rms(y, fg_ref[...])
        elif mode == 'kv':
            kvg_ref, wkv_ref, invf_ref, tab_ref, o_ref, ka_ref, vt_ref, km_ref = refs[4:]
            for sub in range(FFN_ROWS // MOBA_BLOCK):
                rows = slice(sub * MOBA_BLOCK, (sub + 1) * MOBA_BLOCK)
                blk = (pl.program_id(0) * (FFN_ROWS // MOBA_BLOCK) + sub) % nblk
                _kv_tile(y[rows], blk, kvg_ref[...], wkv_ref, invf_ref[...], tab_ref,
                         ka_ref, vt_ref, km_ref, sub)
        else:
            o_ref, = refs[4:]
        o_ref[...] = y

    @pl.when(pl.program_id(0) == 0)
    def _():
        for c in range(n_chunks):
            for cp in chunk_copies(c):
                cp.start()
        run(True)

    @pl.when(pl.program_id(0) > 0)
    def _():
        run(False)


def _ffn_call(x2, g, w_in_all, w_out_all, layer, idx, final_g=None, kv=None, nblk=None):
    t = x2.shape[0]
    mode = 'final' if final_g is not None else ('kv' if kv is not None else 'plain')
    row_spec = pl.BlockSpec((FFN_ROWS, D_MODEL), lambda i: (i, 0))
    in_specs = [row_spec, _resident((1, D_MODEL)),
                pl.BlockSpec(memory_space=pl.ANY), pl.BlockSpec(memory_space=pl.ANY)]
    args = [x2, g.reshape(1, D_MODEL), w_in_all, w_out_all]
    out_specs = row_spec
    out_shape = jax.ShapeDtypeStruct((t, D_MODEL), F32)
    if mode == 'final':
        in_specs.append(_resident((1, D_MODEL)))
        args.append(final_g.reshape(1, D_MODEL))
    elif mode == 'kv':
        kvg, wkv, invf, rope_tab = kv
        kdim = N_KV_HEADS * HEAD_DIM
        sub = FFN_ROWS // MOBA_BLOCK
        ntile = t // MOBA_BLOCK
        in_specs += [_resident((1, D_MODEL)), _resident((D_MODEL, 2 * kdim)),
                     _resident((1, HEAD_DIM)), _resident(rope_tab.shape)]
        args += [kvg.reshape(1, D_MODEL), wkv, invf, rope_tab]
        out_specs = [row_spec,
                     pl.BlockSpec((N_KV_HEADS, FFN_ROWS, 2 * HEAD_DIM), lambda i: (0, i, 0)),
                     pl.BlockSpec((sub, N_KV_HEADS, V_ROWS, MOBA_BLOCK), lambda i: (i, 0, 0, 0)),
                     pl.BlockSpec((sub, 1, kdim), lambda i: (i, 0, 0))]
        out_shape = [out_shape,
                     jax.ShapeDtypeStruct((N_KV_HEADS, t, 2 * HEAD_DIM), BF16),
                     jax.ShapeDtypeStruct((ntile, N_KV_HEADS, V_ROWS, MOBA_BLOCK), BF16),
                     jax.ShapeDtypeStruct((ntile, 1, kdim), F32)]
    return pl.pallas_call(
        functools.partial(_ffn_body, mode=mode, nblk=nblk, layer=layer, idx=idx),
        grid=(t // FFN_ROWS,),
        in_specs=in_specs,
        out_specs=out_specs,
        out_shape=out_shape,
        scratch_shapes=[pltpu.VMEM((D_MODEL, 2 * D_FF), F32),
                        pltpu.VMEM((D_FF, D_MODEL), F32),
                        pltpu.SemaphoreType.DMA((3, D_FF // FFN_COLS))],
        compiler_params=pltpu.CompilerParams(
            dimension_semantics=("arbitrary",), vmem_limit_bytes=FFN_VMEM_LIMIT),
        name="ffn_" + mode,
    )(*args)


def _s5_prep_body(lr_ref, li_ref, ls_ref, bre_ref, bim_ref, cim_ref,
                  ar_ref, ai_ref, amr_ref, ami_ref, pr_ref, pi_ref, btr_ref, bti_ref, cneg_ref):
    lr = lr_ref[...]
    li = li_ref[...]
    dt = jnp.exp(ls_ref[...])
    mag = jnp.exp(lr * dt)
    abar_re = mag * jnp.cos(li * dt)
    abar_im = mag * jnp.sin(li * dt)
    ar_ref[...] = abar_re
    ai_ref[...] = abar_im
    nr, ni = abar_re - 1.0, abar_im
    den = lr * lr + li * li
    coef_re = (nr * lr + ni * li) / den
    coef_im = (ni * lr - nr * li) / den
    k = (lax.broadcasted_iota(jnp.int32, (S5_STEPS, 1), 0) + 1).astype(F32)
    pmag = jnp.exp((lr * dt) * k)
    pang = (li * dt) * k
    pr = pmag * jnp.cos(pang)
    pi = pmag * jnp.sin(pang)
    for j in range(S5_LANE_BLOCKS):
        cols = slice(j * S5_HALF, (j + 1) * S5_HALF)
        for step in range(S5_STEPS):
            rows = slice(step * SUBLANES, (step + 1) * SUBLANES)
            pr_ref[j, rows, :] = jnp.broadcast_to(pr[step:step + 1, cols], (SUBLANES, S5_HALF))
            pi_ref[j, rows, :] = jnp.broadcast_to(pi[step:step + 1, cols], (SUBLANES, S5_HALF))
    amr_ref[...] = pr[S5_STEPS - 1:S5_STEPS, :]
    ami_ref[...] = pi[S5_STEPS - 1:S5_STEPS, :]
    bre = bre_ref[...]
    bim = bim_ref[...]
    btr_ref[...] = coef_re * bre - coef_im * bim
    bti_ref[...] = coef_re * bim + coef_im * bre
    cneg_ref[...] = -cim_ref[...]


def _s5_prep_call(a_re, a_im, log_step, b_re, b_im, c_im):
    n = S5_NSTATE
    row = lambda v: v.reshape(1, n)
    chan_major = lambda v: v.transpose(2, 0, 1).reshape(S5_GROUP, n)
    ls = jnp.repeat(log_step, S5_STATE)
    outs = pl.pallas_call(
        _s5_prep_body,
        out_shape=[jax.ShapeDtypeStruct((1, n), F32)] * 4
        + [jax.ShapeDtypeStruct((S5_LANE_BLOCKS, S5_CHUNK, S5_HALF), F32)] * 2
        + [jax.ShapeDtypeStruct((S5_GROUP, n), F32)] * 3,
        name="s5_prep",
    )(row(a_re), row(a_im), row(ls), chan_major(b_re), chan_major(b_im),
      c_im.transpose(1, 0, 2).reshape(S5_GROUP, n))
    return outs


def _s5_body(x_ref, xprev_ref, g_ref, perm_ref, permt_ref, bblk_ref, cblk_ref, ar_ref, ai_ref,
             amr_ref, ami_ref, pr_ref, pi_ref, d_ref, wglu_ref, o_ref,
             xs_ref, st_ref, c_ref, hb_ref, y_ref, *, nchunk):
    step = pl.program_id(0)

    @pl.when(step % nchunk == 0)
    def _():
        st_ref[...] = jnp.zeros(st_ref.shape, F32)

    @pl.when(step == 0)
    def _():
        y_ref[...] = jnp.zeros(y_ref.shape, F32)

    yp = y_ref[...]
    yp = 0.5 * yp * (1.0 + jnp.tanh(np.sqrt(2.0 / np.pi).astype(np.float32)
                                    * (yp + 0.044715 * (yp * yp * yp))))
    yn = _dot(permt_ref[...], yp.astype(BF16)).astype(BF16)

    def glu_piece(c):
        cols = slice(c * S5_GLU_COLS, (c + 1) * S5_GLU_COLS)
        gcols = slice(D_MODEL + c * S5_GLU_COLS, D_MODEL + (c + 1) * S5_GLU_COLS)
        val = _dot(yn, wglu_ref[:, cols])
        gate = _dot(yn, wglu_ref[:, gcols])
        o_ref[:, cols] = xprev_ref[:, cols] + val * jax.nn.sigmoid(gate)

    x = x_ref[...]
    u = _rms(x, g_ref[...])
    u_hi = u.astype(BF16)
    u_lo = (u - u_hi.astype(F32)).astype(BF16)
    perm = perm_ref[...]
    up_hi = _dot(perm, u_hi)
    up = up_hi + _dot(perm, u_lo)
    ub = up_hi.astype(BF16)

    last = SUBLANES * (S5_STEPS - 1)
    nb = S5_LANE_BLOCKS
    re, im = slice(0, S5_HALF), slice(S5_HALF, 2 * S5_HALF)

    def project_in(j):
        xs_ref[j] = _dot(ub[:, j * LANES:(j + 1) * LANES], bblk_ref[j])

    for j in range(nb):
        project_in(j)

    def scan_steps(j):
        a_r = jnp.broadcast_to(ar_ref[j], (SUBLANES, S5_HALF))
        a_i = jnp.broadcast_to(ai_ref[j], (SUBLANES, S5_HALF))
        state = [xs_ref[j, 0:SUBLANES, re], xs_ref[j, 0:SUBLANES, im]]

        def step(k):
            rows = slice(k * SUBLANES, (k + 1) * SUBLANES)
            h_r, h_i = state
            state[0] = a_r * h_r - a_i * h_i + xs_ref[j, rows, re]
            state[1] = a_r * h_i + a_i * h_r + xs_ref[j, rows, im]
            xs_ref[j, rows, re] = state[0]
            xs_ref[j, rows, im] = state[1]

        return [functools.partial(step, k) for k in range(1, S5_STEPS)]

    def entering_states(j):
        am_r, am_i = amr_ref[j], ami_ref[j]
        c_r, c_i = st_ref[j, :, re], st_ref[j, :, im]
        for i in range(S5_SUBSEQ):
            c_ref[j, i:i + 1, re] = c_r
            c_ref[j, i:i + 1, im] = c_i
            e_r = xs_ref[j, last + i:last + i + 1, re]
            e_i = xs_ref[j, last + i:last + i + 1, im]
            c_r, c_i = am_r * c_r - am_i * c_i + e_r, am_r * c_i + am_i * c_r + e_i
        st_ref[j, :, re] = c_r
        st_ref[j, :, im] = c_i

    def fix_steps(j):
        cc_r = jnp.concatenate([c_ref[j, :, re]] * 2, axis=0)
        cc_i = jnp.concatenate([c_ref[j, :, im]] * 2, axis=0)

        def step(k):
            rows = slice(2 * k * SUBLANES, 2 * (k + 1) * SUBLANES)
            p_r, p_i = pr_ref[j, rows, :], pi_ref[j, rows, :]
            t_r = xs_ref[j, rows, re] + (p_r * cc_r - p_i * cc_i)
            t_i = xs_ref[j, rows, im] + (p_r * cc_i + p_i * cc_r)
            hb_ref[j, rows, :] = jnp.concatenate([t_r, t_i], axis=1).astype(BF16)

        return [functools.partial(step, k) for k in range(S5_STEPS // 2)]

    ys = []
    for j in range(nb + 1):
        if j % 2 == 0 and j < nb:
            glu_piece(j // 2)
        p1 = scan_steps(j) if j < nb else []
        p2 = fix_steps(j - 1) if j >= 1 else []
        for k in range(max(len(p2), (len(p1) + 1) // 2)):
            for f in p1[2 * k:2 * k + 2]:
                f()
            if k < len(p2):
                p2[k]()
        if j < nb:
            entering_states(j)
        if j >= 1:
            ys.append(_dot(hb_ref[j - 1], cblk_ref[j - 1]))

    y_ref[...] = jnp.concatenate(ys, axis=1) + d_ref[...] * up


def _s5_call(x2, bsz, seq, g, prep, c_re, d_skip, w_glu):
    ar, ai, amr, ami, pr, pi, btr, bti, cneg = prep
    nb, gb, ns = S5_LANE_BLOCKS, S5_GROUPS_PER_BLOCK, S5_STATE

    def per_block(v, rows):
        return v.reshape(rows, nb, S5_HALF).transpose(1, 0, 2)

    state_group = jnp.arange(2 * S5_HALF) % S5_HALF // ns
    own_group = state_group[None, :] == jnp.arange(gb)[:, None]
    bt = jnp.stack([btr, bti], axis=1).reshape(S5_GROUP, 2, nb, S5_HALF)
    bt = bt.transpose(2, 0, 1, 3).reshape(nb, 1, S5_GROUP, 2 * S5_HALF)
    bblk = jnp.where(own_group[None, :, None, :], bt, 0.0).reshape(nb, LANES, 2 * S5_HALF).astype(BF16)
    cmat = jnp.stack([c_re.transpose(1, 0, 2).reshape(S5_GROUP, S5_NSTATE), cneg], axis=1)
    cmat = cmat.reshape(S5_GROUP, 2, nb, S5_HALF).transpose(2, 1, 3, 0)
    cmat = cmat.reshape(nb, 2 * S5_HALF, 1, S5_GROUP)
    cblk = jnp.where(own_group.T[None, :, :, None], cmat, 0.0).reshape(nb, 2 * S5_HALF, LANES).astype(BF16)

    r = np.arange(S5_CHUNK)
    perm_np = np.zeros((S5_CHUNK, S5_CHUNK), np.float32)
    perm_np[r, (r % SUBLANES) * S5_STEPS + r // SUBLANES] = 1.0
    perm = jnp.asarray(perm_np, BF16)
    permt = jnp.asarray(perm_np.T, BF16)

    nchunk = seq // S5_CHUNK
    total = bsz * nchunk
    cur_spec = pl.BlockSpec((S5_CHUNK, D_MODEL), lambda s: (jnp.minimum(s, total - 1), 0))
    prev_spec = pl.BlockSpec((S5_CHUNK, D_MODEL), lambda s: (jnp.maximum(s - 1, 0), 0))
    in_specs = [
        cur_spec, prev_spec, _resident((1, D_MODEL)),
        _resident((S5_CHUNK, S5_CHUNK)), _resident((S5_CHUNK, S5_CHUNK)),
        _resident((nb, LANES, 2 * S5_HALF)), _resident((nb, 2 * S5_HALF, LANES)),
        _resident((nb, 1, S5_HALF)), _resident((nb, 1, S5_HALF)),
        _resident((nb, 1, S5_HALF)), _resident((nb, 1, S5_HALF)),
        _resident((nb, S5_CHUNK, S5_HALF)), _resident((nb, S5_CHUNK, S5_HALF)),
        _resident((1, D_MODEL)), _resident((D_MODEL, 2 * D_MODEL)),
    ]
    return pl.pallas_call(
        functools.partial(_s5_body, nchunk=nchunk),
        grid=(total + 1,),
        in_specs=in_specs,
        out_specs=prev_spec,
        out_shape=jax.ShapeDtypeStruct(x2.shape, F32),
        scratch_shapes=[
            pltpu.VMEM((nb, S5_CHUNK, 2 * S5_HALF), F32),
            pltpu.VMEM((nb, 1, 2 * S5_HALF), F32),
            pltpu.VMEM((nb, S5_SUBSEQ, 2 * S5_HALF), F32),
            pltpu.VMEM((nb, S5_CHUNK, 2 * S5_HALF), BF16),
            pltpu.VMEM((S5_CHUNK, D_MODEL), F32),
        ],
        compiler_params=pltpu.CompilerParams(
            dimension_semantics=("arbitrary",), vmem_limit_bytes=VMEM_LIMIT),
        name="s5",
    )(x2, x2, g.reshape(1, D_MODEL), perm, permt, bblk, cblk,
      per_block(ar, 1), per_block(ai, 1), per_block(amr, 1), per_block(ami, 1),
      pr, pi,
      d_skip.reshape(1, D_MODEL), w_glu.astype(BF16))


def _rope_tables(pos0, invf, tab_ref):
    ang0 = pos0.astype(F32) * invf
    c0, s0 = jnp.cos(ang0), jnp.sin(ang0)
    cos_t = c0 * tab_ref[0] - s0 * tab_ref[1]
    sin_t = s0 * tab_ref[2] + c0 * tab_ref[3]
    return cos_t, sin_t


def _rope_head(xh, cos_t, sin_t, low_half):
    half = ROPE_DIM // 2
    swapped = jnp.where(low_half, pltpu.roll(xh, LANES - half, axis=1), pltpu.roll(xh, half, axis=1))
    return xh * cos_t + swapped * sin_t


def _rope_consts():
    half = ROPE_DIM // 2
    inv_freq = ROPE_THETA ** (-jnp.arange(0, ROPE_DIM, 2, dtype=F32) / ROPE_DIM)
    pad = jnp.zeros((HEAD_DIM - ROPE_DIM,), F32)
    invf = jnp.concatenate([inv_freq, inv_freq, pad]).reshape(1, HEAD_DIM)
    sign = jnp.concatenate([-jnp.ones((half,), F32), jnp.ones((half,), F32), pad]).reshape(1, HEAD_DIM)
    ang_r = jnp.arange(MOBA_BLOCK, dtype=F32)[:, None] * invf
    cos_r, sin_r = jnp.cos(ang_r), jnp.sin(ang_r)
    tables = jnp.stack([cos_r, sin_r, cos_r * sign, sin_r * sign])
    return invf, tables


def _kv_tile(x, blk, g, wkv_ref, invf, tab_ref, ka_ref, vt_ref, km_ref, sub):
    rows = slice(sub * MOBA_BLOCK, (sub + 1) * MOBA_BLOCK)
    h = _rms(x, g).astype(BF16)
    kv = _dot(h, wkv_ref[...])
    kdim = N_KV_HEADS * HEAD_DIM
    cos_t, sin_t = _rope_tables(blk * MOBA_BLOCK, invf, tab_ref)
    lane = lax.broadcasted_iota(jnp.int32, (MOBA_BLOCK, HEAD_DIM), 1)
    low_half = lane < ROPE_DIM // 2
    onehot = jnp.where(lane == blk, 1.0, 0.0).astype(BF16)
    pad_row = lax.broadcasted_iota(jnp.int32, (V_ROWS - HEAD_DIM, MOBA_BLOCK), 0)
    ones_rows = jnp.where(pad_row == 0, 1.0, 0.0).astype(BF16)
    means = []
    for hh in range(N_KV_HEADS):
        kh = _rope_head(kv[:, hh * HEAD_DIM:(hh + 1) * HEAD_DIM], cos_t, sin_t, low_half)
        means.append(jnp.mean(kh, axis=0, keepdims=True))
        ka_ref[hh, rows, :] = jnp.concatenate([kh.astype(BF16), onehot], axis=1)
        vt = kv[:, kdim + hh * HEAD_DIM:kdim + (hh + 1) * HEAD_DIM].T.astype(BF16)
        vt_ref[sub, hh] = jnp.concatenate([vt, ones_rows], axis=0)
    km_ref[sub] = jnp.concatenate(means, axis=1)


def _split_bf16(v):
    hi = v.astype(BF16)
    return hi, (v - hi.astype(F32)).astype(BF16)


def _attn_body(x_ref, g_ref, wq_ref, wo_ref, ka_ref, vt_ref, km_ref, invf_ref, o_ref,
               qa_ref, acc_ref, s0_ref, s1_ref, pb_ref, *, nblk):
    own = pl.program_id(1)
    items = KV_GROUP * MOBA_BLOCK
    group_keys = ATT_GROUP * MOBA_BLOCK
    half = ROPE_DIM // 2
    x = x_ref[0]
    h = _rms(x, g_ref[...]).astype(BF16)
    q = _dot(h, wq_ref[...])
    pos = (own * MOBA_BLOCK + lax.broadcasted_iota(jnp.int32, (1, MOBA_BLOCK), 1)).astype(F32)
    ang = invf_ref[...] * pos
    cos_t, sin_t = jnp.cos(ang), jnp.sin(ang)

    def head_t(i):
        qt = q[:, i * HEAD_DIM:(i + 1) * HEAD_DIM].T
        x1, x2 = qt[0:half], qt[half:2 * half]
        rot = jnp.concatenate([x1 * cos_t - x2 * sin_t, x2 * cos_t + x1 * sin_t, qt[2 * half:]], axis=0)
        return rot * (HEAD_DIM ** -0.5)

    blk = lax.broadcasted_iota(jnp.int32, (nblk, items), 0)
    blk_f = blk.astype(F32)
    past = blk < own
    causal = (lax.broadcasted_iota(jnp.int32, (MOBA_BLOCK, items), 0)
              <= lax.broadcasted_iota(jnp.int32, (MOBA_BLOCK, items), 1) % MOBA_BLOCK)
    feat_pad = jnp.zeros((HEAD_DIM - nblk, items), BF16)

    kv_heads = range(N_KV_HEADS)
    for kh in kv_heads:
        qt = jnp.concatenate([head_t(kh * KV_GROUP + i) for i in range(KV_GROUP)], axis=1)
        q_hi, q_lo = _split_bf16(qt)
        k_hi, k_lo = _split_bf16(km_ref[0, kh])
        g_hi = _dot(jnp.concatenate([k_hi, k_lo], axis=0), q_hi)
        gate = g_hi[0:nblk] + (_dot(k_hi, q_lo) + g_hi[nblk:2 * nblk])
        cur = jnp.where(past, gate, -jnp.inf)
        bias = jnp.where(blk == own, 0.0, NEG_INF)
        for _ in range(MOBA_TOPK):
            best = jnp.max(cur, axis=0, keepdims=True)
            cand = jnp.where((cur == best) & (best > -jnp.inf), blk_f, float(nblk))
            pick = blk_f == jnp.min(cand, axis=0, keepdims=True)
            bias = jnp.where(pick, 0.0, bias)
            cur = jnp.where(pick, -jnp.inf, cur)
        q_feat = (qt * LOG2E).astype(BF16)
        qa_ref[kh] = jnp.concatenate([q_feat, bias.astype(BF16), feat_pad], axis=0)
        acc_ref[kh] = jnp.zeros(acc_ref.shape[1:], F32)

    def score_group(kh, gi):
        keys = pl.ds(pl.multiple_of(gi * group_keys, group_keys), group_keys)
        return _dot(ka_ref[kh, 0, keys, :], qa_ref[kh])

    def weights(kh, s_cur_ref, m_prev, own_slot=False):
        if own_slot:
            slabs = []
            for j in range(ATT_GROUP):
                keep = jnp.logical_or(causal, own % ATT_GROUP != j)
                slabs.append(jnp.where(keep, s_cur_ref[kh, j * MOBA_BLOCK:(j + 1) * MOBA_BLOCK, :], NEG_INF))
            s = jnp.concatenate(slabs, axis=0)
        else:
            s = s_cur_ref[kh]
        m_new = jnp.maximum(m_prev, jnp.max(s, axis=0, keepdims=True))
        pb_ref[kh] = jnp.exp2((s - m_new).astype(BF16))
        return m_new, jnp.exp2(m_prev - m_new)

    def accumulate(kh, gi, alpha):
        n0 = gi * ATT_GROUP
        pv = _dot(vt_ref[0, n0, kh], pb_ref[kh, 0:MOBA_BLOCK, :])
        for j in range(1, ATT_GROUP):
            pv = pv + _dot(vt_ref[0, n0 + j, kh], pb_ref[kh, j * MOBA_BLOCK:(j + 1) * MOBA_BLOCK, :])
        acc_ref[kh] = alpha * acc_ref[kh] + pv

    own_group = own // ATT_GROUP

    def softmax_steps(s_cur_ref, gi, ms, before=(None,) * N_KV_HEADS, tail=None, own_slot=False):
        out = []
        alpha_prev = None
        for kh in kv_heads:
            if before[kh] is not None:
                before[kh]()
            if alpha_prev is not None:
                accumulate(kh - 1, gi, alpha_prev)
            m_new, alpha_prev = weights(kh, s_cur_ref, ms[kh], own_slot)
            out.append(m_new)
        if tail is not None:
            tail()
        accumulate(N_KV_HEADS - 1, gi, alpha_prev)
        return tuple(out)

    def scorer(dst_ref, kh, gi):
        def run():
            dst_ref[kh] = score_group(kh, gi)
        return run

    m0 = jnp.full((1, items), SCORE_FLOOR, F32)
    for kh in kv_heads:
        s0_ref[kh] = score_group(kh, 0)

    def group_pair(pi, ms):
        g = 2 * pi
        nxt = [scorer(s0_ref, kh, g + 2) for kh in kv_heads]
        ms = softmax_steps(s0_ref, g, ms, before=[scorer(s1_ref, kh, g + 1) for kh in kv_heads],
                           tail=nxt[0])
        return softmax_steps(s1_ref, g + 1, ms, before=[None] + nxt[1:])

    ms = lax.fori_loop(0, own_group // 2, group_pair, (m0,) * N_KV_HEADS)

    @pl.when(own_group % 2 == 0)
    def _():
        softmax_steps(s0_ref, own_group, ms, own_slot=True)

    @pl.when(own_group % 2 == 1)
    def _():
        mids = softmax_steps(s0_ref, own_group - 1, ms,
                             before=[scorer(s1_ref, kh, own_group) for kh in kv_heads])
        softmax_steps(s1_ref, own_group, mids, own_slot=True)

    outs = []
    for kh in kv_heads:
        ot = acc_ref[kh, 0:HEAD_DIM, :] / acc_ref[kh, HEAD_DIM:HEAD_DIM + 1, :]
        outs.extend(ot[:, i * MOBA_BLOCK:(i + 1) * MOBA_BLOCK].T for i in range(KV_GROUP))

    attn = jnp.concatenate(outs, axis=1).astype(BF16)
    o_ref[0] = x + _dot(attn, wo_ref[...])


def _attn_call(x3, g, w_q, w_o, ka, vt, km, invf_col):
    bsz, seq, _ = x3.shape
    nblk = seq // MOBA_BLOCK
    items = KV_GROUP * MOBA_BLOCK
    x_spec = pl.BlockSpec((1, MOBA_BLOCK, D_MODEL), lambda b, i: (b, i, 0))
    return pl.pallas_call(
        functools.partial(_attn_body, nblk=nblk),
        grid=(bsz, nblk),
        in_specs=[x_spec, _resident((1, D_MODEL)),
                  _resident((D_MODEL, D_MODEL)), _resident((D_MODEL, D_MODEL)),
                  pl.BlockSpec((N_KV_HEADS, 1, seq, 2 * HEAD_DIM), lambda b, i: (0, b, 0, 0),
                               pipeline_mode=pl.Buffered(1)),
                  pl.BlockSpec((1, nblk, N_KV_HEADS, V_ROWS, MOBA_BLOCK),
                               lambda b, i: (b, 0, 0, 0, 0), pipeline_mode=pl.Buffered(1)),
                  pl.BlockSpec((1, N_KV_HEADS, nblk, HEAD_DIM), lambda b, i: (b, 0, 0, 0)),
                  _resident((ROPE_DIM // 2, 1))],
        out_specs=x_spec,
        out_shape=jax.ShapeDtypeStruct(x3.shape, F32),
        scratch_shapes=[
            pltpu.VMEM((N_KV_HEADS, 2 * HEAD_DIM, items), BF16),
            pltpu.VMEM((N_KV_HEADS, V_ROWS, items), F32),
            pltpu.VMEM((N_KV_HEADS, ATT_GROUP * MOBA_BLOCK, items), F32),
            pltpu.VMEM((N_KV_HEADS, ATT_GROUP * MOBA_BLOCK, items), F32),
            pltpu.VMEM((N_KV_HEADS, ATT_GROUP * MOBA_BLOCK, items), BF16)],
        compiler_params=pltpu.CompilerParams(
            dimension_semantics=("arbitrary", "arbitrary"), vmem_limit_bytes=VMEM_LIMIT),
        name="moba_attn",
    )(x3, g.reshape(1, D_MODEL), w_q.astype(BF16), w_o.astype(BF16), ka, vt, km, invf_col)


def kernel(x, norm_g, ffn_w_in, ffn_w_out, s5_a_re, s5_a_im, s5_log_step, s5_b_re, s5_b_im,
           s5_c_re, s5_c_im, s5_d, s5_w_glu, kv_norm_g, w_k, w_v, w_q, w_o, final_g):
    bsz, seq, _ = x.shape
    assert seq % S5_CHUNK == 0 and seq % MOBA_BLOCK == 0
    assert (seq // MOBA_BLOCK) % (2 * SUBLANES) == 0 and seq // MOBA_BLOCK <= HEAD_DIM
    assert (seq // MOBA_BLOCK) % ATT_GROUP == 0
    assert (bsz * seq) % FFN_ROWS == 0 and FFN_ROWS % MOBA_BLOCK == 0
    nblk = seq // MOBA_BLOCK
    x2 = x.reshape(bsz * seq, D_MODEL)
    invf, rope_tab = _rope_consts()

    wkv = jnp.concatenate([w_k, w_v], axis=1).astype(BF16)

    x2 = _ffn_call(x2, norm_g[0, 0], ffn_w_in, ffn_w_out, 0, 0)
    prep = _s5_prep_call(s5_a_re[0], s5_a_im[0], s5_log_step[0], s5_b_re[0], s5_b_im[0], s5_c_im[0])
    x2 = _s5_call(x2, bsz, seq, norm_g[0, 1], prep, s5_c_re[0], s5_d[0], s5_w_glu[0])
    x2, ka, vt, km = _ffn_call(x2, norm_g[0, 2], ffn_w_in, ffn_w_out, 0, 1,
                               kv=(kv_norm_g, wkv, invf, rope_tab), nblk=nblk)
    ka = ka.reshape(N_KV_HEADS, bsz, seq, 2 * HEAD_DIM)
    vt = vt.reshape(bsz, nblk, N_KV_HEADS, V_ROWS, MOBA_BLOCK)
    km = km.reshape(bsz, nblk, N_KV_HEADS, HEAD_DIM).transpose(0, 2, 1, 3)

    x2 = _ffn_call(x2, norm_g[1, 0], ffn_w_in, ffn_w_out, 1, 0)
    x3 = _attn_call(x2.reshape(bsz, seq, D_MODEL), norm_g[1, 1], w_q[0], w_o[0], ka, vt, km,
                    invf[0, 0:ROPE_DIM // 2].reshape(ROPE_DIM // 2, 1))
    x2 = _ffn_call(x3.reshape(bsz * seq, D_MODEL), norm_g[1, 2], ffn_w_in, ffn_w_out, 1, 1,
                   final_g=final_g)
    return x2.reshape(bsz, seq, D_MODEL)
```

```python
import functools

import jax
import jax.numpy as jnp
import numpy as np
from jax import lax
from jax.experimental import pallas as pl
from jax.experimental.pallas import tpu as pltpu

F32 = jnp.float32
BF16 = jnp.bfloat16

D_MODEL = 1024
D_FF = 2816
RMS_EPS = 1e-6
S5_GROUP = 16
S5_GROUPS = D_MODEL // S5_GROUP
S5_STATE = 64
N_HEADS = 8
HEAD_DIM = 128
N_KV_HEADS = 2
KV_GROUP = N_HEADS // N_KV_HEADS
ROPE_DIM = HEAD_DIM // 4
ROPE_THETA = 500000.0
MOBA_BLOCK = 256
MOBA_TOPK = 3
NEG_INF = -1e30
SCORE_FLOOR = 0.5 * NEG_INF
LOG2E = 1.4426950408889634

LANES = 128
SUBLANES = 8
VMEM_LIMIT = 56 * 1024 * 1024
FFN_VMEM_LIMIT = 60 * 1024 * 1024

FFN_ROWS = 512
FFN_COLS = 256

ATT_GROUP = 4
V_ROWS = HEAD_DIM + 2 * SUBLANES

S5_CHUNK = 256
S5_SUBSEQ = SUBLANES
S5_STEPS = S5_CHUNK // S5_SUBSEQ
S5_LANE_BLOCKS = D_MODEL // LANES
S5_GROUPS_PER_BLOCK = LANES // S5_GROUP
S5_HALF = S5_GROUPS_PER_BLOCK * S5_STATE
S5_NSTATE = S5_GROUPS * S5_STATE
S5_GLU_COLS = 2 * D_MODEL // S5_LANE_BLOCKS

def _rms(x, g):
    ms = jnp.mean(x * x, axis=-1, keepdims=True)
    return (x * lax.rsqrt(ms + RMS_EPS)) * g


def _dot(a, b):
    return jnp.dot(a, b, preferred_element_type=F32)


def _resident(shape):
    nd = len(shape)
    return pl.BlockSpec(shape, lambda *_: (0,) * nd, pipeline_mode=pl.Buffered(1))


def _ffn_body(*refs, mode, nblk, layer, idx):
    x_ref, g_ref, win_hbm, wout_hbm = refs[:4]
    win_ref, wout_ref, sem = refs[-3:]
    refs = refs[:-3]
    n_chunks = D_FF // FFN_COLS

    def chunk_copies(c):
        lo = c * FFN_COLS
        cols = [pl.ds(lo, FFN_COLS), pl.ds(D_FF + lo, FFN_COLS)]
        cps = [pltpu.make_async_copy(win_hbm.at[layer, idx, :, cc], win_ref.at[:, cc], sem.at[k, c])
               for k, cc in enumerate(cols)]
        rows = pl.ds(lo, FFN_COLS)
        return cps + [pltpu.make_async_copy(wout_hbm.at[layer, idx, rows, :], wout_ref.at[rows, :], sem.at[2, c])]

    def run(wait_for_weights):
        x = x_ref[...]
        h = _rms(x, g_ref[...]).astype(BF16)
        acc = jnp.zeros(x.shape, F32)
        for c in range(n_chunks):
            lo = c * FFN_COLS
            if wait_for_weights:
                for cp in chunk_copies(c):
                    cp.wait()
            gate = _dot(h, win_ref[:, lo:lo + FFN_COLS].astype(BF16))
            up = _dot(h, win_ref[:, D_FF + lo:D_FF + lo + FFN_COLS].astype(BF16))
            act = (gate * jax.nn.sigmoid(gate)) * up
            acc = acc + _dot(act.astype(BF16), wout_ref[lo:lo + FFN_COLS, :].astype(BF16))
        y = x + 0.5 * acc
        if mode == 'final':
            fg_ref, o_ref = refs[4:]
            y = _rms(y, fg_ref[...])
        elif mode == 'kv':
            kvg_ref, wkv_ref, invf_ref, tab_ref, o_ref, ka_ref, vt_ref, km_ref = refs[4:]
            for sub in range(FFN_ROWS // MOBA_BLOCK):
                rows = slice(sub * MOBA_BLOCK, (sub + 1) * MOBA_BLOCK)
                blk = (pl.program_id(0) * (FFN_ROWS // MOBA_BLOCK) + sub) % nblk
                _kv_tile(y[rows], blk, kvg_ref[...], wkv_ref, invf_ref[...], tab_ref,
                         ka_ref, vt_ref, km_ref, sub)
        else:
            o_ref, = refs[4:]
        o_ref[...] = y

    @pl.when(pl.program_id(0) == 0)
    def _():
        for c in range(n_chunks):
            for k, cp in enumerate(chunk_copies(c)):
                cp.start(priority=(c + k) % 2)
        run(True)

    @pl.when(pl.program_id(0) > 0)
    def _():
        run(False)


def _ffn_call(x2, g, w_in_all, w_out_all, layer, idx, final_g=None, kv=None, nblk=None):
    t = x2.shape[0]
    mode = 'final' if final_g is not None else ('kv' if kv is not None else 'plain')
    row_spec = pl.BlockSpec((FFN_ROWS, D_MODEL), lambda i: (i, 0))
    in_specs = [row_spec, _resident((1, D_MODEL)),
                pl.BlockSpec(memory_space=pl.ANY), pl.BlockSpec(memory_space=pl.ANY)]
    args = [x2, g.reshape(1, D_MODEL), w_in_all, w_out_all]
    out_specs = row_spec
    out_shape = jax.ShapeDtypeStruct((t, D_MODEL), F32)
    if mode == 'final':
        in_specs.append(_resident((1, D_MODEL)))
        args.append(final_g.reshape(1, D_MODEL))
    elif mode == 'kv':
        kvg, wkv, invf, rope_tab = kv
        kdim = N_KV_HEADS * HEAD_DIM
        sub = FFN_ROWS // MOBA_BLOCK
        ntile = t // MOBA_BLOCK
        in_specs += [_resident((1, D_MODEL)), _resident((D_MODEL, 2 * kdim)),
                     _resident((1, HEAD_DIM)), _resident(rope_tab.shape)]
        args += [kvg.reshape(1, D_MODEL), wkv, invf, rope_tab]
        out_specs = [row_spec,
                     pl.BlockSpec((N_KV_HEADS, FFN_ROWS, 2 * HEAD_DIM), lambda i: (0, i, 0)),
                     pl.BlockSpec((sub, N_KV_HEADS, V_ROWS, MOBA_BLOCK), lambda i: (i, 0, 0, 0)),
                     pl.BlockSpec((sub, 1, kdim), lambda i: (i, 0, 0))]
        out_shape = [out_shape,
                     jax.ShapeDtypeStruct((N_KV_HEADS, t, 2 * HEAD_DIM), BF16),
                     jax.ShapeDtypeStruct((ntile, N_KV_HEADS, V_ROWS, MOBA_BLOCK), BF16),
                     jax.ShapeDtypeStruct((ntile, 1, kdim), F32)]
    return pl.pallas_call(
        functools.partial(_ffn_body, mode=mode, nblk=nblk, layer=layer, idx=idx),
        grid=(t // FFN_ROWS,),
        in_specs=in_specs,
        out_specs=out_specs,
        out_shape=out_shape,
        scratch_shapes=[pltpu.VMEM((D_MODEL, 2 * D_FF), F32),
                        pltpu.VMEM((D_FF, D_MODEL), F32),
                        pltpu.SemaphoreType.DMA((3, D_FF // FFN_COLS))],
        compiler_params=pltpu.CompilerParams(
            dimension_semantics=("arbitrary",), vmem_limit_bytes=FFN_VMEM_LIMIT),
        name="ffn_" + mode,
    )(*args)


def _s5_prep_body(lr_ref, li_ref, ls_ref, bre_ref, bim_ref, cim_ref,
                  ar_ref, ai_ref, amr_ref, ami_ref, pr_ref, pi_ref, btr_ref, bti_ref, cneg_ref):
    lr = lr_ref[...]
    li = li_ref[...]
    dt = jnp.exp(ls_ref[...])
    mag = jnp.exp(lr * dt)
    abar_re = mag * jnp.cos(li * dt)
    abar_im = mag * jnp.sin(li * dt)
    ar_ref[...] = abar_re
    ai_ref[...] = abar_im
    nr, ni = abar_re - 1.0, abar_im
    den = lr * lr + li * li
    coef_re = (nr * lr + ni * li) / den
    coef_im = (ni * lr - nr * li) / den
    k = (lax.broadcasted_iota(jnp.int32, (S5_STEPS, 1), 0) + 1).astype(F32)
    pmag = jnp.exp((lr * dt) * k)
    pang = (li * dt) * k
    pr = pmag * jnp.cos(pang)
    pi = pmag * jnp.sin(pang)
    for j in range(S5_LANE_BLOCKS):
        cols = slice(j * S5_HALF, (j + 1) * S5_HALF)
        for step in range(S5_STEPS):
            rows = slice(step * SUBLANES, (step + 1) * SUBLANES)
            pr_ref[j, rows, :] = jnp.broadcast_to(pr[step:step + 1, cols], (SUBLANES, S5_HALF))
            pi_ref[j, rows, :] = jnp.broadcast_to(pi[step:step + 1, cols], (SUBLANES, S5_HALF))
    amr_ref[...] = pr[S5_STEPS - 1:S5_STEPS, :]
    ami_ref[...] = pi[S5_STEPS - 1:S5_STEPS, :]
    bre = bre_ref[...]
    bim = bim_ref[...]
    btr_ref[...] = coef_re * bre - coef_im * bim
    bti_ref[...] = coef_re * bim + coef_im * bre
    cneg_ref[...] = -cim_ref[...]


def _s5_prep_call(a_re, a_im, log_step, b_re, b_im, c_im):
    n = S5_NSTATE
    row = lambda v: v.reshape(1, n)
    chan_major = lambda v: v.transpose(2, 0, 1).reshape(S5_GROUP, n)
    ls = jnp.repeat(log_step, S5_STATE)
    outs = pl.pallas_call(
        _s5_prep_body,
        out_shape=[jax.ShapeDtypeStruct((1, n), F32)] * 4
        + [jax.ShapeDtypeStruct((S5_LANE_BLOCKS, S5_CHUNK, S5_HALF), F32)] * 2
        + [jax.ShapeDtypeStruct((S5_GROUP, n), F32)] * 3,
        name="s5_prep",
    )(row(a_re), row(a_im), row(ls), chan_major(b_re), chan_major(b_im),
      c_im.transpose(1, 0, 2).reshape(S5_GROUP, n))
    return outs


def _s5_body(x_ref, xprev_ref, g_ref, perm_ref, permt_ref, bblk_ref, cblk_ref, ar_ref, ai_ref,
             amr_ref, ami_ref, pr_ref, pi_ref, d_ref, wglu_ref, o_ref,
             xs_ref, st_ref, c_ref, hb_ref, y_ref, *, nchunk):
    step = pl.program_id(0)

    @pl.when(step % nchunk == 0)
    def _():
        st_ref[...] = jnp.zeros(st_ref.shape, F32)

    @pl.when(step == 0)
    def _():
        y_ref[...] = jnp.zeros(y_ref.shape, F32)

    yp = y_ref[...]
    yp = 0.5 * yp * (1.0 + jnp.tanh(np.sqrt(2.0 / np.pi).astype(np.float32)
                                    * (yp + 0.044715 * (yp * yp * yp))))
    yn = _dot(permt_ref[...], yp.astype(BF16)).astype(BF16)

    def glu_piece(c):
        cols = slice(c * S5_GLU_COLS, (c + 1) * S5_GLU_COLS)
        gcols = slice(D_MODEL + c * S5_GLU_COLS, D_MODEL + (c + 1) * S5_GLU_COLS)
        val = _dot(yn, wglu_ref[:, cols])
        gate = _dot(yn, wglu_ref[:, gcols])
        o_ref[:, cols] = xprev_ref[:, cols] + val * jax.nn.sigmoid(gate)

    x = x_ref[...]
    u = _rms(x, g_ref[...])
    u_hi = u.astype(BF16)
    u_lo = (u - u_hi.astype(F32)).astype(BF16)
    perm = perm_ref[...]
    up_hi = _dot(perm, u_hi)
    up = up_hi + _dot(perm, u_lo)
    ub = up_hi.astype(BF16)

    last = SUBLANES * (S5_STEPS - 1)
    nb = S5_LANE_BLOCKS
    re, im = slice(0, S5_HALF), slice(S5_HALF, 2 * S5_HALF)

    def project_in(j):
        xs_ref[j] = _dot(ub[:, j * LANES:(j + 1) * LANES], bblk_ref[j])

    for j in range(nb):
        project_in(j)

    def scan_steps(j):
        a_r = jnp.broadcast_to(ar_ref[j], (SUBLANES, S5_HALF))
        a_i = jnp.broadcast_to(ai_ref[j], (SUBLANES, S5_HALF))
        state = [xs_ref[j, 0:SUBLANES, re], xs_ref[j, 0:SUBLANES, im]]

        def step(k):
            rows = slice(k * SUBLANES, (k + 1) * SUBLANES)
            h_r, h_i = state
            state[0] = a_r * h_r - a_i * h_i + xs_ref[j, rows, re]
            state[1] = a_r * h_i + a_i * h_r + xs_ref[j, rows, im]
            xs_ref[j, rows, re] = state[0]
            xs_ref[j, rows, im] = state[1]

        return [functools.partial(step, k) for k in range(1, S5_STEPS)]

    def entering_states(j):
        am_r, am_i = amr_ref[j], ami_ref[j]
        c_r, c_i = st_ref[j, :, re], st_ref[j, :, im]
        for i in range(S5_SUBSEQ):
            c_ref[j, i:i + 1, re] = c_r
            c_ref[j, i:i + 1, im] = c_i
            e_r = xs_ref[j, last + i:last + i + 1, re]
            e_i = xs_ref[j, last + i:last + i + 1, im]
            c_r, c_i = am_r * c_r - am_i * c_i + e_r, am_r * c_i + am_i * c_r + e_i
        st_ref[j, :, re] = c_r
        st_ref[j, :, im] = c_i

    def fix_steps(j):
        cc_r = jnp.concatenate([c_ref[j, :, re]] * 2, axis=0)
        cc_i = jnp.concatenate([c_ref[j, :, im]] * 2, axis=0)

        def step(k):
            rows = slice(2 * k * SUBLANES, 2 * (k + 1) * SUBLANES)
            p_r, p_i = pr_ref[j, rows, :], pi_ref[j, rows, :]
            t_r = xs_ref[j, rows, re] + (p_r * cc_r - p_i * cc_i)
            t_i = xs_ref[j, rows, im] + (p_r * cc_i + p_i * cc_r)
            hb_ref[j, rows, :] = jnp.concatenate([t_r, t_i], axis=1).astype(BF16)

        return [functools.partial(step, k) for k in range(S5_STEPS // 2)]

    ys = []
    for j in range(nb + 1):
        if j % 2 == 0 and j < nb:
            glu_piece(j // 2)
        p1 = scan_steps(j) if j < nb else []
        p2 = fix_steps(j - 1) if j >= 1 else []
        for k in range(max(len(p2), (len(p1) + 1) // 2)):
            for f in p1[2 * k:2 * k + 2]:
                f()
            if k < len(p2):
                p2[k]()
        if j < nb:
            entering_states(j)
        if j >= 1:
            ys.append(_dot(hb_ref[j - 1], cblk_ref[j - 1]))

    y_ref[...] = jnp.concatenate(ys, axis=1) + d_ref[...] * up


def _s5_call(x2, bsz, seq, g, prep, c_re, d_skip, w_glu):
    ar, ai, amr, ami, pr, pi, btr, bti, cneg = prep
    nb, gb, ns = S5_LANE_BLOCKS, S5_GROUPS_PER_BLOCK, S5_STATE

    def per_block(v, rows):
        return v.reshape(rows, nb, S5_HALF).transpose(1, 0, 2)

    state_group = jnp.arange(2 * S5_HALF) % S5_HALF // ns
    own_group = state_group[None, :] == jnp.arange(gb)[:, None]
    bt = jnp.stack([btr, bti], axis=1).reshape(S5_GROUP, 2, nb, S5_HALF)
    bt = bt.transpose(2, 0, 1, 3).reshape(nb, 1, S5_GROUP, 2 * S5_HALF)
    bblk = jnp.where(own_group[None, :, None, :], bt, 0.0).reshape(nb, LANES, 2 * S5_HALF).astype(BF16)
    cmat = jnp.stack([c_re.transpose(1, 0, 2).reshape(S5_GROUP, S5_NSTATE), cneg], axis=1)
    cmat = cmat.reshape(S5_GROUP, 2, nb, S5_HALF).transpose(2, 1, 3, 0)
    cmat = cmat.reshape(nb, 2 * S5_HALF, 1, S5_GROUP)
    cblk = jnp.where(own_group.T[None, :, :, None], cmat, 0.0).reshape(nb, 2 * S5_HALF, LANES).astype(BF16)

    r = np.arange(S5_CHUNK)
    perm_np = np.zeros((S5_CHUNK, S5_CHUNK), np.float32)
    perm_np[r, (r % SUBLANES) * S5_STEPS + r // SUBLANES] = 1.0
    perm = jnp.asarray(perm_np, BF16)
    permt = jnp.asarray(perm_np.T, BF16)

    nchunk = seq // S5_CHUNK
    total = bsz * nchunk
    cur_spec = pl.BlockSpec((S5_CHUNK, D_MODEL), lambda s: (jnp.minimum(s, total - 1), 0))
    prev_spec = pl.BlockSpec((S5_CHUNK, D_MODEL), lambda s: (jnp.maximum(s - 1, 0), 0))
    in_specs = [
        cur_spec, prev_spec, _resident((1, D_MODEL)),
        _resident((S5_CHUNK, S5_CHUNK)), _resident((S5_CHUNK, S5_CHUNK)),
        _resident((nb, LANES, 2 * S5_HALF)), _resident((nb, 2 * S5_HALF, LANES)),
        _resident((nb, 1, S5_HALF)), _resident((nb, 1, S5_HALF)),
        _resident((nb, 1, S5_HALF)), _resident((nb, 1, S5_HALF)),
        _resident((nb, S5_CHUNK, S5_HALF)), _resident((nb, S5_CHUNK, S5_HALF)),
        _resident((1, D_MODEL)), _resident((D_MODEL, 2 * D_MODEL)),
    ]
    return pl.pallas_call(
        functools.partial(_s5_body, nchunk=nchunk),
        grid=(total + 1,),
        in_specs=in_specs,
        out_specs=prev_spec,
        out_shape=jax.ShapeDtypeStruct(x2.shape, F32),
        scratch_shapes=[
            pltpu.VMEM((nb, S5_CHUNK, 2 * S5_HALF), F32),
            pltpu.VMEM((nb, 1, 2 * S5_HALF), F32),
            pltpu.VMEM((nb, S5_SUBSEQ, 2 * S5_HALF), F32),
            pltpu.VMEM((nb, S5_CHUNK, 2 * S5_HALF), BF16),
            pltpu.VMEM((S5_CHUNK, D_MODEL), F32),
        ],
        compiler_params=pltpu.CompilerParams(
            dimension_semantics=("arbitrary",), vmem_limit_bytes=VMEM_LIMIT),
        name="s5",
    )(x2, x2, g.reshape(1, D_MODEL), perm, permt, bblk, cblk,
      per_block(ar, 1), per_block(ai, 1), per_block(amr, 1), per_block(ami, 1),
      pr, pi,
      d_skip.reshape(1, D_MODEL), w_glu.astype(BF16))


def _rope_tables(pos0, invf, tab_ref):
    ang0 = pos0.astype(F32) * invf
    c0, s0 = jnp.cos(ang0), jnp.sin(ang0)
    cos_t = c0 * tab_ref[0] - s0 * tab_ref[1]
    sin_t = s0 * tab_ref[2] + c0 * tab_ref[3]
    return cos_t, sin_t


def _rope_head(xh, cos_t, sin_t, low_half):
    half = ROPE_DIM // 2
    swapped = jnp.where(low_half, pltpu.roll(xh, LANES - half, axis=1), pltpu.roll(xh, half, axis=1))
    return xh * cos_t + swapped * sin_t


def _rope_consts():
    half = ROPE_DIM // 2
    inv_freq = ROPE_THETA ** (-jnp.arange(0, ROPE_DIM, 2, dtype=F32) / ROPE_DIM)
    pad = jnp.zeros((HEAD_DIM - ROPE_DIM,), F32)
    invf = jnp.concatenate([inv_freq, inv_freq, pad]).reshape(1, HEAD_DIM)
    sign = jnp.concatenate([-jnp.ones((half,), F32), jnp.ones((half,), F32), pad]).reshape(1, HEAD_DIM)
    ang_r = jnp.arange(MOBA_BLOCK, dtype=F32)[:, None] * invf
    cos_r, sin_r = jnp.cos(ang_r), jnp.sin(ang_r)
    tables = jnp.stack([cos_r, sin_r, cos_r * sign, sin_r * sign])
    return invf, tables


def _kv_tile(x, blk, g, wkv_ref, invf, tab_ref, ka_ref, vt_ref, km_ref, sub):
    rows = slice(sub * MOBA_BLOCK, (sub + 1) * MOBA_BLOCK)
    h = _rms(x, g).astype(BF16)
    kv = _dot(h, wkv_ref[...])
    kdim = N_KV_HEADS * HEAD_DIM
    cos_t, sin_t = _rope_tables(blk * MOBA_BLOCK, invf, tab_ref)
    lane = lax.broadcasted_iota(jnp.int32, (MOBA_BLOCK, HEAD_DIM), 1)
    low_half = lane < ROPE_DIM // 2
    onehot = jnp.where(lane == blk, 1.0, 0.0).astype(BF16)
    pad_row = lax.broadcasted_iota(jnp.int32, (V_ROWS - HEAD_DIM, MOBA_BLOCK), 0)
    ones_rows = jnp.where(pad_row == 0, 1.0, 0.0).astype(BF16)
    means = []
    for hh in range(N_KV_HEADS):
        kh = _rope_head(kv[:, hh * HEAD_DIM:(hh + 1) * HEAD_DIM], cos_t, sin_t, low_half)
        means.append(jnp.mean(kh, axis=0, keepdims=True))
        ka_ref[hh, rows, :] = jnp.concatenate([kh.astype(BF16), onehot], axis=1)
        vt = kv[:, kdim + hh * HEAD_DIM:kdim + (hh + 1) * HEAD_DIM].T.astype(BF16)
        vt_ref[sub, hh] = jnp.concatenate([vt, ones_rows], axis=0)
    km_ref[sub] = jnp.concatenate(means, axis=1)


def _split_bf16(v):
    hi = v.astype(BF16)
    return hi, (v - hi.astype(F32)).astype(BF16)


def _attn_body(x_ref, g_ref, wq_ref, wo_ref, ka_ref, vt_ref, km_ref, invf_ref, o_ref,
               qa_ref, acc_ref, s0_ref, s1_ref, pb_ref, *, nblk):
    own = pl.program_id(1)
    items = KV_GROUP * MOBA_BLOCK
    group_keys = ATT_GROUP * MOBA_BLOCK
    half = ROPE_DIM // 2
    x = x_ref[0]
    h = _rms(x, g_ref[...]).astype(BF16)
    q = _dot(h, wq_ref[...])
    pos = (own * MOBA_BLOCK + lax.broadcasted_iota(jnp.int32, (1, MOBA_BLOCK), 1)).astype(F32)
    ang = invf_ref[...] * pos
    cos_t, sin_t = jnp.cos(ang), jnp.sin(ang)

    def head_t(i):
        qt = q[:, i * HEAD_DIM:(i + 1) * HEAD_DIM].T
        x1, x2 = qt[0:half], qt[half:2 * half]
        rot = jnp.concatenate([x1 * cos_t - x2 * sin_t, x2 * cos_t + x1 * sin_t, qt[2 * half:]], axis=0)
        return rot * (HEAD_DIM ** -0.5)

    blk = lax.broadcasted_iota(jnp.int32, (nblk, items), 0)
    blk_f = blk.astype(F32)
    past = blk < own
    causal = (lax.broadcasted_iota(jnp.int32, (MOBA_BLOCK, items), 0)
              <= lax.broadcasted_iota(jnp.int32, (MOBA_BLOCK, items), 1) % MOBA_BLOCK)
    feat_pad = jnp.zeros((HEAD_DIM - nblk, items), BF16)

    kv_heads = range(N_KV_HEADS)
    for kh in kv_heads:
        qt = jnp.concatenate([head_t(kh * KV_GROUP + i) for i in range(KV_GROUP)], axis=1)
        q_hi, q_lo = _split_bf16(qt)
        k_hi, k_lo = _split_bf16(km_ref[0, kh])
        g_hi = _dot(jnp.concatenate([k_hi, k_lo], axis=0), q_hi)
        gate = g_hi[0:nblk] + (_dot(k_hi, q_lo) + g_hi[nblk:2 * nblk])
        cur = jnp.where(past, gate, -jnp.inf)
        bias = jnp.where(blk == own, 0.0, NEG_INF)
        for _ in range(MOBA_TOPK):
            best = jnp.max(cur, axis=0, keepdims=True)
            cand = jnp.where((cur == best) & (best > -jnp.inf), blk_f, float(nblk))
            pick = blk_f == jnp.min(cand, axis=0, keepdims=True)
            bias = jnp.where(pick, 0.0, bias)
            cur = jnp.where(pick, -jnp.inf, cur)
        q_feat = (qt * LOG2E).astype(BF16)
        qa_ref[kh] = jnp.concatenate([q_feat, bias.astype(BF16), feat_pad], axis=0)
        acc_ref[kh] = jnp.zeros(acc_ref.shape[1:], F32)

    def score_group(kh, gi):
        keys = pl.ds(pl.multiple_of(gi * group_keys, group_keys), group_keys)
        return _dot(ka_ref[kh, 0, keys, :], qa_ref[kh])

    def weights(kh, s_cur_ref, m_prev, own_slot=False):
        if own_slot:
            slabs = []
            for j in range(ATT_GROUP):
                keep = jnp.logical_or(causal, own % ATT_GROUP != j)
                slabs.append(jnp.where(keep, s_cur_ref[kh, j * MOBA_BLOCK:(j + 1) * MOBA_BLOCK, :], NEG_INF))
            s = jnp.concatenate(slabs, axis=0)
        else:
            s = s_cur_ref[kh]
        m_new = jnp.maximum(m_prev, jnp.max(s, axis=0, keepdims=True))
        pb_ref[kh] = jnp.exp2((s - m_new).astype(BF16))
        return m_new, jnp.exp2(m_prev - m_new)

    def accumulate(kh, gi, alpha):
        n0 = gi * ATT_GROUP
        pv = _dot(vt_ref[0, n0, kh], pb_ref[kh, 0:MOBA_BLOCK, :])
        for j in range(1, ATT_GROUP):
            pv = pv + _dot(vt_ref[0, n0 + j, kh], pb_ref[kh, j * MOBA_BLOCK:(j + 1) * MOBA_BLOCK, :])
        acc_ref[kh] = alpha * acc_ref[kh] + pv

    own_group = own // ATT_GROUP

    def softmax_steps(s_cur_ref, gi, ms, before=(None,) * N_KV_HEADS, tail=None, own_slot=False):
        out = []
        alpha_prev = None
        for kh in kv_heads:
            if before[kh] is not None:
                before[kh]()
            if alpha_prev is not None:
                accumulate(kh - 1, gi, alpha_prev)
            m_new, alpha_prev = weights(kh, s_cur_ref, ms[kh], own_slot)
            out.append(m_new)
        if tail is not None:
            tail()
        accumulate(N_KV_HEADS - 1, gi, alpha_prev)
        return tuple(out)

    def scorer(dst_ref, kh, gi):
        def run():
            dst_ref[kh] = score_group(kh, gi)
        return run

    m0 = jnp.full((1, items), SCORE_FLOOR, F32)
    for kh in kv_heads:
        s0_ref[kh] = score_group(kh, 0)

    def group_pair(pi, ms):
        g = 2 * pi
        nxt = [scorer(s0_ref, kh, g + 2) for kh in kv_heads]
        ms = softmax_steps(s0_ref, g, ms, before=[scorer(s1_ref, kh, g + 1) for kh in kv_heads],
                           tail=nxt[0])
        return softmax_steps(s1_ref, g + 1, ms, before=[None] + nxt[1:])

    ms = lax.fori_loop(0, own_group // 2, group_pair, (m0,) * N_KV_HEADS)

    @pl.when(own_group % 2 == 0)
    def _():
        softmax_steps(s0_ref, own_group, ms, own_slot=True)

    @pl.when(own_group % 2 == 1)
    def _():
        mids = softmax_steps(s0_ref, own_group - 1, ms,
                             before=[scorer(s1_ref, kh, own_group) for kh in kv_heads])
        softmax_steps(s1_ref, own_group, mids, own_slot=True)

    outs = []
    for kh in kv_heads:
        ot = acc_ref[kh, 0:HEAD_DIM, :] / acc_ref[kh, HEAD_DIM:HEAD_DIM + 1, :]
        outs.extend(ot[:, i * MOBA_BLOCK:(i + 1) * MOBA_BLOCK].T for i in range(KV_GROUP))

    attn = jnp.concatenate(outs, axis=1).astype(BF16)
    o_ref[0] = x + _dot(attn, wo_ref[...])


def _attn_call(x3, g, w_q, w_o, ka, vt, km, invf_col):
    bsz, seq, _ = x3.shape
    nblk = seq // MOBA_BLOCK
    items = KV_GROUP * MOBA_BLOCK
    x_spec = pl.BlockSpec((1, MOBA_BLOCK, D_MODEL), lambda b, i: (b, i, 0))
    return pl.pallas_call(
        functools.partial(_attn_body, nblk=nblk),
        grid=(bsz, nblk),
        in_specs=[x_spec, _resident((1, D_MODEL)),
                  _resident((D_MODEL, D_MODEL)), _resident((D_MODEL, D_MODEL)),
                  pl.BlockSpec((N_KV_HEADS, 1, seq, 2 * HEAD_DIM), lambda b, i: (0, b, 0, 0),
                               pipeline_mode=pl.Buffered(1)),
                  pl.BlockSpec((1, nblk, N_KV_HEADS, V_ROWS, MOBA_BLOCK),
                               lambda b, i: (b, 0, 0, 0, 0), pipeline_mode=pl.Buffered(1)),
                  pl.BlockSpec((1, N_KV_HEADS, nblk, HEAD_DIM), lambda b, i: (b, 0, 0, 0)),
                  _resident((ROPE_DIM // 2, 1))],
        out_specs=x_spec,
        out_shape=jax.ShapeDtypeStruct(x3.shape, F32),
        scratch_shapes=[
            pltpu.VMEM((N_KV_HEADS, 2 * HEAD_DIM, items), BF16),
            pltpu.VMEM((N_KV_HEADS, V_ROWS, items), F32),
            pltpu.VMEM((N_KV_HEADS, ATT_GROUP * MOBA_BLOCK, items), F32),
            pltpu.VMEM((N_KV_HEADS, ATT_GROUP * MOBA_BLOCK, items), F32),
            pltpu.VMEM((N_KV_HEADS, ATT_GROUP * MOBA_BLOCK, items), BF16)],
        compiler_params=pltpu.CompilerParams(
            dimension_semantics=("arbitrary", "arbitrary"), vmem_limit_bytes=VMEM_LIMIT),
        name="moba_attn",
    )(x3, g.reshape(1, D_MODEL), w_q.astype(BF16), w_o.astype(BF16), ka, vt, km, invf_col)


def kernel(x, norm_g, ffn_w_in, ffn_w_out, s5_a_re, s5_a_im, s5_log_step, s5_b_re, s5_b_im,
           s5_c_re, s5_c_im, s5_d, s5_w_glu, kv_norm_g, w_k, w_v, w_q, w_o, final_g):
    bsz, seq, _ = x.shape
    assert seq % S5_CHUNK == 0 and seq % MOBA_BLOCK == 0
    assert (seq // MOBA_BLOCK) % (2 * SUBLANES) == 0 and seq // MOBA_BLOCK <= HEAD_DIM
    assert (seq // MOBA_BLOCK) % ATT_GROUP == 0
    assert (bsz * seq) % FFN_ROWS == 0 and FFN_ROWS % MOBA_BLOCK == 0
    nblk = seq // MOBA_BLOCK
    x2 = x.reshape(bsz * seq, D_MODEL)
    invf, rope_tab = _rope_consts()

    wkv = jnp.concatenate([w_k, w_v], axis=1).astype(BF16)

    x2 = _ffn_call(x2, norm_g[0, 0], ffn_w_in, ffn_w_out, 0, 0)
    prep = _s5_prep_call(s5_a_re[0], s5_a_im[0], s5_log_step[0], s5_b_re[0], s5_b_im[0], s5_c_im[0])
    x2 = _s5_call(x2, bsz, seq, norm_g[0, 1], prep, s5_c_re[0], s5_d[0], s5_w_glu[0])
    x2, ka, vt, km = _ffn_call(x2, norm_g[0, 2], ffn_w_in, ffn_w_out, 0, 1,
                               kv=(kv_norm_g, wkv, invf, rope_tab), nblk=nblk)
    ka = ka.reshape(N_KV_HEADS, bsz, seq, 2 * HEAD_DIM)
    vt = vt.reshape(bsz, nblk, N_KV_HEADS, V_ROWS, MOBA_BLOCK)
    km = km.reshape(bsz, nblk, N_KV_HEADS, HEAD_DIM).transpose(0, 2, 1, 3)

    x2 = _ffn_call(x2, norm_g[1, 0], ffn_w_in, ffn_w_out, 1, 0)
    x3 = _attn_call(x2.reshape(bsz, seq, D_MODEL), norm_g[1, 1], w_q[0], w_o[0], ka, vt, km,
                    invf[0, 0:ROPE_DIM // 2].reshape(ROPE_DIM // 2, 1))
    x2 = _ffn_call(x3.reshape(bsz * seq, D_MODEL), norm_g[1, 2], ffn_w_in, ffn_w_out, 1, 1,
                   final_g=final_g)
    return x2.reshape(bsz, seq, D_MODEL)
```
